```python
import math, functools
import jax, jax.numpy as jnp
from jax import lax
import numpy as np

D_MODEL = 1024
BATCH = 2
SEQ = 8192
DEPTH = 4
DEC_BATCH = 128
DEC_SEQ = 4
PAST_LEN = 8192
PAGE_SIZE = 128

MIX_WIDTH = D_MODEL
GLA_WIDTH = D_MODEL // 2
GLA_HEADS = 4
GLA_DV = GLA_WIDTH // GLA_HEADS
GLA_DK = GLA_DV // 2
GLA_QK = GLA_HEADS * GLA_DK
GATE_RANK = 16
GATE_NORMALIZER = 16.0
GLA_CHUNK = 64
SWA_HEAD_DIM = 64
SWA_HEADS = (MIX_WIDTH - GLA_WIDTH) // SWA_HEAD_DIM
SWA_KV_HEADS = SWA_HEADS // 4
SWA_GROUP = SWA_HEADS // SWA_KV_HEADS
WINDOW = 128
ROPE_THETA = 10000.0
D_FF = ((8 * D_MODEL // 3 + 255) // 256) * 256
NORM_EPS = 1e-6

_IN_SIZES = (GLA_QK, GLA_QK, GLA_WIDTH, GATE_RANK, GLA_WIDTH,
             SWA_HEADS * SWA_HEAD_DIM, SWA_KV_HEADS * SWA_HEAD_DIM, SWA_KV_HEADS * SWA_HEAD_DIM)
IN_COLS = sum(_IN_SIZES)
_SPLITS = tuple(int(s) for s in np.cumsum(_IN_SIZES)[:-1])

kernel_name = "hymba_gla_swa_sink_decoder_step"


def _rmsnorm(x, g):
    xf = x.astype(jnp.float32)
    y = xf * lax.rsqrt(jnp.mean(xf * xf, axis=-1, keepdims=True) + NORM_EPS)
    return (y * g.astype(jnp.float32)).astype(x.dtype)


def _rope(x, pos):
    half = x.shape[-1] // 2
    inv = ROPE_THETA ** (-jnp.arange(half, dtype=jnp.float32) / half)
    ang = pos.astype(jnp.float32)[:, None] * inv[None, :]
    cos = jnp.cos(ang)[None, :, None, :]
    sin = jnp.sin(ang)[None, :, None, :]
    xf = x.astype(jnp.float32)
    x1, x2 = xf[..., :half], xf[..., half:]
    return jnp.concatenate([x1 * cos - x2 * sin, x2 * cos + x1 * sin], axis=-1).astype(x.dtype)


def _gla(q, k, v, g, S0):
    B, T, H, K = q.shape
    V = v.shape[-1]
    C = math.gcd(T, GLA_CHUNK)
    n = T // C
    causal = jnp.tril(jnp.ones((C, C), dtype=bool))[None, :, :, None, None]

    def to_chunks(a):
        return a.reshape(B, n, C, H, a.shape[-1]).swapaxes(0, 1)

    def step(S, inp):
        qc, kc, vc, gc = inp
        b = jnp.cumsum(gc, axis=1)
        o_inter = jnp.einsum('bthk,bhkv->bthv', qc * jnp.exp(b), S)
        diff = b[:, :, None] - b[:, None, :]
        decay = jnp.where(causal, jnp.exp(jnp.where(causal, diff, 0.0)), 0.0)
        A = jnp.einsum('bthk,btshk,bshk->bhts', qc, decay, kc)
        o_intra = jnp.einsum('bhts,bshv->bthv', A, vc)
        b_last = b[:, -1]
        k_dec = kc * jnp.exp(b_last[:, None] - b)
        S_new = jnp.exp(b_last)[..., None] * S + jnp.einsum('bshk,bshv->bhkv', k_dec, vc)
        return S_new, o_inter + o_intra

    S, o = lax.scan(step, S0, (to_chunks(q), to_chunks(k), to_chunks(v), to_chunks(g)))
    return o.swapaxes(0, 1).reshape(B, T, H, V), S


def _sink_attention(q, k, v, sinks, qpos, kpos):
    hd = q.shape[-1]
    s = jnp.einsum('bnqkgd,bnskd->bnkgqs', q.astype(jnp.float32), k.astype(jnp.float32)) * (hd ** -0.5)
    diff = qpos[:, :, None] - kpos[:, None, :]
    mask = (diff >= 0) & (diff < WINDOW) & (kpos[:, None, :] >= 0)
    s = jnp.where(mask[None, :, None, None], s, -jnp.inf)
    sink = sinks.astype(jnp.float32).reshape(1, 1, SWA_KV_HEADS, SWA_GROUP, 1, 1)
    m = jnp.maximum(jnp.max(s, axis=-1, keepdims=True), sink)
    p = jnp.exp(s - m)
    denom = jnp.sum(p, axis=-1, keepdims=True) + jnp.exp(sink - m)
    return jnp.einsum('bnkgqs,bnskd->bnqkgd', p / denom, v.astype(jnp.float32))


def _swa_prompt(q, k, v, sinks):
    B, T = q.shape[0], q.shape[1]
    N = T // WINDOW
    qb = q.reshape(B, N, WINDOW, SWA_KV_HEADS, SWA_GROUP, SWA_HEAD_DIM)
    kb = k.reshape(B, N, WINDOW, SWA_KV_HEADS, SWA_HEAD_DIM)
    vb = v.reshape(B, N, WINDOW, SWA_KV_HEADS, SWA_HEAD_DIM)
    kk = jnp.concatenate([jnp.concatenate([jnp.zeros_like(kb[:, :1]), kb[:, :-1]], axis=1), kb], axis=2)
    vv = jnp.concatenate([jnp.concatenate([jnp.zeros_like(vb[:, :1]), vb[:, :-1]], axis=1), vb], axis=2)
    qpos = jnp.arange(T).reshape(N, WINDOW)
    kpos = (jnp.arange(N)[:, None] - 1) * WINDOW + jnp.arange(2 * WINDOW)[None, :]
    o = _sink_attention(qb, kk, vv, sinks, qpos, kpos)
    w_buf = min(WINDOW, PAST_LEN)
    return o.reshape(B, T, SWA_HEADS * SWA_HEAD_DIM), k[:, -w_buf:], v[:, -w_buf:]


def _swa_sample(q, k, v, sinks, kbuf, vbuf):
    B, T = q.shape[0], q.shape[1]
    W = kbuf.shape[1]
    keys = jnp.concatenate([kbuf.astype(k.dtype), k], axis=1)
    vals = jnp.concatenate([vbuf.astype(v.dtype), v], axis=1)
    qpos = (PAST_LEN + jnp.arange(T))[None]
    kpos = (PAST_LEN - W + jnp.arange(W + T))[None]
    o = _sink_attention(q.reshape(B, 1, T, SWA_KV_HEADS, SWA_GROUP, SWA_HEAD_DIM),
                        keys[:, None], vals[:, None], sinks, qpos, kpos)
    return o.reshape(B, T, SWA_HEADS * SWA_HEAD_DIM), keys[:, -W:], vals[:, -W:]


def _layer(x, pos, S0, attend, norm_attn, w_in, w_gk2, b_gk2, gla_norm, sinks, w_o,
           norm_ffn, w_gate, w_up, w_down):
    B, T, _ = x.shape
    f32 = jnp.float32
    h = _rmsnorm(x, norm_attn)
    z = h @ w_in
    gq, gk, gv, ga, gg, sq, sk, sv = jnp.split(z, _SPLITS, axis=-1)
    q = gq.reshape(B, T, GLA_HEADS, GLA_DK).astype(f32) * (GLA_DK ** -0.5)
    k = gk.reshape(B, T, GLA_HEADS, GLA_DK).astype(f32)
    v = gv.reshape(B, T, GLA_HEADS, GLA_DV).astype(f32)
    g = jax.nn.log_sigmoid((ga @ w_gk2 + b_gk2).astype(f32)) / GATE_NORMALIZER
    o, S = _gla(q, k, v, g.reshape(B, T, GLA_HEADS, GLA_DK), S0.astype(f32))
    o = _rmsnorm(o, gla_norm) * jax.nn.silu(gg.reshape(B, T, GLA_HEADS, GLA_DV).astype(f32))
    o_gla = o.reshape(B, T, GLA_WIDTH).astype(x.dtype)
    qs = _rope(sq.reshape(B, T, SWA_HEADS, SWA_HEAD_DIM), pos)
    ks = _rope(sk.reshape(B, T, SWA_KV_HEADS, SWA_HEAD_DIM), pos)
    vs = sv.reshape(B, T, SWA_KV_HEADS, SWA_HEAD_DIM)
    o_swa, kc, vc = attend(qs, ks, vs, sinks)
    x = x + jnp.concatenate([o_gla, o_swa.astype(x.dtype)], axis=-1) @ w_o
    h = _rmsnorm(x, norm_ffn)
    x = x + (jax.nn.silu(h @ w_gate) * (h @ w_up)) @ w_down
    return x, S, kc, vc


def setup_inputs(seed: int = 0) -> dict:
    key = jax.random.key(seed)
    ks = jax.random.split(key, 20)
    nrm = jax.random.normal
    w_buf = min(WINDOW, PAST_LEN)
    return {
        "x_prompt": nrm(ks[0], (BATCH, SEQ, D_MODEL), jnp.float32),
        "x_sample": nrm(ks[1], (DEC_BATCH, DEC_SEQ, D_MODEL), jnp.float32),
        "state_gla": 0.5 * nrm(ks[2], (DEPTH, DEC_BATCH, GLA_HEADS, GLA_DK, GLA_DV), jnp.float32),
        "cache_swa_k": nrm(ks[3], (DEPTH, DEC_BATCH, w_buf, SWA_KV_HEADS, SWA_HEAD_DIM), jnp.float32),
        "cache_swa_v": nrm(ks[4], (DEPTH, DEC_BATCH, w_buf, SWA_KV_HEADS, SWA_HEAD_DIM), jnp.float32),
        "norm_attn": 1.0 + 0.01 * nrm(ks[5], (DEPTH, D_MODEL), jnp.float32),
        "w_in": nrm(ks[6], (DEPTH, D_MODEL, IN_COLS), jnp.float32) * D_MODEL ** -0.5,
        "w_gk2": nrm(ks[7], (DEPTH, GATE_RANK, GLA_QK), jnp.float32) * GATE_RANK ** -0.5,
        "b_gk2": 0.1 * nrm(ks[8], (DEPTH, GLA_QK), jnp.float32),
        "gla_norm": 1.0 + 0.01 * nrm(ks[9], (DEPTH, GLA_DV), jnp.float32),
        "attn_sinks": 0.5 * nrm(ks[10], (DEPTH, SWA_HEADS), jnp.float32),
        "w_o": nrm(ks[11], (DEPTH, MIX_WIDTH, D_MODEL), jnp.float32) * MIX_WIDTH ** -0.5,
        "norm_ffn": 1.0 + 0.01 * nrm(ks[12], (DEPTH, D_MODEL), jnp.float32),
        "w_gate": nrm(ks[13], (DEPTH, D_MODEL, D_FF), jnp.float32) * D_MODEL ** -0.5,
        "w_up": nrm(ks[14], (DEPTH, D_MODEL, D_FF), jnp.float32) * D_MODEL ** -0.5,
        "w_down": nrm(ks[15], (DEPTH, D_FF, D_MODEL), jnp.float32) * D_FF ** -0.5,
        "norm_final": 1.0 + 0.01 * nrm(ks[16], (D_MODEL,), jnp.float32),
    }


def reference(x_prompt, x_sample, state_gla, cache_swa_k, cache_swa_v, norm_attn, w_in, w_gk2,
              b_gk2, gla_norm, attn_sinks, w_o, norm_ffn, w_gate, w_up, w_down, norm_final):
    pos_p = jnp.arange(x_prompt.shape[1])
    pos_s = PAST_LEN + jnp.arange(x_sample.shape[1])
    yp, ys = x_prompt, x_sample
    Sp_l, kp_l, vp_l, Ss_l, ks_l, vs_l = [], [], [], [], [], []
    for l in range(DEPTH):
        w = (norm_attn[l], w_in[l], w_gk2[l], b_gk2[l], gla_norm[l], attn_sinks[l], w_o[l],
             norm_ffn[l], w_gate[l], w_up[l], w_down[l])
        S0 = jnp.zeros((yp.shape[0], GLA_HEADS, GLA_DK, GLA_DV), jnp.float32)
        yp, Sp, kp, vp = _layer(yp, pos_p, S0, _swa_prompt, *w)
        attend_s = functools.partial(_swa_sample, kbuf=cache_swa_k[l], vbuf=cache_swa_v[l])
        ys, Ss, kss, vss = _layer(ys, pos_s, state_gla[l], attend_s, *w)
        Sp_l.append(Sp.astype(state_gla.dtype)); kp_l.append(kp.astype(cache_swa_k.dtype))
        vp_l.append(vp.astype(cache_swa_v.dtype)); Ss_l.append(Ss.astype(state_gla.dtype))
        ks_l.append(kss.astype(cache_swa_k.dtype)); vs_l.append(vss.astype(cache_swa_v.dtype))
    y_prompt = _rmsnorm(yp, norm_final)
    y_sample = _rmsnorm(ys, norm_final)
    return (y_prompt, y_sample, jnp.stack(Sp_l), jnp.stack(kp_l), jnp.stack(vp_l),
            jnp.stack(Ss_l), jnp.stack(ks_l), jnp.stack(vs_l))
```

```python
import functools

import jax
import jax.numpy as jnp
import numpy as np
from jax import lax
from jax.experimental import pallas as pl
from jax.experimental.pallas import tpu as pltpu

F32 = jnp.float32
BF16 = jnp.bfloat16

D_MODEL = 1024
DEPTH = 4
PAST_LEN = 8192
GLA_HEADS = 4
GLA_DV = 128
GLA_DK = 64
GLA_QK = GLA_HEADS * GLA_DK
GLA_WIDTH = GLA_HEADS * GLA_DV
GATE_RANK = 16
GATE_NORMALIZER = 16.0
SWA_HEAD_DIM = 64
SWA_HEADS = 8
SWA_KV_HEADS = 2
SWA_GROUP = 4
SWA_Q = SWA_HEADS * SWA_HEAD_DIM
SWA_KV = SWA_KV_HEADS * SWA_HEAD_DIM
WINDOW = 128
ROPE_THETA = 10000.0
D_FF = 2816
NORM_EPS = 1e-6

LANES = 128
SUBLANES = 8
VMEM_LIMIT = 56 * 1024 * 1024

C_Q, C_K, C_V, C_GG, C_SQ, C_SK, C_SV, C_GA = 0, 256, 512, 1024, 1536, 2048, 2176, 2304
IN_COLS_P = C_GA + LANES

TOK_TILE = 512
GLA_CHUNK = 64
GLA_SUB = 8
GLA_TILE = 512
FF_CHUNK = D_FF
SAMPLE_BT = 8
NEG = -1e30


def _rms(x, w):
    ms = jnp.mean(x * x, axis=-1, keepdims=True)
    return x * lax.rsqrt(ms + NORM_EPS) * w


def _sigmoid(x):
    return 1.0 / (1.0 + jnp.exp(-x))


def _dot(a, b):
    return jnp.dot(a, b, preferred_element_type=F32)


def _dot_nt(a, b):
    return lax.dot_general(a, b, (((1,), (1,)), ((), ())), preferred_element_type=F32)


def _in_proj_kernel(x_ref, nw_ref, w_ref, wg2_ref, bg2_ref, cos_ref, sa_ref, sb_ref,
                    q_ref, k_ref, g_ref, v_ref, gg_ref, sq_ref, sk_ref, sv_ref):
    h = _rms(x_ref[...], nw_ref[...]).astype(BF16)

    def proj(c0, c1):
        return _dot(h, w_ref[:, c0:c1])

    q_ref[...] = proj(C_Q, C_K) * (GLA_DK ** -0.5)
    k_ref[...] = proj(C_K, C_V)
    v_ref[...] = proj(C_V, C_GG).astype(BF16)
    gg_ref[...] = proj(C_GG, C_SQ)
    ga = proj(C_GA, IN_COLS_P).astype(BF16)
    gx = _dot(ga, wg2_ref[...]) + bg2_ref[...]
    g_ref[...] = (jnp.minimum(gx, 0.0) - jnp.log1p(jnp.exp(-jnp.abs(gx)))) * (1.0 / GATE_NORMALIZER)

    cos, sa, sb = cos_ref[...], sa_ref[...], sb_ref[...]

    def rope(z):
        return z * cos + pltpu.roll(z, LANES - 32, 1) * sa + pltpu.roll(z, 32, 1) * sb

    for j in range(SWA_Q // LANES):
        zq = proj(C_SQ + j * LANES, C_SQ + (j + 1) * LANES)
        sq_ref[:, j * LANES:(j + 1) * LANES] = (rope(zq) * (SWA_HEAD_DIM ** -0.5)).astype(BF16)
    sk_ref[...] = rope(proj(C_SK, C_SV))
    sv_ref[...] = proj(C_SV, C_GA)


def _in_proj(x, nw, w, wg2, bg2, tabs, layer):
    n = x.shape[0]
    nt = n // TOK_TILE
    cos, sa, sb = tabs
    ntab = cos.shape[0] // TOK_TILE
    row = lambda w_: pl.BlockSpec((TOK_TILE, w_), lambda i: (i, 0))
    tab = pl.BlockSpec((TOK_TILE, LANES), lambda i: (i % ntab, 0))
    const = lambda shp: pl.BlockSpec((None,) + shp, lambda i: (layer, 0, 0), pipeline_mode=pl.Buffered(1))
    outs = [(GLA_QK, F32), (GLA_QK, F32), (GLA_QK, F32), (GLA_WIDTH, BF16), (GLA_WIDTH, F32),
            (SWA_Q, BF16), (SWA_KV, F32), (SWA_KV, F32)]
    return pl.pallas_call(
        _in_proj_kernel,
        grid=(nt,),
        in_specs=[row(D_MODEL), const((1, D_MODEL)), const((D_MODEL, IN_COLS_P)),
                  const((LANES, GLA_QK)), const((1, GLA_QK)), tab, tab, tab],
        out_specs=[row(w_) for w_, _ in outs],
        out_shape=[jax.ShapeDtypeStruct((n, w_), dt) for w_, dt in outs],
        compiler_params=pltpu.CompilerParams(dimension_semantics=("arbitrary",),
                                             vmem_limit_bytes=VMEM_LIMIT),
        name="in_proj",
    )(x, nw, w, wg2, bg2, cos, sa, sb)


def _gla_kernel(q_ref, k_ref, g_ref, v_ref, gg_ref, gn_ref, o_ref, sout_ref, s_ref):
    ti = pl.program_id(2)
    C, SB = GLA_CHUNK, GLA_SUB
    NB = C // SB

    @pl.when(ti == 0)
    def _():
        s_ref[...] = jnp.zeros_like(s_ref)

    row = lax.broadcasted_iota(jnp.int32, (C, LANES), 0)
    ltri = (lax.broadcasted_iota(jnp.int32, (C, C), 0) >= lax.broadcasted_iota(jnp.int32, (C, C), 1)
            ).astype(F32).astype(BF16)
    head_a1 = lax.broadcasted_iota(jnp.int32, (C, (NB - 1) * LANES), 1) % LANES < GLA_DK
    head_a2 = lax.broadcasted_iota(jnp.int32, (C, NB * LANES), 1) % LANES < GLA_DK
    col_s = lax.broadcasted_iota(jnp.int32, (C, LANES), 1) % C
    blockdiag = (row // SB) == (col_s // SB)
    r2 = lax.broadcasted_iota(jnp.int32, (2 * C, 2 * GLA_DV), 0)
    c2 = lax.broadcasted_iota(jnp.int32, (2 * C, 2 * GLA_DV), 1)
    pair_diag = (r2 < C) == (c2 < GLA_DV)
    gn = gn_ref[...]

    def sub_bcast(b, j):
        b3 = b.reshape(NB, SB, LANES)
        return jnp.broadcast_to(b3[:, j:j + 1, :], (NB, SB, LANES)).reshape(C, LANES)

    def chunk(c, carry):
        rows = pl.ds(pl.multiple_of(c * C, C), C)
        q = q_ref[rows, :]
        k = k_ref[rows, :]
        g = g_ref[rows, :]
        v = v_ref[rows, :]
        g_hi = g.astype(BF16)
        g_lo = (g - g_hi.astype(F32)).astype(BF16)
        b = _dot(ltri, g_hi) + _dot(ltri, g_lo)
        b_last = b[C - 1:C, :]

        k1 = k * jnp.exp(sub_bcast(b, SB - 1) - b)
        lhs, rhs = [], []
        for j in range(NB - 1):
            e = b[SB * j + SB - 1:SB * j + SB, :]
            lhs.append((q * jnp.exp(jnp.where(row >= SB * (j + 1), b - e, NEG))).astype(BF16))
            rhs.append(jnp.where(row // SB == j, k1, 0.0))
        rh = jnp.concatenate(rhs, axis=1)
        r1 = jnp.concatenate([jnp.where(head_a1, rh, 0.0), jnp.where(head_a1, 0.0, rh)], axis=0)
        a = _dot_nt(jnp.concatenate(lhs, axis=1), r1.astype(BF16))

        lhs, rhs = [], []
        for jj in range(SB):
            e = sub_bcast(b, jj)
            lhs.append((q * jnp.exp(jnp.where(row % SB >= jj, b - e, NEG))).astype(BF16))
            rhs.append(jnp.where(row % SB == jj, k, 0.0))
        rh = jnp.concatenate(rhs, axis=1)
        r2_ = jnp.concatenate([jnp.where(head_a2, rh, 0.0), jnp.where(head_a2, 0.0, rh)], axis=0)
        a = a + jnp.where(blockdiag, _dot_nt(jnp.concatenate(lhs, axis=1), r2_.astype(BF16)), 0.0)

        s_old = s_ref[...]
        vf = v.astype(F32)
        v_bd = jnp.where(pair_diag, jnp.concatenate([vf, vf], axis=0), 0.0)
        lo = jnp.concatenate([(q * jnp.exp(b)).astype(BF16), a.astype(BF16)], axis=1)
        ro = jnp.concatenate([s_old.astype(BF16), v_bd.astype(BF16)], axis=0)
        o = _dot(lo, ro)

        kd = k * jnp.exp(b_last - b)
        mt = jnp.concatenate([kd, jnp.broadcast_to(b_last, (C, LANES))], axis=0).T
        upd = _dot(mt[:, :C].astype(BF16), v)
        s_ref[...] = jnp.exp(mt[:, C:C + 1]) * s_old + jnp.where(pair_diag, upd, 0.0)

        gg = gg_ref[rows, :]
        outs = []
        for hh in range(2):
            oh = o[:, hh * GLA_DV:(hh + 1) * GLA_DV]
            outs.append(_rms(oh, gn))
        y = jnp.concatenate(outs, axis=1) * (gg * _sigmoid(gg))
        o_ref[rows, :] = y.astype(BF16)
        return carry

    lax.fori_loop(0, GLA_TILE // C, chunk, 0)

    @pl.when(ti == pl.num_programs(2) - 1)
    def _():
        s = s_ref[...]
        sout_ref[0, 0] = s[:GLA_DK, :GLA_DV]
        sout_ref[0, 1] = s[GLA_DK:, GLA_DV:]


def _gla_prompt(q, k, g, v, gg, gn, batch, seq):
    nt = seq // GLA_TILE
    npair = GLA_HEADS // 2
    blk = lambda w_: pl.BlockSpec((GLA_TILE, w_), lambda b, p, t: (b * nt + t, p))
    return pl.pallas_call(
        _gla_kernel,
        grid=(batch, npair, nt),
        in_specs=[blk(LANES), blk(LANES), blk(LANES), blk(2 * GLA_DV), blk(2 * GLA_DV),
                  pl.BlockSpec((1, GLA_DV), lambda b, p, t: (0, 0))],
        out_specs=[blk(2 * GLA_DV),
                   pl.BlockSpec((1, 2, GLA_DK, GLA_DV), lambda b, p, t: (b, p, 0, 0))],
        out_shape=[jax.ShapeDtypeStruct((batch * seq, GLA_WIDTH), BF16),
                   jax.ShapeDtypeStruct((batch, GLA_HEADS, GLA_DK, GLA_DV), F32)],
        scratch_shapes=[pltpu.VMEM((2 * GLA_DK, 2 * GLA_DV), F32)],
        compiler_params=pltpu.CompilerParams(
            dimension_semantics=("arbitrary", "arbitrary", "arbitrary"), vmem_limit_bytes=VMEM_LIMIT),
        name="gla_prompt",
    )(q, k, g, v, gg, gn)


def _swa_kernel(sink_ref, q_ref, kp_ref, kc_ref, vp_ref, vc_ref, o_ref):
    n = pl.program_id(1)
    W = WINDOW
    kk = jnp.concatenate([kp_ref[...], kc_ref[...]], axis=0)
    vv = jnp.concatenate([vp_ref[...], vc_ref[...]], axis=0)
    kk_sw = pltpu.roll(kk, SWA_HEAD_DIM, 1)
    vv_sw = pltpu.roll(vv, SWA_HEAD_DIM, 1)
    lane = lax.broadcasted_iota(jnp.int32, (2 * W, LANES), 1)
    lo = lane < SWA_HEAD_DIM

    def ext(x, x_sw, kv):
        a = jnp.where(lo, x if kv == 0 else x_sw, 0.0)
        b = jnp.where(lo, 0.0, x_sw if kv == 0 else x)
        return jnp.concatenate([a, b], axis=0).astype(BF16)

    qi = lax.broadcasted_iota(jnp.int32, (W, 2 * W), 0)
    kj = lax.broadcasted_iota(jnp.int32, (W, 2 * W), 1)
    mask = (kj > qi) & (kj <= qi + W) & ((kj >= W) | (n > 0))
    lane_q = lax.broadcasted_iota(jnp.int32, (W, LANES), 1) < SWA_HEAD_DIM

    for kv in range(SWA_KV_HEADS):
        k_ext = ext(kk, kk_sw, kv)
        v_ext = ext(vv, vv_sw, kv)
        for pp in range(SWA_GROUP // 2):
            p = kv * (SWA_GROUP // 2) + pp
            s = _dot_nt(q_ref[:, p * LANES:(p + 1) * LANES], k_ext)
            ps, rden = [], []
            for hh in range(2):
                sink = sink_ref[2 * p + hh]
                sh = jnp.where(mask, s[:, hh * 2 * W:(hh + 1) * 2 * W], NEG)
                m = jnp.maximum(jnp.max(sh, axis=-1, keepdims=True), sink)
                e = jnp.exp(sh - m)
                den = jnp.sum(e, axis=-1, keepdims=True) + jnp.exp(sink - m)
                ps.append(e.astype(BF16))
                rden.append(1.0 / den)
            o = _dot(jnp.concatenate(ps, axis=1), v_ext)
            o = o * jnp.where(lane_q, rden[0], rden[1])
            o_ref[:, p * LANES:(p + 1) * LANES] = o.astype(BF16)


def _swa_prompt(sinks, sq, sk, sv, batch, seq):
    nb = seq // WINDOW
    cur = lambda b, n: (b * nb + n, 0)
    prev = lambda b, n: (b * nb + jnp.maximum(n - 1, 0), 0)
    kvspec = lambda im: pl.BlockSpec((WINDOW, SWA_KV), im)
    return pl.pallas_call(
        _swa_kernel,
        grid=(batch, nb),
        in_specs=[pl.BlockSpec(memory_space=pltpu.SMEM),
                  pl.BlockSpec((WINDOW, SWA_Q), cur),
                  kvspec(prev), kvspec(cur), kvspec(prev), kvspec(cur)],
        out_specs=pl.BlockSpec((WINDOW, SWA_Q), cur),
        out_shape=jax.ShapeDtypeStruct((batch * seq, SWA_Q), BF16),
        compiler_params=pltpu.CompilerParams(dimension_semantics=("arbitrary", "arbitrary"),
                                             vmem_limit_bytes=VMEM_LIMIT),
        name="swa_prompt",
    )(sinks, sq, sk, sk, sv, sv)


def _out_ffn_kernel(x_ref, og_ref, os_ref, wo_ref, nw_ref, wg_ref, wu_ref, wd_ref, nf_ref, y_ref, *, final):
    x = (x_ref[...] + _dot(og_ref[...], wo_ref[:GLA_WIDTH, :]) + _dot(os_ref[...], wo_ref[GLA_WIDTH:, :]))
    h = _rms(x, nw_ref[...]).astype(BF16)
    acc = x
    for c0 in range(0, D_FF, FF_CHUNK):
        a = _dot(h, wg_ref[:, c0:c0 + FF_CHUNK])
        u = _dot(h, wu_ref[:, c0:c0 + FF_CHUNK])
        acc = acc + _dot((a * _sigmoid(a) * u).astype(BF16), wd_ref[c0:c0 + FF_CHUNK, :])
    if final:
        acc = _rms(acc, nf_ref[...])
    y_ref[...] = acc


def _out_ffn(x, og, osw, wo, nw, wg, wu, wd, nf, layer, final):
    n = x.shape[0]
    row = lambda w_: pl.BlockSpec((TOK_TILE, w_), lambda i: (i, 0))
    const = lambda shp: pl.BlockSpec((None,) + shp, lambda i: (layer, 0, 0), pipeline_mode=pl.Buffered(1))
    return pl.pallas_call(
        functools.partial(_out_ffn_kernel, final=final),
        grid=(n // TOK_TILE,),
        in_specs=[row(D_MODEL), row(GLA_WIDTH), row(SWA_Q), const((D_MODEL, D_MODEL)), const((1, D_MODEL)),
                  const((D_MODEL, D_FF)), const((D_MODEL, D_FF)), const((D_FF, D_MODEL)),
                  pl.BlockSpec((1, D_MODEL), lambda i: (0, 0))],
        out_specs=row(D_MODEL),
        out_shape=jax.ShapeDtypeStruct((n, D_MODEL), F32),
        compiler_params=pltpu.CompilerParams(dimension_semantics=("arbitrary",),
                                             vmem_limit_bytes=VMEM_LIMIT),
        name="out_ffn",
    )(x, og, osw, wo, nw, wg, wu, wd, nf)


def _sample_kernel(q_ref, k_ref, g_ref, kt_ref, gt_ref, v_ref, gg_ref, s0_ref, gn_ref,
                   qx_ref, kn_ref, vn_ref, ck_ref, cv_ref, sink_ref,
                   og_ref, sn_ref, os_ref, cko_ref, cvo_ref, *, nt):
    TP = SUBLANES
    q, k, g, v = q_ref[...], k_ref[...], g_ref[...], v_ref[...]
    s0 = s0_ref[...]
    tt = lax.broadcasted_iota(jnp.int32, (1, TP, 1), 1)
    b = jnp.zeros_like(g)
    for s in range(nt):
        b = b + jnp.where(tt >= s, g[:, s:s + 1, :], 0.0)
    o = jnp.einsum('gtk,gkv->gtv', (q * jnp.exp(b)).astype(BF16), s0.astype(BF16),
                   preferred_element_type=F32)
    for s in range(nt):
        dec = jnp.exp(jnp.where(tt >= s, b - b[:, s:s + 1, :], NEG))
        a = jnp.sum(q * k[:, s:s + 1, :] * dec, axis=-1, keepdims=True)
        o = o + a * v[:, s:s + 1, :]
    gg = gg_ref[...]
    og_ref[...] = _rms(o, gn_ref[...]) * (gg * _sigmoid(gg))

    kt, gt = kt_ref[...], gt_ref[...]
    cum = [gt[:, :, 0:1]]
    for s in range(1, nt):
        cum.append(cum[-1] + gt[:, :, s:s + 1])
    s_new = jnp.exp(cum[-1]) * s0
    for s in range(nt):
        s_new = s_new + (kt[:, :, s:s + 1] * jnp.exp(cum[-1] - cum[s])) * v[:, s:s + 1, :]
    sn_ref[...] = s_new

    W = WINDOW
    qx = qx_ref[...]
    ck, cv = ck_ref[...], cv_ref[...]
    kn, vn = kn_ref[...], vn_ref[...]
    sink = sink_ref[...]
    nrow = SWA_KV_HEADS * nt * SWA_GROUP
    r = lax.broadcasted_iota(jnp.int32, (1, nrow, W), 1)
    tq = (r % (nt * SWA_GROUP)) // SWA_GROUP
    j = lax.broadcasted_iota(jnp.int32, (1, nrow, W), 2)
    sc = jnp.einsum('bqd,bsd->bqs', qx, ck.astype(BF16), preferred_element_type=F32)
    sc = jnp.where(j > tq, sc, NEG)
    tq1 = tq[:, :, 0:1]
    qf = qx.astype(F32)
    sn = [jnp.where(tq1 >= s, jnp.sum(qf * kn[:, s:s + 1, :], axis=-1, keepdims=True), NEG)
          for s in range(nt)]
    m = jnp.maximum(jnp.max(sc, axis=-1, keepdims=True), sink)
    for s in range(nt):
        m = jnp.maximum(m, sn[s])
    pc = jnp.exp(sc - m)
    pn = [jnp.exp(sn[s] - m) for s in range(nt)]
    den = jnp.sum(pc, axis=-1, keepdims=True) + jnp.exp(sink - m)
    for s in range(nt):
        den = den + pn[s]
    o = jnp.einsum('bqs,bsd->bqd', pc.astype(BF16), cv.astype(BF16), preferred_element_type=F32)
    for s in range(nt):
        o = o + pn[s] * vn[:, s:s + 1, :]
    os_ref[...] = o * (1.0 / den)

    cko_ref[:, :W - nt, :] = ck[:, nt:, :]
    cko_ref[:, W - nt:, :] = kn
    cvo_ref[:, :W - nt, :] = cv[:, nt:, :]
    cvo_ref[:, W - nt:, :] = vn


def _sample_mix(q4, k4, g4, kt, gt, v4, gg4, state, gn, qx, kn, vn, ck, cv, sink_rows, layer, nbatch, nt):
    bt = SAMPLE_BT
    G = bt * GLA_HEADS
    TP = SUBLANES
    nrow = SWA_KV_HEADS * nt * SWA_GROUP
    b3 = lambda a, c: pl.BlockSpec((G, a, c), lambda i: (i, 0, 0))
    s3 = lambda a, c: pl.BlockSpec((bt, a, c), lambda i: (i, 0, 0))
    lay_g = pl.BlockSpec((None, G, GLA_DK, GLA_DV), lambda i: (layer, i, 0, 0))
    lay_c = pl.BlockSpec((None, bt, WINDOW, SWA_KV), lambda i: (layer, i, 0, 0))
    ng = nbatch * GLA_HEADS
    return pl.pallas_call(
        functools.partial(_sample_kernel, nt=nt),
        grid=(nbatch // bt,),
        in_specs=[b3(TP, GLA_DK), b3(TP, GLA_DK), b3(TP, GLA_DK), b3(GLA_DK, nt), b3(GLA_DK, nt),
                  b3(TP, GLA_DV), b3(TP, GLA_DV), lay_g,
                  pl.BlockSpec((1, 1, GLA_DV), lambda i: (0, 0, 0)),
                  s3(nrow, SWA_KV), s3(nt, SWA_KV), s3(nt, SWA_KV), lay_c, lay_c,
                  pl.BlockSpec((1, nrow, 1), lambda i: (0, 0, 0))],
        out_specs=[b3(TP, GLA_DV), b3(GLA_DK, GLA_DV), s3(nrow, SWA_KV), s3(WINDOW, SWA_KV), s3(WINDOW, SWA_KV)],
        out_shape=[jax.ShapeDtypeStruct((ng, TP, GLA_DV), F32),
                   jax.ShapeDtypeStruct((ng, GLA_DK, GLA_DV), F32),
                   jax.ShapeDtypeStruct((nbatch, nrow, SWA_KV), F32),
                   jax.ShapeDtypeStruct((nbatch, WINDOW, SWA_KV), F32),
                   jax.ShapeDtypeStruct((nbatch, WINDOW, SWA_KV), F32)],
        compiler_params=pltpu.CompilerParams(dimension_semantics=("arbitrary",),
                                             vmem_limit_bytes=VMEM_LIMIT),
        name="sample_mix",
    )(q4, k4, g4, kt, gt, v4, gg4, state, gn, qx, kn, vn, ck, cv, sink_rows)


def _rope_tables(pos):
    half = SWA_HEAD_DIM // 2
    inv = ROPE_THETA ** (-jnp.arange(half, dtype=F32) / half)
    ang = pos.astype(F32)[:, None] * inv[None, :]
    cos, sin = jnp.cos(ang), jnp.sin(ang)
    zero = jnp.zeros_like(sin)
    rep = LANES // SWA_HEAD_DIM
    return (jnp.tile(jnp.concatenate([cos, cos], axis=1), (1, rep)),
            jnp.tile(jnp.concatenate([-sin, zero], axis=1), (1, rep)),
            jnp.tile(jnp.concatenate([zero, sin], axis=1), (1, rep)))


def kernel(x_prompt, x_sample, state_gla, cache_swa_k, cache_swa_v, norm_attn, w_in, w_gk2, b_gk2,
           gla_norm, attn_sinks, w_o, norm_ffn, w_gate, w_up, w_down, norm_final):
    batch, seq, _ = x_prompt.shape
    nbatch, nt, _ = x_sample.shape
    n_s = nbatch * nt

    splits = np.cumsum([GLA_QK, GLA_QK, GLA_WIDTH, GATE_RANK, GLA_WIDTH, SWA_Q, SWA_KV, SWA_KV])
    c_ga0, c_ga1 = int(splits[2]), int(splits[3])
    w_in_p = jnp.concatenate(
        [w_in[:, :, :c_ga0], w_in[:, :, c_ga1:], w_in[:, :, c_ga0:c_ga1],
         jnp.zeros((DEPTH, D_MODEL, LANES - GATE_RANK), w_in.dtype)], axis=2).astype(BF16)
    wg2_p = jnp.concatenate([w_gk2, jnp.zeros((DEPTH, LANES - GATE_RANK, GLA_QK), w_gk2.dtype)],
                            axis=1).astype(BF16)
    bg2 = b_gk2.reshape(DEPTH, 1, GLA_QK)
    nw_a = norm_attn.reshape(DEPTH, 1, D_MODEL)
    nw_f = norm_ffn.reshape(DEPTH, 1, D_MODEL)
    wo_b, wg_b, wu_b, wd_b = (w.astype(BF16) for w in (w_o, w_gate, w_up, w_down))
    nf = norm_final.reshape(1, D_MODEL)

    tabs_p = _rope_tables(jnp.arange(seq))
    tabs_s = _rope_tables(jnp.tile(PAST_LEN + jnp.arange(nt), TOK_TILE // nt))

    state_r = state_gla.reshape(DEPTH, nbatch * GLA_HEADS, GLA_DK, GLA_DV)
    ck_r = cache_swa_k.reshape(DEPTH, nbatch, WINDOW, SWA_KV)
    cv_r = cache_swa_v.reshape(DEPTH, nbatch, WINDOW, SWA_KV)

    xp = x_prompt.reshape(batch * seq, D_MODEL)
    xs = x_sample.reshape(n_s, D_MODEL)
    sp_l, kp_l, vp_l, ss_l, ks_l, vs_l = [], [], [], [], [], []
    tpad = ((0, 0), (0, SUBLANES - nt), (0, 0))
    for l in range(DEPTH):
        final = l == DEPTH - 1
        q, k, g, v, gg, sq, sk, sv = _in_proj(xp, nw_a, w_in_p, wg2_p, bg2, tabs_p, l)
        og, s_fin = _gla_prompt(q, k, g, v, gg, gla_norm[l].reshape(1, GLA_DV), batch, seq)
        osw = _swa_prompt(attn_sinks[l], sq, sk, sv, batch, seq)
        xp = _out_ffn(xp, og, osw, wo_b, nw_f, wg_b, wu_b, wd_b, nf, l, final)
        sp_l.append(s_fin)
        kp_l.append(sk.reshape(batch, seq, SWA_KV_HEADS, SWA_HEAD_DIM)[:, seq - WINDOW:])
        vp_l.append(sv.reshape(batch, seq, SWA_KV_HEADS, SWA_HEAD_DIM)[:, seq - WINDOW:])

        q, k, g, v, gg, sq, sk, sv = _in_proj(xs, nw_a, w_in_p, wg2_p, bg2, tabs_s, l)

        def bhtk(a, w_):
            return a.reshape(nbatch, nt, GLA_HEADS, w_).transpose(0, 2, 1, 3).reshape(nbatch * GLA_HEADS, nt, w_)

        def bhkt(a):
            return a.reshape(nbatch, nt, GLA_HEADS, GLA_DK).transpose(0, 2, 3, 1).reshape(
                nbatch * GLA_HEADS, GLA_DK, nt)

        q4, k4, g4 = (jnp.pad(bhtk(a, GLA_DK), tpad) for a in (q, k, g))
        v4 = jnp.pad(bhtk(v.astype(F32), GLA_DV), tpad)
        gg4 = jnp.pad(bhtk(gg, GLA_DV), tpad)
        qk = sq.reshape(nbatch, nt, SWA_KV_HEADS, SWA_GROUP, SWA_HEAD_DIM).transpose(0, 2, 1, 3, 4)
        zq = jnp.zeros_like(qk[:, 0])
        qx = jnp.stack([jnp.concatenate([qk[:, 0], zq], axis=-1), jnp.concatenate([zq, qk[:, 1]], axis=-1)],
                       axis=1).reshape(nbatch, SWA_KV_HEADS * nt * SWA_GROUP, SWA_KV)
        sink_rows = jnp.broadcast_to(attn_sinks[l].reshape(SWA_KV_HEADS, 1, SWA_GROUP),
                                     (SWA_KV_HEADS, nt, SWA_GROUP)).reshape(1, -1, 1)
        og4, s_new, os4, ck_new, cv_new = _sample_mix(
            q4, k4, g4, bhkt(k), bhkt(g), v4, gg4, state_r, gla_norm[l].reshape(1, 1, GLA_DV),
            qx, sk.reshape(nbatch, nt, SWA_KV), sv.reshape(nbatch, nt, SWA_KV), ck_r, cv_r, sink_rows,
            l, nbatch, nt)
        og = og4[:, :nt].reshape(nbatch, GLA_HEADS, nt, GLA_DV).transpose(0, 2, 1, 3).reshape(n_s, GLA_WIDTH)
        os5 = os4.reshape(nbatch, SWA_KV_HEADS, nt, SWA_GROUP, SWA_KV)
        osw = jnp.stack([os5[:, 0, :, :, :SWA_HEAD_DIM], os5[:, 1, :, :, SWA_HEAD_DIM:]], axis=2)
        osw = osw.reshape(n_s, SWA_Q)
        xs = _out_ffn(xs, og.astype(BF16), osw.astype(BF16), wo_b, nw_f, wg_b, wu_b, wd_b, nf, l, final)
        ss_l.append(s_new.reshape(nbatch, GLA_HEADS, GLA_DK, GLA_DV))
        ks_l.append(ck_new.reshape(nbatch, WINDOW, SWA_KV_HEADS, SWA_HEAD_DIM))
        vs_l.append(cv_new.reshape(nbatch, WINDOW, SWA_KV_HEADS, SWA_HEAD_DIM))

    return (xp.reshape(batch, seq, D_MODEL), xs.reshape(nbatch, nt, D_MODEL),
            jnp.stack(sp_l), jnp.stack(kp_l), jnp.stack(vp_l),
            jnp.stack(ss_l), jnp.stack(ks_l), jnp.stack(vs_l))
```

```python
import functools

import jax
import jax.numpy as jnp
import numpy as np
from jax import lax
from jax.experimental import pallas as pl
from jax.experimental.pallas import tpu as pltpu

F32 = jnp.float32
BF16 = jnp.bfloat16

D_MODEL = 1024
DEPTH = 4
PAST_LEN = 8192
GLA_HEADS = 4
GLA_DV = 128
GLA_DK = 64
GLA_QK = GLA_HEADS * GLA_DK
GLA_WIDTH = GLA_HEADS * GLA_DV
GATE_RANK = 16
GATE_NORMALIZER = 16.0
SWA_HEAD_DIM = 64
SWA_HEADS = 8
SWA_KV_HEADS = 2
SWA_GROUP = 4
SWA_Q = SWA_HEADS * SWA_HEAD_DIM
SWA_KV = SWA_KV_HEADS * SWA_HEAD_DIM
WINDOW = 128
ROPE_THETA = 10000.0
D_FF = 2816
NORM_EPS = 1e-6

LANES = 128
SUBLANES = 8
VMEM_LIMIT = 56 * 1024 * 1024

C_Q, C_K, C_V, C_GG, C_SQ, C_SK, C_SV, C_GA = 0, 256, 512, 1024, 1536, 2048, 2176, 2304
IN_COLS_P = C_GA + LANES

TOK_TILE = 512
GLA_CHUNK = 64
GLA_SUB = 8
GLA_TILE = 512
GLA_UNROLL = 2
FF_CHUNK = D_FF
SWA_ROWS = 32
SAMPLE_BT = 8
NEG = -1e30
LOG2E = 1.4426950408889634


def _rms(x, w):
    ms = jnp.mean(x * x, axis=-1, keepdims=True)
    return x * lax.rsqrt(ms + NORM_EPS) * w


def _sigmoid(x):
    return 1.0 / (1.0 + jnp.exp(-x))


def _dot(a, b):
    return jnp.dot(a, b, preferred_element_type=F32)


def _dot_nt(a, b):
    return lax.dot_general(a, b, (((1,), (1,)), ((), ())), preferred_element_type=F32)


def _in_proj_kernel(x_ref, nw_ref, w_ref, wg2_ref, bg2_ref, cos_ref, sa_ref, sb_ref,
                    q_ref, k_ref, g_ref, v_ref, gg_ref, sq_ref, sk_ref, sv_ref):
    h = _rms(x_ref[...], nw_ref[...]).astype(BF16)

    def proj(c0, c1):
        return _dot(h, w_ref[:, c0:c1])

    q_ref[...] = proj(C_Q, C_K) * (GLA_DK ** -0.5)
    k_ref[...] = proj(C_K, C_V)
    v_ref[...] = proj(C_V, C_GG).astype(BF16)
    gg_ref[...] = proj(C_GG, C_SQ)
    ga = proj(C_GA, IN_COLS_P).astype(BF16)
    gx = _dot(ga, wg2_ref[...]) + bg2_ref[...]
    g_ref[...] = (jnp.minimum(gx, 0.0) - jnp.log1p(jnp.exp(-jnp.abs(gx)))) * (1.0 / GATE_NORMALIZER)

    cos, sa, sb = cos_ref[...], sa_ref[...], sb_ref[...]

    def rope(z):
        return z * cos + pltpu.roll(z, LANES - 32, 1) * sa + pltpu.roll(z, 32, 1) * sb

    for j in range(SWA_Q // LANES):
        zq = proj(C_SQ + j * LANES, C_SQ + (j + 1) * LANES)
        sq_ref[:, j * LANES:(j + 1) * LANES] = (rope(zq) * (SWA_HEAD_DIM ** -0.5 * LOG2E)).astype(BF16)
    sk_ref[...] = rope(proj(C_SK, C_SV))
    sv_ref[...] = proj(C_SV, C_GA)


def _in_proj(x, nw, w, wg2, bg2, tabs, layer):
    n = x.shape[0]
    nt = n // TOK_TILE
    cos, sa, sb = tabs
    ntab = cos.shape[0] // TOK_TILE
    row = lambda w_: pl.BlockSpec((TOK_TILE, w_), lambda i: (i, 0))
    tab = pl.BlockSpec((TOK_TILE, LANES), lambda i: (i % ntab, 0))
    const = lambda shp: pl.BlockSpec((None,) + shp, lambda i: (layer, 0, 0), pipeline_mode=pl.Buffered(1))
    outs = [(GLA_QK, F32), (GLA_QK, F32), (GLA_QK, F32), (GLA_WIDTH, BF16), (GLA_WIDTH, F32),
            (SWA_Q, BF16), (SWA_KV, F32), (SWA_KV, F32)]
    return pl.pallas_call(
        _in_proj_kernel,
        grid=(nt,),
        in_specs=[row(D_MODEL), const((1, D_MODEL)), const((D_MODEL, IN_COLS_P)),
                  const((LANES, GLA_QK)), const((1, GLA_QK)), tab, tab, tab],
        out_specs=[row(w_) for w_, _ in outs],
        out_shape=[jax.ShapeDtypeStruct((n, w_), dt) for w_, dt in outs],
        compiler_params=pltpu.CompilerParams(dimension_semantics=("arbitrary",),
                                             vmem_limit_bytes=VMEM_LIMIT),
        name="in_proj",
    )(x, nw, w, wg2, bg2, cos, sa, sb)


def _gla_kernel(q_ref, k_ref, g_ref, v_ref, gg_ref, gn_ref, bias_ref, o_ref, sout_ref, s_ref):
    ti = pl.program_id(1)
    C, SB = GLA_CHUNK, GLA_SUB
    NB = C // SB
    NP = GLA_HEADS // 2

    @pl.when(ti == 0)
    def _():
        s_ref[...] = jnp.zeros_like(s_ref)

    ltri = (lax.broadcasted_iota(jnp.int32, (C, 2 * C), 0) >= lax.broadcasted_iota(jnp.int32, (C, 2 * C), 1) % C
            ).astype(F32).astype(BF16)
    lane = lax.broadcasted_iota(jnp.int32, (1, LANES), 1)
    mask_a = (lane < GLA_DK).astype(F32)
    mask_b = 1.0 - mask_a
    col8 = lax.broadcasted_iota(jnp.int32, (SB, LANES), 1) % C
    colblk, colmod = col8 // SB, col8 % SB
    gn = gn_ref[...]
    zk = jnp.zeros((GLA_DK, GLA_DV), BF16)
    off1 = [sum(C - SB * (jj + 1) for jj in range(j)) for j in range(NB - 1)]

    def sub_bcast(x, j):
        x3 = x.reshape(NB, SB, LANES)
        return jnp.broadcast_to(x3[:, j:j + 1, :], (NB, SB, LANES)).reshape(C, LANES)

    def heads(x):
        return jnp.concatenate([x * mask_a, x * mask_b], axis=0).astype(BF16)

    def body(it, carry):
        rows = [pl.ds(pl.multiple_of((it * GLA_UNROLL + cc) * C, C), C) for cc in range(GLA_UNROLL)]
        units = [(cc, p) for cc in range(GLA_UNROLL) for p in range(NP)]
        kl = [slice(p * LANES, (p + 1) * LANES) for p in range(NP)]
        vl = [slice(h * GLA_DV, (h + 1) * GLA_DV) for h in range(GLA_HEADS)]

        b_all = []
        for cc in range(GLA_UNROLL):
            g2 = g_ref[rows[cc], :] * LOG2E
            g_hi = g2.astype(BF16)
            g_lo = (g2 - g_hi.astype(F32)).astype(BF16)
            b_all.append(_dot(ltri, jnp.concatenate([g_hi, g_lo], axis=0)))

        pre = {}
        for cc, p in units:
            q, k, b = q_ref[rows[cc], kl[p]], k_ref[rows[cc], kl[p]], b_all[cc][:, kl[p]]
            b_last = b[C - 1:C, :]
            k1 = k * jnp.exp2(sub_bcast(b, SB - 1) - b)
            lhs1 = jnp.concatenate(
                [q[SB * (j + 1):, :] * jnp.exp2(b[SB * (j + 1):, :] - b[SB * j + SB - 1:SB * j + SB, :])
                 for j in range(NB - 1)], axis=0)
            lhs2 = jnp.concatenate(
                [q * jnp.exp2((b - sub_bcast(b, jj)) + jnp.tile(bias_ref[jj], (NB, 1))) for jj in range(SB)],
                axis=0)
            kd = k * jnp.exp2(b_last - b)
            pre[cc, p] = (lhs1.astype(BF16), heads(k1), lhs2.astype(BF16), heads(k),
                          (q * jnp.exp2(b)).astype(BF16), kd, b_last)

        outs = {u: (_dot_nt(pre[u][0], pre[u][1]), _dot_nt(pre[u][2], pre[u][3])) for u in units}

        lo = {}
        for u in units:
            out1, out2 = outs[u]
            a_rows = []
            for i in range(NB):
                d = out2[(SB - 1) * C + SB * i:(SB - 1) * C + SB * (i + 1), :]
                for jj in range(SB - 2, -1, -1):
                    d = jnp.where(colmod == jj, out2[jj * C + SB * i:jj * C + SB * (i + 1), :], d)
                acc = jnp.where(colblk == i, d, 0.0)
                for j in range(i):
                    r0 = off1[j] + SB * (i - j - 1)
                    acc = jnp.where(colblk == j, out1[r0:r0 + SB, :], acc)
                a_rows.append(acc)
            a = jnp.concatenate(a_rows, axis=0)
            lo[u] = jnp.concatenate([pre[u][4], a.astype(BF16)], axis=1)

        o_all = {}
        for cc in range(GLA_UNROLL):
            mts = [jnp.concatenate([pre[cc, p][5], jnp.broadcast_to(pre[cc, p][6], (C, LANES))], axis=0).T
                   for p in range(NP)]
            for p in range(NP):
                ha, hb = 2 * p, 2 * p + 1
                v_a, v_b = v_ref[rows[cc], vl[ha]], v_ref[rows[cc], vl[hb]]
                s_a, s_b = s_ref[ha], s_ref[hb]
                o_all[cc, ha] = _dot(lo[cc, p], jnp.concatenate([s_a.astype(BF16), zk, v_a, zk], axis=0))
                o_all[cc, hb] = _dot(lo[cc, p], jnp.concatenate([zk, s_b.astype(BF16), zk, v_b], axis=0))
                kdt = mts[p][:, :C].astype(BF16)
                dec = jnp.exp2(mts[p][:, C:C + 1])
                s_ref[ha] = dec[:GLA_DK] * s_a + _dot(kdt[:GLA_DK], v_a)
                s_ref[hb] = dec[GLA_DK:] * s_b + _dot(kdt[GLA_DK:], v_b)

        for cc in range(GLA_UNROLL):
            for h in range(GLA_HEADS):
                gg = gg_ref[rows[cc], vl[h]]
                o_ref[rows[cc], vl[h]] = (_rms(o_all[cc, h], gn) * (gg * _sigmoid(gg))).astype(BF16)
        return carry

    lax.fori_loop(0, GLA_TILE // (C * GLA_UNROLL), body, 0)

    @pl.when(ti == pl.num_programs(1) - 1)
    def _():
        sout_ref[0] = s_ref[...]


def _gla_prompt(q, k, g, v, gg, gn, batch, seq):
    nt = seq // GLA_TILE
    blk = lambda w_: pl.BlockSpec((GLA_TILE, w_), lambda b, t: (b * nt + t, 0))
    r = np.arange(GLA_SUB)
    bias = np.where(r[None, :, None] >= r[:, None, None], 0.0, NEG) * np.ones((1, 1, LANES))
    return pl.pallas_call(
        _gla_kernel,
        grid=(batch, nt),
        in_specs=[blk(GLA_QK), blk(GLA_QK), blk(GLA_QK), blk(GLA_WIDTH), blk(GLA_WIDTH),
                  pl.BlockSpec((1, GLA_DV), lambda b, t: (0, 0)),
                  pl.BlockSpec((GLA_SUB, GLA_SUB, LANES), lambda b, t: (0, 0, 0))],
        out_specs=[blk(GLA_WIDTH),
                   pl.BlockSpec((1, GLA_HEADS, GLA_DK, GLA_DV), lambda b, t: (b, 0, 0, 0))],
        out_shape=[jax.ShapeDtypeStruct((batch * seq, GLA_WIDTH), BF16),
                   jax.ShapeDtypeStruct((batch, GLA_HEADS, GLA_DK, GLA_DV), F32)],
        scratch_shapes=[pltpu.VMEM((GLA_HEADS, GLA_DK, GLA_DV), F32)],
        compiler_params=pltpu.CompilerParams(
            dimension_semantics=("arbitrary", "arbitrary"), vmem_limit_bytes=VMEM_LIMIT),
        name="gla_prompt",
    )(q, k, g, v, gg, gn, jnp.asarray(bias, F32))


def _swa_kernel(sink_ref, q_ref, kp_ref, kc_ref, vp_ref, vc_ref, bias_ref, o_ref, s_scr, p_scr, r_scr):
    W = WINDOW
    kk = jnp.concatenate([kp_ref[...], kc_ref[...]], axis=0)
    vv = jnp.concatenate([vp_ref[...], vc_ref[...]], axis=0)
    kk_sw = pltpu.roll(kk, SWA_HEAD_DIM, 1)
    vv_sw = pltpu.roll(vv, SWA_HEAD_DIM, 1)
    lo = lax.broadcasted_iota(jnp.int32, (2 * W, LANES), 1) < SWA_HEAD_DIM

    def ext(x, x_sw, kv):
        a = jnp.where(lo, x if kv == 0 else x_sw, 0.0)
        b = jnp.where(lo, 0.0, x_sw if kv == 0 else x)
        return jnp.concatenate([a, b], axis=0).astype(BF16)

    lane_q = lax.broadcasted_iota(jnp.int32, (SWA_ROWS, LANES), 1) < SWA_HEAD_DIM

    npair = SWA_HEADS // 2
    k_ext = [ext(kk, kk_sw, kv) for kv in range(SWA_KV_HEADS)]
    v_ext = [ext(vv, vv_sw, kv) for kv in range(SWA_KV_HEADS)]
    for p in range(npair):
        s_scr[p] = _dot_nt(q_ref[:, p * LANES:(p + 1) * LANES], k_ext[p // (SWA_GROUP // 2)])
    for p in range(npair):
        for r0 in range(0, W, SWA_ROWS):
            rs = slice(r0, r0 + SWA_ROWS)
            rden = []
            for hh in range(2):
                cs = slice(hh * 2 * W, (hh + 1) * 2 * W)
                sink = sink_ref[2 * p + hh] * LOG2E
                sh = s_scr[p, rs, cs] + bias_ref[rs, :]
                m = jnp.maximum(jnp.max(sh, axis=-1, keepdims=True), sink)
                e = jnp.exp2(sh - m)
                den = jnp.sum(e, axis=-1, keepdims=True) + jnp.exp2(sink - m)
                p_scr[p, rs, cs] = e.astype(BF16)
                rden.append(1.0 / den)
            r_scr[p, rs, :] = jnp.where(lane_q, rden[0], rden[1])
    for p in range(npair):
        o = _dot(p_scr[p], v_ext[p // (SWA_GROUP // 2)]) * r_scr[p]
        o_ref[:, p * LANES:(p + 1) * LANES] = o.astype(BF16)


def _swa_prompt(sinks, sq, sk, sv, batch, seq):
    nb = seq // WINDOW
    W = WINDOW
    cur = lambda b, n: (b * nb + n, 0)
    prev = lambda b, n: (b * nb + jnp.maximum(n - 1, 0), 0)
    kvspec = lambda im: pl.BlockSpec((W, SWA_KV), im)
    qi, kj = np.arange(W)[:, None], np.arange(2 * W)[None, :]
    band = (kj > qi) & (kj <= qi + W)
    bias = np.stack([np.where(band & (kj >= W), 0.0, NEG), np.where(band, 0.0, NEG)])
    return pl.pallas_call(
        _swa_kernel,
        grid=(batch, nb),
        in_specs=[pl.BlockSpec(memory_space=pltpu.SMEM),
                  pl.BlockSpec((W, SWA_Q), cur),
                  kvspec(prev), kvspec(cur), kvspec(prev), kvspec(cur),
                  pl.BlockSpec((None, W, 2 * W), lambda b, n: (jnp.minimum(n, 1), 0, 0))],
        out_specs=pl.BlockSpec((W, SWA_Q), cur),
        out_shape=jax.ShapeDtypeStruct((batch * seq, SWA_Q), BF16),
        scratch_shapes=[pltpu.VMEM((SWA_HEADS // 2, W, 4 * W), F32), pltpu.VMEM((SWA_HEADS // 2, W, 4 * W), BF16),
                        pltpu.VMEM((SWA_HEADS // 2, W, LANES), F32)],
        compiler_params=pltpu.CompilerParams(dimension_semantics=("arbitrary", "arbitrary"),
                                             vmem_limit_bytes=VMEM_LIMIT),
        name="swa_prompt",
    )(sinks, sq, sk, sk, sv, sv, jnp.asarray(bias, F32))


def _out_ffn_kernel(x_ref, og_ref, os_ref, wo_ref, nw_ref, wg_ref, wu_ref, wd_ref, nf_ref, y_ref, *, final):
    x = (x_ref[...] + _dot(og_ref[...], wo_ref[:GLA_WIDTH, :]) + _dot(os_ref[...], wo_ref[GLA_WIDTH:, :]))
    h = _rms(x, nw_ref[...]).astype(BF16)
    acc = x
    for c0 in range(0, D_FF, FF_CHUNK):
        a = _dot(h, wg_ref[:, c0:c0 + FF_CHUNK])
        u = _dot(h, wu_ref[:, c0:c0 + FF_CHUNK])
        acc = acc + _dot((a * _sigmoid(a) * u).astype(BF16), wd_ref[c0:c0 + FF_CHUNK, :])
    if final:
        acc = _rms(acc, nf_ref[...])
    y_ref[...] = acc


def _out_ffn(x, og, osw, wo, nw, wg, wu, wd, nf, layer, final):
    n = x.shape[0]
    row = lambda w_: pl.BlockSpec((TOK_TILE, w_), lambda i: (i, 0))
    const = lambda shp: pl.BlockSpec((None,) + shp, lambda i: (layer, 0, 0), pipeline_mode=pl.Buffered(1))
    return pl.pallas_call(
        functools.partial(_out_ffn_kernel, final=final),
        grid=(n // TOK_TILE,),
        in_specs=[row(D_MODEL), row(GLA_WIDTH), row(SWA_Q), const((D_MODEL, D_MODEL)), const((1, D_MODEL)),
                  const((D_MODEL, D_FF)), const((D_MODEL, D_FF)), const((D_FF, D_MODEL)),
                  pl.BlockSpec((1, D_MODEL), lambda i: (0, 0))],
        out_specs=row(D_MODEL),
        out_shape=jax.ShapeDtypeStruct((n, D_MODEL), F32),
        compiler_params=pltpu.CompilerParams(dimension_semantics=("arbitrary",),
                                             vmem_limit_bytes=VMEM_LIMIT),
        name="out_ffn",
    )(x, og, osw, wo, nw, wg, wu, wd, nf)


def _sample_kernel(q_ref, k_ref, g_ref, kt_ref, gt_ref, v_ref, gg_ref, s0_ref, gn_ref,
                   qx_ref, kn_ref, vn_ref, ck_ref, cv_ref, sink_ref,
                   og_ref, sn_ref, os_ref, cko_ref, cvo_ref, *, nt):
    TP = SUBLANES
    q, k, g, v = q_ref[...], k_ref[...], g_ref[...], v_ref[...]
    s0 = s0_ref[...]
    tt = lax.broadcasted_iota(jnp.int32, (1, TP, 1), 1)
    b = jnp.zeros_like(g)
    for s in range(nt):
        b = b + jnp.where(tt >= s, g[:, s:s + 1, :], 0.0)
    o = jnp.einsum('gtk,gkv->gtv', (q * jnp.exp(b)).astype(BF16), s0.astype(BF16),
                   preferred_element_type=F32)
    for s in range(nt):
        dec = jnp.exp(jnp.where(tt >= s, b - b[:, s:s + 1, :], NEG))
        a = jnp.sum(q * k[:, s:s + 1, :] * dec, axis=-1, keepdims=True)
        o = o + a * v[:, s:s + 1, :]
    gg = gg_ref[...]
    og_ref[...] = _rms(o, gn_ref[...]) * (gg * _sigmoid(gg))

    kt, gt = kt_ref[...], gt_ref[...]
    cum = [gt[:, :, 0:1]]
    for s in range(1, nt):
        cum.append(cum[-1] + gt[:, :, s:s + 1])
    s_new = jnp.exp(cum[-1]) * s0
    for s in range(nt):
        s_new = s_new + (kt[:, :, s:s + 1] * jnp.exp(cum[-1] - cum[s])) * v[:, s:s + 1, :]
    sn_ref[...] = s_new

    W = WINDOW
    qx = qx_ref[...]
    ck, cv = ck_ref[...], cv_ref[...]
    kn, vn = kn_ref[...], vn_ref[...]
    sink = sink_ref[...] * LOG2E
    nrow = SWA_KV_HEADS * nt * SWA_GROUP
    r = lax.broadcasted_iota(jnp.int32, (1, nrow, W), 1)
    tq = (r % (nt * SWA_GROUP)) // SWA_GROUP
    j = lax.broadcasted_iota(jnp.int32, (1, nrow, W), 2)
    sc = jnp.einsum('bqd,bsd->bqs', qx, ck.astype(BF16), preferred_element_type=F32)
    sc = jnp.where(j > tq, sc, NEG)
    tq1 = tq[:, :, 0:1]
    qf = qx.astype(F32)
    sn = [jnp.where(tq1 >= s, jnp.sum(qf * kn[:, s:s + 1, :], axis=-1, keepdims=True), NEG)
          for s in range(nt)]
    m = jnp.maximum(jnp.max(sc, axis=-1, keepdims=True), sink)
    for s in range(nt):
        m = jnp.maximum(m, sn[s])
    pc = jnp.exp2(sc - m)
    pn = [jnp.exp2(sn[s] - m) for s in range(nt)]
    den = jnp.sum(pc, axis=-1, keepdims=True) + jnp.exp2(sink - m)
    for s in range(nt):
        den = den + pn[s]
    o = jnp.einsum('bqs,bsd->bqd', pc.astype(BF16), cv.astype(BF16), preferred_element_type=F32)
    for s in range(nt):
        o = o + pn[s] * vn[:, s:s + 1, :]
    os_ref[...] = o * (1.0 / den)

    cko_ref[:, :W - nt, :] = ck[:, nt:, :]
    cko_ref[:, W - nt:, :] = kn
    cvo_ref[:, :W - nt, :] = cv[:, nt:, :]
    cvo_ref[:, W - nt:, :] = vn


def _sample_mix(q4, k4, g4, kt, gt, v4, gg4, state, gn, qx, kn, vn, ck, cv, sink_rows, layer, nbatch, nt):
    bt = SAMPLE_BT
    G = bt * GLA_HEADS
    TP = SUBLANES
    nrow = SWA_KV_HEADS * nt * SWA_GROUP
    b3 = lambda a, c: pl.BlockSpec((G, a, c), lambda i: (i, 0, 0))
    s3 = lambda a, c: pl.BlockSpec((bt, a, c), lambda i: (i, 0, 0))
    lay_g = pl.BlockSpec((None, G, GLA_DK, GLA_DV), lambda i: (layer, i, 0, 0))
    lay_c = pl.BlockSpec((None, bt, WINDOW, SWA_KV), lambda i: (layer, i, 0, 0))
    ng = nbatch * GLA_HEADS
    return pl.pallas_call(
        functools.partial(_sample_kernel, nt=nt),
        grid=(nbatch // bt,),
        in_specs=[b3(TP, GLA_DK), b3(TP, GLA_DK), b3(TP, GLA_DK), b3(GLA_DK, nt), b3(GLA_DK, nt),
                  b3(TP, GLA_DV), b3(TP, GLA_DV), lay_g,
                  pl.BlockSpec((1, 1, GLA_DV), lambda i: (0, 0, 0)),
                  s3(nrow, SWA_KV), s3(nt, SWA_KV), s3(nt, SWA_KV), lay_c, lay_c,
                  pl.BlockSpec((1, nrow, 1), lambda i: (0, 0, 0))],
        out_specs=[b3(TP, GLA_DV), b3(GLA_DK, GLA_DV), s3(nrow, SWA_KV), s3(WINDOW, SWA_KV), s3(WINDOW, SWA_KV)],
        out_shape=[jax.ShapeDtypeStruct((ng, TP, GLA_DV), F32),
                   jax.ShapeDtypeStruct((ng, GLA_DK, GLA_DV), F32),
                   jax.ShapeDtypeStruct((nbatch, nrow, SWA_KV), F32),
                   jax.ShapeDtypeStruct((nbatch, WINDOW, SWA_KV), F32),
                   jax.ShapeDtypeStruct((nbatch, WINDOW, SWA_KV), F32)],
        compiler_params=pltpu.CompilerParams(dimension_semantics=("arbitrary",),
                                             vmem_limit_bytes=VMEM_LIMIT),
        name="sample_mix",
    )(q4, k4, g4, kt, gt, v4, gg4, state, gn, qx, kn, vn, ck, cv, sink_rows)


def _rope_tables(pos):
    half = SWA_HEAD_DIM // 2
    inv = ROPE_THETA ** (-jnp.arange(half, dtype=F32) / half)
    ang = pos.astype(F32)[:, None] * inv[None, :]
    cos, sin = jnp.cos(ang), jnp.sin(ang)
    zero = jnp.zeros_like(sin)
    rep = LANES // SWA_HEAD_DIM
    return (jnp.tile(jnp.concatenate([cos, cos], axis=1), (1, rep)),
            jnp.tile(jnp.concatenate([-sin, zero], axis=1), (1, rep)),
            jnp.tile(jnp.concatenate([zero, sin], axis=1), (1, rep)))


def kernel(x_prompt, x_sample, state_gla, cache_swa_k, cache_swa_v, norm_attn, w_in, w_gk2, b_gk2,
           gla_norm, attn_sinks, w_o, norm_ffn, w_gate, w_up, w_down, norm_final):
    batch, seq, _ = x_prompt.shape
    nbatch, nt, _ = x_sample.shape
    n_s = nbatch * nt

    splits = np.cumsum([GLA_QK, GLA_QK, GLA_WIDTH, GATE_RANK, GLA_WIDTH, SWA_Q, SWA_KV, SWA_KV])
    c_ga0, c_ga1 = int(splits[2]), int(splits[3])
    w_in_p = jnp.concatenate(
        [w_in[:, :, :c_ga0], w_in[:, :, c_ga1:], w_in[:, :, c_ga0:c_ga1],
         jnp.zeros((DEPTH, D_MODEL, LANES - GATE_RANK), w_in.dtype)], axis=2).astype(BF16)
    wg2_p = jnp.concatenate([w_gk2, jnp.zeros((DEPTH, LANES - GATE_RANK, GLA_QK), w_gk2.dtype)],
                            axis=1).astype(BF16)
    bg2 = b_gk2.reshape(DEPTH, 1, GLA_QK)
    nw_a = norm_attn.reshape(DEPTH, 1, D_MODEL)
    nw_f = norm_ffn.reshape(DEPTH, 1, D_MODEL)
    wo_b, wg_b, wu_b, wd_b = (w.astype(BF16) for w in (w_o, w_gate, w_up, w_down))
    nf = norm_final.reshape(1, D_MODEL)

    tabs_p = _rope_tables(jnp.arange(seq))
    tabs_s = _rope_tables(jnp.tile(PAST_LEN + jnp.arange(nt), TOK_TILE // nt))

    state_r = state_gla.reshape(DEPTH, nbatch * GLA_HEADS, GLA_DK, GLA_DV)
    ck_r = cache_swa_k.reshape(DEPTH, nbatch, WINDOW, SWA_KV)
    cv_r = cache_swa_v.reshape(DEPTH, nbatch, WINDOW, SWA_KV)

    xp = x_prompt.reshape(batch * seq, D_MODEL)
    xs = x_sample.reshape(n_s, D_MODEL)
    sp_l, kp_l, vp_l, ss_l, ks_l, vs_l = [], [], [], [], [], []
    tpad = ((0, 0), (0, SUBLANES - nt), (0, 0))
    for l in range(DEPTH):
        final = l == DEPTH - 1
        q, k, g, v, gg, sq, sk, sv = _in_proj(xp, nw_a, w_in_p, wg2_p, bg2, tabs_p, l)
        og, s_fin = _gla_prompt(q, k, g, v, gg, gla_norm[l].reshape(1, GLA_DV), batch, seq)
        osw = _swa_prompt(attn_sinks[l], sq, sk, sv, batch, seq)
        xp = _out_ffn(xp, og, osw, wo_b, nw_f, wg_b, wu_b, wd_b, nf, l, final)
        sp_l.append(s_fin)
        kp_l.append(sk.reshape(batch, seq, SWA_KV_HEADS, SWA_HEAD_DIM)[:, seq - WINDOW:])
        vp_l.append(sv.reshape(batch, seq, SWA_KV_HEADS, SWA_HEAD_DIM)[:, seq - WINDOW:])

        q, k, g, v, gg, sq, sk, sv = _in_proj(xs, nw_a, w_in_p, wg2_p, bg2, tabs_s, l)

        def bhtk(a, w_):
            return a.reshape(nbatch, nt, GLA_HEADS, w_).transpose(0, 2, 1, 3).reshape(nbatch * GLA_HEADS, nt, w_)

        def bhkt(a):
            return a.reshape(nbatch, nt, GLA_HEADS, GLA_DK).transpose(0, 2, 3, 1).reshape(
                nbatch * GLA_HEADS, GLA_DK, nt)

        q4, k4, g4 = (jnp.pad(bhtk(a, GLA_DK), tpad) for a in (q, k, g))
        v4 = jnp.pad(bhtk(v.astype(F32), GLA_DV), tpad)
        gg4 = jnp.pad(bhtk(gg, GLA_DV), tpad)
        qk = sq.reshape(nbatch, nt, SWA_KV_HEADS, SWA_GROUP, SWA_HEAD_DIM).transpose(0, 2, 1, 3, 4)
        zq = jnp.zeros_like(qk[:, 0])
        qx = jnp.stack([jnp.concatenate([qk[:, 0], zq], axis=-1), jnp.concatenate([zq, qk[:, 1]], axis=-1)],
                       axis=1).reshape(nbatch, SWA_KV_HEADS * nt * SWA_GROUP, SWA_KV)
        sink_rows = jnp.broadcast_to(attn_sinks[l].reshape(SWA_KV_HEADS, 1, SWA_GROUP),
                                     (SWA_KV_HEADS, nt, SWA_GROUP)).reshape(1, -1, 1)
        og4, s_new, os4, ck_new, cv_new = _sample_mix(
            q4, k4, g4, bhkt(k), bhkt(g), v4, gg4, state_r, gla_norm[l].reshape(1, 1, GLA_DV),
            qx, sk.reshape(nbatch, nt, SWA_KV), sv.reshape(nbatch, nt, SWA_KV), ck_r, cv_r, sink_rows,
            l, nbatch, nt)
        og = og4[:, :nt].reshape(nbatch, GLA_HEADS, nt, GLA_DV).transpose(0, 2, 1, 3).reshape(n_s, GLA_WIDTH)
        os5 = os4.reshape(nbatch, SWA_KV_HEADS, nt, SWA_GROUP, SWA_KV)
        osw = jnp.stack([os5[:, 0, :, :, :SWA_HEAD_DIM], os5[:, 1, :, :, SWA_HEAD_DIM:]], axis=2)
        osw = osw.reshape(n_s, SWA_Q)
        xs = _out_ffn(xs, og.astype(BF16), osw.astype(BF16), wo_b, nw_f, wg_b, wu_b, wd_b, nf, l, final)
        ss_l.append(s_new.reshape(nbatch, GLA_HEADS, GLA_DK, GLA_DV))
        ks_l.append(ck_new.reshape(nbatch, WINDOW, SWA_KV_HEADS, SWA_HEAD_DIM))
        vs_l.append(cv_new.reshape(nbatch, WINDOW, SWA_KV_HEADS, SWA_HEAD_DIM))

    return (xp.reshape(batch, seq, D_MODEL), xs.reshape(nbatch, nt, D_MODEL),
            jnp.stack(sp_l), jnp.stack(kp_l), jnp.stack(vp_l),
            jnp.stack(ss_l), jnp.stack(ks_l), jnp.stack(vs_l))
```

```python
import functools

import jax
import jax.numpy as jnp
import numpy as np
from jax import lax
from jax.experimental import pallas as pl
from jax.experimental.pallas import tpu as pltpu

F32 = jnp.float32
BF16 = jnp.bfloat16

D_MODEL = 1024
DEPTH = 4
PAST_LEN = 8192
GLA_HEADS = 4
GLA_DV = 128
GLA_DK = 64
GLA_QK = GLA_HEADS * GLA_DK
GLA_WIDTH = GLA_HEADS * GLA_DV
GATE_RANK = 16
GATE_NORMALIZER = 16.0
SWA_HEAD_DIM = 64
SWA_HEADS = 8
SWA_KV_HEADS = 2
SWA_GROUP = 4
SWA_Q = SWA_HEADS * SWA_HEAD_DIM
SWA_KV = SWA_KV_HEADS * SWA_HEAD_DIM
WINDOW = 128
ROPE_THETA = 10000.0
D_FF = 2816
NORM_EPS = 1e-6

LANES = 128
SUBLANES = 8
VMEM_LIMIT = 56 * 1024 * 1024

C_Q, C_K, C_V, C_GG, C_SQ, C_SK, C_SV, C_GA = 0, 256, 512, 1024, 1536, 2048, 2176, 2304
IN_COLS_P = C_GA + LANES

TOK_TILE = 512
GLA_CHUNK = 64
GLA_SUB = 8
GLA_TILE = 512
GLA_UNROLL = 2
FF_CHUNK = D_FF
SWA_ROWS = 32
SAMPLE_BT = 8
NEG = -1e30
LOG2E = 1.4426950408889634


def _rms(x, w):
    ms = jnp.mean(x * x, axis=-1, keepdims=True)
    return x * lax.rsqrt(ms + NORM_EPS) * w


def _sigmoid(x):
    return 1.0 / (1.0 + jnp.exp(-x))


def _dot(a, b):
    return jnp.dot(a, b, preferred_element_type=F32)


def _dot_nt(a, b):
    return lax.dot_general(a, b, (((1,), (1,)), ((), ())), preferred_element_type=F32)


def _in_proj_kernel(x_ref, nw_ref, w_ref, wg2_ref, bg2_ref, cos_ref, sa_ref, sb_ref,
                    q_ref, k_ref, g_ref, v_ref, gg_ref, sq_ref, sk_ref, sv_ref):
    h = _rms(x_ref[...], nw_ref[...]).astype(BF16)

    def proj(c0, c1):
        return _dot(h, w_ref[:, c0:c1])

    q_ref[...] = proj(C_Q, C_K) * (GLA_DK ** -0.5)
    k_ref[...] = proj(C_K, C_V)
    v_ref[...] = proj(C_V, C_GG).astype(BF16)
    gg_ref[...] = proj(C_GG, C_SQ)
    ga = proj(C_GA, IN_COLS_P).astype(BF16)
    gx = _dot(ga, wg2_ref[...]) + bg2_ref[...]
    g_ref[...] = (jnp.minimum(gx, 0.0) - jnp.log1p(jnp.exp(-jnp.abs(gx)))) * (1.0 / GATE_NORMALIZER)

    cos, sa, sb = cos_ref[...], sa_ref[...], sb_ref[...]

    def rope(z):
        return z * cos + pltpu.roll(z, LANES - 32, 1) * sa + pltpu.roll(z, 32, 1) * sb

    for j in range(SWA_Q // (2 * LANES)):
        zq = proj(C_SQ + 2 * j * LANES, C_SQ + 2 * (j + 1) * LANES)
        for jj in range(2):
            sq_ref[:, (2 * j + jj) * LANES:(2 * j + jj + 1) * LANES] = (
                rope(zq[:, jj * LANES:(jj + 1) * LANES]) * (SWA_HEAD_DIM ** -0.5 * LOG2E)).astype(BF16)
    zkv = proj(C_SK, C_GA)
    sk_ref[...] = rope(zkv[:, :SWA_KV])
    sv_ref[...] = zkv[:, SWA_KV:]


def _in_proj(x, nw, w, wg2, bg2, tabs, layer):
    n = x.shape[0]
    nt = n // TOK_TILE
    cos, sa, sb = tabs
    ntab = cos.shape[0] // TOK_TILE
    row = lambda w_: pl.BlockSpec((TOK_TILE, w_), lambda i: (i, 0))
    tab = pl.BlockSpec((TOK_TILE, LANES), lambda i: (i % ntab, 0))
    const = lambda shp: pl.BlockSpec((None,) + shp, lambda i: (layer, 0, 0), pipeline_mode=pl.Buffered(1))
    outs = [(GLA_QK, F32), (GLA_QK, F32), (GLA_QK, F32), (GLA_WIDTH, BF16), (GLA_WIDTH, F32),
            (SWA_Q, BF16), (SWA_KV, F32), (SWA_KV, F32)]
    return pl.pallas_call(
        _in_proj_kernel,
        grid=(nt,),
        in_specs=[row(D_MODEL), const((1, D_MODEL)), const((D_MODEL, IN_COLS_P)),
                  const((LANES, GLA_QK)), const((1, GLA_QK)), tab, tab, tab],
        out_specs=[row(w_) for w_, _ in outs],
        out_shape=[jax.ShapeDtypeStruct((n, w_), dt) for w_, dt in outs],
        compiler_params=pltpu.CompilerParams(dimension_semantics=("arbitrary",),
                                             vmem_limit_bytes=VMEM_LIMIT),
        name="in_proj",
    )(x, nw, w, wg2, bg2, cos, sa, sb)


def _gla_kernel(q_ref, k_ref, g_ref, v_ref, gg_ref, gn_ref, bias_ref, o_ref, sout_ref, s_ref):
    ti = pl.program_id(1)
    C, SB = GLA_CHUNK, GLA_SUB
    NB = C // SB
    NP = GLA_HEADS // 2

    @pl.when(ti == 0)
    def _():
        s_ref[...] = jnp.zeros_like(s_ref)

    ltri = (lax.broadcasted_iota(jnp.int32, (C, 2 * C), 0) >= lax.broadcasted_iota(jnp.int32, (C, 2 * C), 1) % C
            ).astype(F32).astype(BF16)
    lane = lax.broadcasted_iota(jnp.int32, (1, LANES), 1)
    mask_a = (lane < GLA_DK).astype(F32)
    mask_b = 1.0 - mask_a
    col8 = lax.broadcasted_iota(jnp.int32, (SB, LANES), 1) % C
    colblk, colmod = col8 // SB, col8 % SB
    gn = gn_ref[...]
    zk = jnp.zeros((GLA_DK, GLA_DV), BF16)
    off1 = [sum(C - SB * (jj + 1) for jj in range(j)) for j in range(NB - 1)]

    def sub_bcast(x, j):
        x3 = x.reshape(NB, SB, LANES)
        return jnp.broadcast_to(x3[:, j:j + 1, :], (NB, SB, LANES)).reshape(C, LANES)

    def heads(x):
        return jnp.concatenate([x * mask_a, x * mask_b], axis=0).astype(BF16)

    def body(it, carry):
        rows = [pl.ds(pl.multiple_of((it * GLA_UNROLL + cc) * C, C), C) for cc in range(GLA_UNROLL)]
        units = [(cc, p) for cc in range(GLA_UNROLL) for p in range(NP)]
        kl = [slice(p * LANES, (p + 1) * LANES) for p in range(NP)]
        vl = [slice(h * GLA_DV, (h + 1) * GLA_DV) for h in range(GLA_HEADS)]

        b_all = []
        for cc in range(GLA_UNROLL):
            g2 = g_ref[rows[cc], :] * LOG2E
            g_hi = g2.astype(BF16)
            g_lo = (g2 - g_hi.astype(F32)).astype(BF16)
            b_all.append(_dot(ltri, jnp.concatenate([g_hi, g_lo], axis=0)))

        pre = {}
        for cc, p in units:
            q, k, b = q_ref[rows[cc], kl[p]], k_ref[rows[cc], kl[p]], b_all[cc][:, kl[p]]
            b_last = b[C - 1:C, :]
            k1 = k * jnp.exp2(sub_bcast(b, SB - 1) - b)
            lhs1 = jnp.concatenate(
                [q[SB * (j + 1):, :] * jnp.exp2(b[SB * (j + 1):, :] - b[SB * j + SB - 1:SB * j + SB, :])
                 for j in range(NB - 1)], axis=0)
            lhs2 = jnp.concatenate(
                [q * jnp.exp2((b - sub_bcast(b, jj)) + jnp.tile(bias_ref[jj], (NB, 1))) for jj in range(SB)],
                axis=0)
            kd = k * jnp.exp2(b_last - b)
            pre[cc, p] = (lhs1.astype(BF16), heads(k1), lhs2.astype(BF16), heads(k),
                          (q * jnp.exp2(b)).astype(BF16), kd, b_last)

        outs = {u: (_dot_nt(pre[u][0], pre[u][1]), _dot_nt(pre[u][2], pre[u][3])) for u in units}

        lo = {}
        for u in units:
            out1, out2 = outs[u]
            a_rows = []
            for i in range(NB):
                d = out2[(SB - 1) * C + SB * i:(SB - 1) * C + SB * (i + 1), :]
                for jj in range(SB - 2, -1, -1):
                    d = jnp.where(colmod == jj, out2[jj * C + SB * i:jj * C + SB * (i + 1), :], d)
                acc = jnp.where(colblk == i, d, 0.0)
                for j in range(i):
                    r0 = off1[j] + SB * (i - j - 1)
                    acc = jnp.where(colblk == j, out1[r0:r0 + SB, :], acc)
                a_rows.append(acc)
            a = jnp.concatenate(a_rows, axis=0)
            lo[u] = jnp.concatenate([pre[u][4], a.astype(BF16)], axis=1)

        o_all = {}
        for cc in range(GLA_UNROLL):
            mts = [jnp.concatenate([pre[cc, p][5], jnp.broadcast_to(pre[cc, p][6], (C, LANES))], axis=0).T
                   for p in range(NP)]
            for p in range(NP):
                ha, hb = 2 * p, 2 * p + 1
                v_a, v_b = v_ref[rows[cc], vl[ha]], v_ref[rows[cc], vl[hb]]
                s_a, s_b = s_ref[ha], s_ref[hb]
                o_all[cc, ha] = _dot(lo[cc, p], jnp.concatenate([s_a.astype(BF16), zk, v_a, zk], axis=0))
                o_all[cc, hb] = _dot(lo[cc, p], jnp.concatenate([zk, s_b.astype(BF16), zk, v_b], axis=0))
                kdt = mts[p][:, :C].astype(BF16)
                dec = jnp.exp2(mts[p][:, C:C + 1])
                s_ref[ha] = dec[:GLA_DK] * s_a + _dot(kdt[:GLA_DK], v_a)
                s_ref[hb] = dec[GLA_DK:] * s_b + _dot(kdt[GLA_DK:], v_b)

        for cc in range(GLA_UNROLL):
            for h in range(GLA_HEADS):
                gg = gg_ref[rows[cc], vl[h]]
                o_ref[rows[cc], vl[h]] = (_rms(o_all[cc, h], gn) * (gg * _sigmoid(gg))).astype(BF16)
        return carry

    lax.fori_loop(0, GLA_TILE // (C * GLA_UNROLL), body, 0)

    @pl.when(ti == pl.num_programs(1) - 1)
    def _():
        sout_ref[0] = s_ref[...]


def _gla_prompt(q, k, g, v, gg, gn, batch, seq):
    nt = seq // GLA_TILE
    blk = lambda w_: pl.BlockSpec((GLA_TILE, w_), lambda b, t: (b * nt + t, 0))
    r = np.arange(GLA_SUB)
    bias = np.where(r[None, :, None] >= r[:, None, None], 0.0, NEG) * np.ones((1, 1, LANES))
    return pl.pallas_call(
        _gla_kernel,
        grid=(batch, nt),
        in_specs=[blk(GLA_QK), blk(GLA_QK), blk(GLA_QK), blk(GLA_WIDTH), blk(GLA_WIDTH),
                  pl.BlockSpec((1, GLA_DV), lambda b, t: (0, 0)),
                  pl.BlockSpec((GLA_SUB, GLA_SUB, LANES), lambda b, t: (0, 0, 0))],
        out_specs=[blk(GLA_WIDTH),
                   pl.BlockSpec((1, GLA_HEADS, GLA_DK, GLA_DV), lambda b, t: (b, 0, 0, 0))],
        out_shape=[jax.ShapeDtypeStruct((batch * seq, GLA_WIDTH), BF16),
                   jax.ShapeDtypeStruct((batch, GLA_HEADS, GLA_DK, GLA_DV), F32)],
        scratch_shapes=[pltpu.VMEM((GLA_HEADS, GLA_DK, GLA_DV), F32)],
        compiler_params=pltpu.CompilerParams(
            dimension_semantics=("arbitrary", "arbitrary"), vmem_limit_bytes=VMEM_LIMIT),
        name="gla_prompt",
    )(q, k, g, v, gg, gn, jnp.asarray(bias, F32))


def _swa_kernel(sink_ref, q_ref, kp_ref, kc_ref, vp_ref, vc_ref, bias_ref, o_ref, s_scr, p_scr, r_scr):
    W = WINDOW
    kk = jnp.concatenate([kp_ref[...], kc_ref[...]], axis=0)
    vv = jnp.concatenate([vp_ref[...], vc_ref[...]], axis=0)
    kk_sw = pltpu.roll(kk, SWA_HEAD_DIM, 1)
    vv_sw = pltpu.roll(vv, SWA_HEAD_DIM, 1)
    lo = lax.broadcasted_iota(jnp.int32, (2 * W, LANES), 1) < SWA_HEAD_DIM

    def ext(x, x_sw, kv):
        a = jnp.where(lo, x if kv == 0 else x_sw, 0.0)
        b = jnp.where(lo, 0.0, x_sw if kv == 0 else x)
        return jnp.concatenate([a, b], axis=0).astype(BF16)

    lane_q = lax.broadcasted_iota(jnp.int32, (SWA_ROWS, LANES), 1) < SWA_HEAD_DIM

    npair = SWA_HEADS // 2
    k_ext = [ext(kk, kk_sw, kv) for kv in range(SWA_KV_HEADS)]
    v_ext = [ext(vv, vv_sw, kv) for kv in range(SWA_KV_HEADS)]
    for p in range(npair):
        s_scr[p] = _dot_nt(q_ref[:, p * LANES:(p + 1) * LANES], k_ext[p // (SWA_GROUP // 2)])
    for p in range(npair):
        for r0 in range(0, W, SWA_ROWS):
            rs = slice(r0, r0 + SWA_ROWS)
            rden = []
            for hh in range(2):
                cs = slice(hh * 2 * W, (hh + 1) * 2 * W)
                sink = sink_ref[2 * p + hh] * LOG2E
                sh = s_scr[p, rs, cs] + bias_ref[rs, :]
                m = jnp.maximum(jnp.max(sh, axis=-1, keepdims=True), sink)
                e = jnp.exp2(sh - m)
                den = jnp.sum(e, axis=-1, keepdims=True) + jnp.exp2(sink - m)
                p_scr[p, rs, cs] = e.astype(BF16)
                rden.append(1.0 / den)
            r_scr[p, rs, :] = jnp.where(lane_q, rden[0], rden[1])
    for p in range(npair):
        o = _dot(p_scr[p], v_ext[p // (SWA_GROUP // 2)]) * r_scr[p]
        o_ref[:, p * LANES:(p + 1) * LANES] = o.astype(BF16)


def _swa_prompt(sinks, sq, sk, sv, batch, seq):
    nb = seq // WINDOW
    W = WINDOW
    cur = lambda b, n: (b * nb + n, 0)
    prev = lambda b, n: (b * nb + jnp.maximum(n - 1, 0), 0)
    kvspec = lambda im: pl.BlockSpec((W, SWA_KV), im)
    qi, kj = np.arange(W)[:, None], np.arange(2 * W)[None, :]
    band = (kj > qi) & (kj <= qi + W)
    bias = np.stack([np.where(band & (kj >= W), 0.0, NEG), np.where(band, 0.0, NEG)])
    return pl.pallas_call(
        _swa_kernel,
        grid=(batch, nb),
        in_specs=[pl.BlockSpec(memory_space=pltpu.SMEM),
                  pl.BlockSpec((W, SWA_Q), cur),
                  kvspec(prev), kvspec(cur), kvspec(prev), kvspec(cur),
                  pl.BlockSpec((None, W, 2 * W), lambda b, n: (jnp.minimum(n, 1), 0, 0))],
        out_specs=pl.BlockSpec((W, SWA_Q), cur),
        out_shape=jax.ShapeDtypeStruct((batch * seq, SWA_Q), BF16),
        scratch_shapes=[pltpu.VMEM((SWA_HEADS // 2, W, 4 * W), F32), pltpu.VMEM((SWA_HEADS // 2, W, 4 * W), BF16),
                        pltpu.VMEM((SWA_HEADS // 2, W, LANES), F32)],
        compiler_params=pltpu.CompilerParams(dimension_semantics=("arbitrary", "arbitrary"),
                                             vmem_limit_bytes=VMEM_LIMIT),
        name="swa_prompt",
    )(sinks, sq, sk, sk, sv, sv, jnp.asarray(bias, F32))


def _out_ffn_kernel(x_ref, og_ref, os_ref, wo_ref, nw_ref, wg_ref, wu_ref, wd_ref, nf_ref, y_ref, *, final):
    x = (x_ref[...] + _dot(og_ref[...], wo_ref[:GLA_WIDTH, :]) + _dot(os_ref[...], wo_ref[GLA_WIDTH:, :]))
    h = _rms(x, nw_ref[...]).astype(BF16)
    acc = x
    for c0 in range(0, D_FF, FF_CHUNK):
        a = _dot(h, wg_ref[:, c0:c0 + FF_CHUNK])
        u = _dot(h, wu_ref[:, c0:c0 + FF_CHUNK])
        acc = acc + _dot((a * _sigmoid(a) * u).astype(BF16), wd_ref[c0:c0 + FF_CHUNK, :])
    if final:
        acc = _rms(acc, nf_ref[...])
    y_ref[...] = acc


def _out_ffn(x, og, osw, wo, nw, wg, wu, wd, nf, layer, final):
    n = x.shape[0]
    row = lambda w_: pl.BlockSpec((TOK_TILE, w_), lambda i: (i, 0))
    const = lambda shp: pl.BlockSpec((None,) + shp, lambda i: (layer, 0, 0), pipeline_mode=pl.Buffered(1))
    return pl.pallas_call(
        functools.partial(_out_ffn_kernel, final=final),
        grid=(n // TOK_TILE,),
        in_specs=[row(D_MODEL), row(GLA_WIDTH), row(SWA_Q), const((D_MODEL, D_MODEL)), const((1, D_MODEL)),
                  const((D_MODEL, D_FF)), const((D_MODEL, D_FF)), const((D_FF, D_MODEL)),
                  pl.BlockSpec((1, D_MODEL), lambda i: (0, 0))],
        out_specs=row(D_MODEL),
        out_shape=jax.ShapeDtypeStruct((n, D_MODEL), F32),
        compiler_params=pltpu.CompilerParams(dimension_semantics=("arbitrary",),
                                             vmem_limit_bytes=VMEM_LIMIT),
        name="out_ffn",
    )(x, og, osw, wo, nw, wg, wu, wd, nf)


def _sample_kernel(*refs, nt, n_carried):
    (q_ref, k_ref, g_ref, v_ref, gg_ref, s0_ref, gn_ref,
     qx_ref, kn_ref, vn_ref, ck_ref, cv_ref, sink_ref) = refs[:13]
    og_ref, os_ref, sn_ref, cko_ref, cvo_ref = refs[13 + n_carried:]
    TP = SUBLANES
    q, k, g, v = q_ref[...], k_ref[...], g_ref[...], v_ref[...]
    s0 = s0_ref[...]
    tt = lax.broadcasted_iota(jnp.int32, (1, TP, 1), 1)
    b = jnp.zeros_like(g)
    for s in range(nt):
        b = b + jnp.where(tt >= s, g[:, s:s + 1, :], 0.0)
    o = jnp.einsum('gtk,gkv->gtv', (q * jnp.exp(b)).astype(BF16), s0.astype(BF16),
                   preferred_element_type=F32)
    for s in range(nt):
        dec = jnp.exp(jnp.where(tt >= s, b - b[:, s:s + 1, :], NEG))
        a = jnp.sum(q * k[:, s:s + 1, :] * dec, axis=-1, keepdims=True)
        o = o + a * v[:, s:s + 1, :]
    gg = gg_ref[...]
    og_ref[...] = _rms(o, gn_ref[...]) * (gg * _sigmoid(gg))

    b_last = b[:, nt - 1:nt, :]
    d = jnp.exp(b_last)
    d_hi = d.astype(BF16).astype(F32)
    x = jnp.where(tt < nt, k * jnp.exp(b_last - b), jnp.where(tt == nt, d_hi, jnp.where(tt == nt + 1, d - d_hi, 0.0)))
    ones = jnp.where((tt == nt) | (tt == nt + 1), 1.0, 0.0)
    y = jnp.concatenate([jnp.where(tt < nt, v, 0.0), jnp.broadcast_to(ones, v.shape)], axis=2)
    r = lax.dot_general(x.astype(BF16), y.astype(BF16), (((1,), (1,)), ((0,), (0,))),
                        preferred_element_type=F32)
    sn_ref[...] = r[:, :, GLA_DV:] * s0 + r[:, :, :GLA_DV]

    W = WINDOW
    qx = qx_ref[...]
    ck, cv = ck_ref[...], cv_ref[...]
    kn, vn = kn_ref[...], vn_ref[...]
    sink = sink_ref[...] * LOG2E
    nrow = SWA_KV_HEADS * nt * SWA_GROUP
    r = lax.broadcasted_iota(jnp.int32, (1, nrow, W), 1)
    tq = (r % (nt * SWA_GROUP)) // SWA_GROUP
    j = lax.broadcasted_iota(jnp.int32, (1, nrow, W), 2)
    sc = jnp.einsum('bqd,bsd->bqs', qx, ck.astype(BF16), preferred_element_type=F32)
    sc = jnp.where(j > tq, sc, NEG)
    tq1 = tq[:, :, 0:1]
    qf = qx.astype(F32)
    sn = [jnp.where(tq1 >= s, jnp.sum(qf * kn[:, s:s + 1, :], axis=-1, keepdims=True), NEG)
          for s in range(nt)]
    m = jnp.maximum(jnp.max(sc, axis=-1, keepdims=True), sink)
    for s in range(nt):
        m = jnp.maximum(m, sn[s])
    pc = jnp.exp2(sc - m)
    pn = [jnp.exp2(sn[s] - m) for s in range(nt)]
    den = jnp.sum(pc, axis=-1, keepdims=True) + jnp.exp2(sink - m)
    for s in range(nt):
        den = den + pn[s]
    o = jnp.einsum('bqs,bsd->bqd', pc.astype(BF16), cv.astype(BF16), preferred_element_type=F32)
    for s in range(nt):
        o = o + pn[s] * vn[:, s:s + 1, :]
    os_ref[...] = o * (1.0 / den)

    cko_ref[:, :W - nt, :] = ck[:, nt:, :]
    cko_ref[:, W - nt:, :] = kn
    cvo_ref[:, :W - nt, :] = cv[:, nt:, :]
    cvo_ref[:, W - nt:, :] = vn


def _sample_mix(q4, k4, g4, v4, gg4, state, gn, qx, kn, vn, ck, cv, sink_rows, carried, layer, nbatch, nt):
    bt = SAMPLE_BT
    G = bt * GLA_HEADS
    TP = SUBLANES
    nrow = SWA_KV_HEADS * nt * SWA_GROUP
    b3 = lambda a, c: pl.BlockSpec((G, a, c), lambda i: (i, 0, 0))
    s3 = lambda a, c: pl.BlockSpec((bt, a, c), lambda i: (i, 0, 0))
    lay_g = pl.BlockSpec((None, G, GLA_DK, GLA_DV), lambda i: (layer, i, 0, 0))
    lay_c = pl.BlockSpec((None, bt, WINDOW, SWA_KV), lambda i: (layer, i, 0, 0))
    ng = nbatch * GLA_HEADS
    n_in = 13
    return pl.pallas_call(
        functools.partial(_sample_kernel, nt=nt, n_carried=len(carried)),
        grid=(nbatch // bt,),
        in_specs=[b3(TP, GLA_DK), b3(TP, GLA_DK), b3(TP, GLA_DK),
                  b3(TP, GLA_DV), b3(TP, GLA_DV), lay_g,
                  pl.BlockSpec((1, 1, GLA_DV), lambda i: (0, 0, 0)),
                  s3(nrow, SWA_KV), s3(nt, SWA_KV), s3(nt, SWA_KV), lay_c, lay_c,
                  pl.BlockSpec((1, nrow, 1), lambda i: (0, 0, 0))]
                 + [pl.BlockSpec(memory_space=pl.ANY)] * len(carried),
        out_specs=[b3(TP, GLA_DV), s3(nrow, SWA_KV), lay_g, lay_c, lay_c],
        out_shape=[jax.ShapeDtypeStruct((ng, TP, GLA_DV), F32),
                   jax.ShapeDtypeStruct((nbatch, nrow, SWA_KV), F32),
                   jax.ShapeDtypeStruct((DEPTH, ng, GLA_DK, GLA_DV), F32),
                   jax.ShapeDtypeStruct((DEPTH, nbatch, WINDOW, SWA_KV), F32),
                   jax.ShapeDtypeStruct((DEPTH, nbatch, WINDOW, SWA_KV), F32)],
        input_output_aliases={n_in + j: 2 + j for j in range(len(carried))},
        compiler_params=pltpu.CompilerParams(dimension_semantics=("arbitrary",),
                                             vmem_limit_bytes=VMEM_LIMIT),
        name="sample_mix",
    )(q4, k4, g4, v4, gg4, state, gn, qx, kn, vn, ck, cv, sink_rows, *carried)


def _rope_tables(pos):
    half = SWA_HEAD_DIM // 2
    inv = ROPE_THETA ** (-jnp.arange(half, dtype=F32) / half)
    ang = pos.astype(F32)[:, None] * inv[None, :]
    cos, sin = jnp.cos(ang), jnp.sin(ang)
    zero = jnp.zeros_like(sin)
    rep = LANES // SWA_HEAD_DIM
    return (jnp.tile(jnp.concatenate([cos, cos], axis=1), (1, rep)),
            jnp.tile(jnp.concatenate([-sin, zero], axis=1), (1, rep)),
            jnp.tile(jnp.concatenate([zero, sin], axis=1), (1, rep)))


def kernel(x_prompt, x_sample, state_gla, cache_swa_k, cache_swa_v, norm_attn, w_in, w_gk2, b_gk2,
           gla_norm, attn_sinks, w_o, norm_ffn, w_gate, w_up, w_down, norm_final):
    batch, seq, _ = x_prompt.shape
    nbatch, nt, _ = x_sample.shape
    n_s = nbatch * nt

    splits = np.cumsum([GLA_QK, GLA_QK, GLA_WIDTH, GATE_RANK, GLA_WIDTH, SWA_Q, SWA_KV, SWA_KV])
    c_ga0, c_ga1 = int(splits[2]), int(splits[3])
    w_in_p = jnp.concatenate(
        [w_in[:, :, :c_ga0], w_in[:, :, c_ga1:], w_in[:, :, c_ga0:c_ga1],
         jnp.zeros((DEPTH, D_MODEL, LANES - GATE_RANK), w_in.dtype)], axis=2).astype(BF16)
    wg2_p = jnp.concatenate([w_gk2, jnp.zeros((DEPTH, LANES - GATE_RANK, GLA_QK), w_gk2.dtype)],
                            axis=1).astype(BF16)
    bg2 = b_gk2.reshape(DEPTH, 1, GLA_QK)
    nw_a = norm_attn.reshape(DEPTH, 1, D_MODEL)
    nw_f = norm_ffn.reshape(DEPTH, 1, D_MODEL)
    wo_b, wg_b, wu_b, wd_b = (w.astype(BF16) for w in (w_o, w_gate, w_up, w_down))
    nf = norm_final.reshape(1, D_MODEL)

    tabs_p = _rope_tables(jnp.arange(seq))
    tabs_s = _rope_tables(jnp.tile(PAST_LEN + jnp.arange(nt), TOK_TILE // nt))

    state_r = state_gla.reshape(DEPTH, nbatch * GLA_HEADS, GLA_DK, GLA_DV)
    ck_r = cache_swa_k.reshape(DEPTH, nbatch, WINDOW, SWA_KV)
    cv_r = cache_swa_v.reshape(DEPTH, nbatch, WINDOW, SWA_KV)

    xp = x_prompt.reshape(batch * seq, D_MODEL)
    xs = x_sample.reshape(n_s, D_MODEL)
    sp_l, kp_l, vp_l, carried = [], [], [], []
    tpad = ((0, 0), (0, SUBLANES - nt), (0, 0))
    for l in range(DEPTH):
        final = l == DEPTH - 1
        q, k, g, v, gg, sq, sk, sv = _in_proj(xp, nw_a, w_in_p, wg2_p, bg2, tabs_p, l)
        og, s_fin = _gla_prompt(q, k, g, v, gg, gla_norm[l].reshape(1, GLA_DV), batch, seq)
        osw = _swa_prompt(attn_sinks[l], sq, sk, sv, batch, seq)
        xp = _out_ffn(xp, og, osw, wo_b, nw_f, wg_b, wu_b, wd_b, nf, l, final)
        sp_l.append(s_fin)
        last = lambda a: a.reshape(batch, seq, SWA_KV)[:, seq - WINDOW:].reshape(
            batch, WINDOW, SWA_KV_HEADS, SWA_HEAD_DIM)
        kp_l.append(last(sk))
        vp_l.append(last(sv))

        q, k, g, v, gg, sq, sk, sv = _in_proj(xs, nw_a, w_in_p, wg2_p, bg2, tabs_s, l)

        def bhtk(a, w_):
            return a.reshape(nbatch, nt, GLA_HEADS, w_).transpose(0, 2, 1, 3).reshape(nbatch * GLA_HEADS, nt, w_)

        q4, k4, g4 = (jnp.pad(bhtk(a, GLA_DK), tpad) for a in (q, k, g))
        v4 = jnp.pad(bhtk(v.astype(F32), GLA_DV), tpad)
        gg4 = jnp.pad(bhtk(gg, GLA_DV), tpad)
        qk = sq.reshape(nbatch, nt, SWA_KV_HEADS, SWA_GROUP, SWA_HEAD_DIM).transpose(0, 2, 1, 3, 4)
        zq = jnp.zeros_like(qk[:, 0])
        qx = jnp.stack([jnp.concatenate([qk[:, 0], zq], axis=-1), jnp.concatenate([zq, qk[:, 1]], axis=-1)],
                       axis=1).reshape(nbatch, SWA_KV_HEADS * nt * SWA_GROUP, SWA_KV)
        sink_rows = jnp.broadcast_to(attn_sinks[l].reshape(SWA_KV_HEADS, 1, SWA_GROUP),
                                     (SWA_KV_HEADS, nt, SWA_GROUP)).reshape(1, -1, 1)
        og4, os4, *carried = _sample_mix(
            q4, k4, g4, v4, gg4, state_r, gla_norm[l].reshape(1, 1, GLA_DV),
            qx, sk.reshape(nbatch, nt, SWA_KV), sv.reshape(nbatch, nt, SWA_KV), ck_r, cv_r, sink_rows,
            carried, l, nbatch, nt)
        og = og4[:, :nt].reshape(nbatch, GLA_HEADS, nt, GLA_DV).transpose(0, 2, 1, 3).reshape(n_s, GLA_WIDTH)
        os5 = os4.reshape(nbatch, SWA_KV_HEADS, nt, SWA_GROUP, SWA_KV)
        osw = jnp.stack([os5[:, 0, :, :, :SWA_HEAD_DIM], os5[:, 1, :, :, SWA_HEAD_DIM:]], axis=2)
        osw = osw.reshape(n_s, SWA_Q)
        xs = _out_ffn(xs, og.astype(BF16), osw.astype(BF16), wo_b, nw_f, wg_b, wu_b, wd_b, nf, l, final)

    s_all, ck_all, cv_all = carried
    cache_shape = (DEPTH, nbatch, WINDOW, SWA_KV_HEADS, SWA_HEAD_DIM)
    return (xp.reshape(batch, seq, D_MODEL), xs.reshape(nbatch, nt, D_MODEL),
            jnp.stack(sp_l), jnp.stack(kp_l), jnp.stack(vp_l),
            s_all.reshape(DEPTH, nbatch, GLA_HEADS, GLA_DK, GLA_DV),
            ck_all.reshape(cache_shape), cv_all.reshape(cache_shape))
```

```python
import functools

import jax
import jax.numpy as jnp
import numpy as np
from jax import lax
from jax.experimental import pallas as pl
from jax.experimental.pallas import tpu as pltpu

F32 = jnp.float32
BF16 = jnp.bfloat16

D_MODEL = 1024
DEPTH = 4
PAST_LEN = 8192
GLA_HEADS = 4
GLA_DV = 128
GLA_DK = 64
GLA_QK = GLA_HEADS * GLA_DK
GLA_WIDTH = GLA_HEADS * GLA_DV
GATE_RANK = 16
GATE_NORMALIZER = 16.0
SWA_HEAD_DIM = 64
SWA_HEADS = 8
SWA_KV_HEADS = 2
SWA_GROUP = 4
SWA_Q = SWA_HEADS * SWA_HEAD_DIM
SWA_KV = SWA_KV_HEADS * SWA_HEAD_DIM
WINDOW = 128
ROPE_THETA = 10000.0
D_FF = 2816
NORM_EPS = 1e-6

LANES = 128
SUBLANES = 8
VMEM_LIMIT = 56 * 1024 * 1024

C_Q, C_K, C_V, C_GG, C_SQ, C_SK, C_SV, C_GA = 0, 256, 512, 1024, 1536, 2048, 2176, 2304
IN_COLS_P = C_GA + LANES

TOK_TILE = 512
GLA_CHUNK = 64
GLA_SUB = 8
GLA_TILE = 512
GLA_UNROLL = 4
FF_CHUNK = D_FF
SWA_ROWS = 32
SWA_TILE = 512
SAMPLE_BT = 8
NEG = -1e30
LOG2E = 1.4426950408889634


def _rms(x, w):
    ms = jnp.mean(x * x, axis=-1, keepdims=True)
    return x * lax.rsqrt(ms + NORM_EPS) * w


def _sigmoid(x):
    return 1.0 / (1.0 + jnp.exp(-x))


def _dot(a, b):
    return jnp.dot(a, b, preferred_element_type=F32)


def _dot_nt(a, b):
    return lax.dot_general(a, b, (((1,), (1,)), ((), ())), preferred_element_type=F32)


def _in_proj_kernel(x_ref, nw_ref, w_ref, wg2_ref, bg2_ref, cos_ref, sa_ref, sb_ref,
                    q_ref, k_ref, g_ref, v_ref, gg_ref, sq_ref, sk_ref, sv_ref):
    h = _rms(x_ref[...], nw_ref[...]).astype(BF16)
    z = _dot(h, w_ref[...])

    def proj(c0, c1):
        return z[:, c0:c1]

    q_ref[...] = proj(C_Q, C_K) * (GLA_DK ** -0.5)
    k_ref[...] = proj(C_K, C_V)
    v_ref[...] = proj(C_V, C_GG).astype(BF16)
    gg_ref[...] = proj(C_GG, C_SQ)
    ga = proj(C_GA, IN_COLS_P).astype(BF16)
    gx = _dot(ga, wg2_ref[...]) + bg2_ref[...]
    g_ref[...] = (jnp.minimum(gx, 0.0) - jnp.log1p(jnp.exp(-jnp.abs(gx)))) * (1.0 / GATE_NORMALIZER)

    cos, sa, sb = cos_ref[...], sa_ref[...], sb_ref[...]

    def rope(z):
        return z * cos + pltpu.roll(z, LANES - 32, 1) * sa + pltpu.roll(z, 32, 1) * sb

    for j in range(SWA_Q // (2 * LANES)):
        zq = proj(C_SQ + 2 * j * LANES, C_SQ + 2 * (j + 1) * LANES)
        for jj in range(2):
            sq_ref[:, (2 * j + jj) * LANES:(2 * j + jj + 1) * LANES] = (
                rope(zq[:, jj * LANES:(jj + 1) * LANES]) * (SWA_HEAD_DIM ** -0.5 * LOG2E)).astype(BF16)
    zkv = proj(C_SK, C_GA)
    sk_ref[...] = rope(zkv[:, :SWA_KV])
    sv_ref[...] = zkv[:, SWA_KV:]


def _in_proj(x, nw, w, wg2, bg2, tabs, layer):
    n = x.shape[0]
    nt = n // TOK_TILE
    cos, sa, sb = tabs
    ntab = cos.shape[0] // TOK_TILE
    row = lambda w_: pl.BlockSpec((TOK_TILE, w_), lambda i: (i, 0))
    tab = pl.BlockSpec((TOK_TILE, LANES), lambda i: (i % ntab, 0))
    const = lambda shp: pl.BlockSpec((None,) + shp, lambda i: (layer, 0, 0), pipeline_mode=pl.Buffered(1))
    outs = [(GLA_QK, F32), (GLA_QK, F32), (GLA_QK, F32), (GLA_WIDTH, BF16), (GLA_WIDTH, F32),
            (SWA_Q, BF16), (SWA_KV, F32), (SWA_KV, F32)]
    return pl.pallas_call(
        _in_proj_kernel,
        grid=(nt,),
        in_specs=[row(D_MODEL), const((1, D_MODEL)), const((D_MODEL, IN_COLS_P)),
                  const((LANES, GLA_QK)), const((1, GLA_QK)), tab, tab, tab],
        out_specs=[row(w_) for w_, _ in outs],
        out_shape=[jax.ShapeDtypeStruct((n, w_), dt) for w_, dt in outs],
        compiler_params=pltpu.CompilerParams(dimension_semantics=("arbitrary",),
                                             vmem_limit_bytes=VMEM_LIMIT),
        name="in_proj",
    )(x, nw, w, wg2, bg2, cos, sa, sb)


def _gla_kernel(q_ref, k_ref, g_ref, v_ref, gg_ref, gn_ref, bias_ref, o_ref, sout_ref, s_ref):
    ti = pl.program_id(1)
    C, SB = GLA_CHUNK, GLA_SUB
    NB = C // SB
    NP = GLA_HEADS // 2

    @pl.when(ti == 0)
    def _():
        s_ref[...] = jnp.zeros_like(s_ref)

    ltri = (lax.broadcasted_iota(jnp.int32, (C, 2 * C), 0) >= lax.broadcasted_iota(jnp.int32, (C, 2 * C), 1) % C
            ).astype(F32).astype(BF16)
    lane = lax.broadcasted_iota(jnp.int32, (1, LANES), 1)
    mask_a = (lane < GLA_DK).astype(F32)
    mask_b = 1.0 - mask_a
    col8 = lax.broadcasted_iota(jnp.int32, (SB, LANES), 1) % C
    colblk, colmod = col8 // SB, col8 % SB
    gn = gn_ref[...]
    zk = jnp.zeros((GLA_DK, GLA_DV), BF16)
    off1 = [sum(C - SB * (jj + 1) for jj in range(j)) for j in range(NB - 1)]

    def sub_bcast(x, j):
        x3 = x.reshape(NB, SB, LANES)
        return jnp.broadcast_to(x3[:, j:j + 1, :], (NB, SB, LANES)).reshape(C, LANES)

    def heads(x):
        return jnp.concatenate([x * mask_a, x * mask_b], axis=0).astype(BF16)

    def body(it, carry):
        rows = [pl.ds(pl.multiple_of((it * GLA_UNROLL + cc) * C, C), C) for cc in range(GLA_UNROLL)]
        units = [(cc, p) for cc in range(GLA_UNROLL) for p in range(NP)]
        kl = [slice(p * LANES, (p + 1) * LANES) for p in range(NP)]
        vl = [slice(h * GLA_DV, (h + 1) * GLA_DV) for h in range(GLA_HEADS)]

        b_all = []
        for cc in range(GLA_UNROLL):
            g2 = g_ref[rows[cc], :] * LOG2E
            g_hi = g2.astype(BF16)
            g_lo = (g2 - g_hi.astype(F32)).astype(BF16)
            b_all.append(_dot(ltri, jnp.concatenate([g_hi, g_lo], axis=0)))

        pre = {}
        for cc, p in units:
            q, k, b = q_ref[rows[cc], kl[p]], k_ref[rows[cc], kl[p]], b_all[cc][:, kl[p]]
            b_last = b[C - 1:C, :]
            k1 = k * jnp.exp2(sub_bcast(b, SB - 1) - b)
            lhs1 = jnp.concatenate(
                [q[SB * (j + 1):, :] * jnp.exp2(b[SB * (j + 1):, :] - b[SB * j + SB - 1:SB * j + SB, :])
                 for j in range(NB - 1)], axis=0)
            lhs2 = jnp.concatenate(
                [q * jnp.exp2((b - sub_bcast(b, jj)) + jnp.tile(bias_ref[jj], (NB, 1))) for jj in range(SB)],
                axis=0)
            kd = k * jnp.exp2(b_last - b)
            pre[cc, p] = (lhs1.astype(BF16), heads(k1), lhs2.astype(BF16), heads(k),
                          (q * jnp.exp2(b)).astype(BF16), kd, b_last)

        outs = {u: (_dot_nt(pre[u][0], pre[u][1]), _dot_nt(pre[u][2], pre[u][3])) for u in units}

        lo = {}
        for u in units:
            out1, out2 = outs[u]
            a_rows = []
            for i in range(NB):
                d = out2[(SB - 1) * C + SB * i:(SB - 1) * C + SB * (i + 1), :]
                for jj in range(SB - 2, -1, -1):
                    d = jnp.where(colmod == jj, out2[jj * C + SB * i:jj * C + SB * (i + 1), :], d)
                acc = jnp.where(colblk == i, d, 0.0)
                for j in range(i):
                    r0 = off1[j] + SB * (i - j - 1)
                    acc = jnp.where(colblk == j, out1[r0:r0 + SB, :], acc)
                a_rows.append(acc)
            a = jnp.concatenate(a_rows, axis=0)
            lo[u] = jnp.concatenate([pre[u][4], a.astype(BF16)], axis=1)

        o_all = {}
        for cc in range(GLA_UNROLL):
            mts = [jnp.concatenate([pre[cc, p][5], jnp.broadcast_to(pre[cc, p][6], (C, LANES))], axis=0).T
                   for p in range(NP)]
            for p in range(NP):
                ha, hb = 2 * p, 2 * p + 1
                v_a, v_b = v_ref[rows[cc], vl[ha]], v_ref[rows[cc], vl[hb]]
                s_a, s_b = s_ref[ha], s_ref[hb]
                o_all[cc, ha] = _dot(lo[cc, p], jnp.concatenate([s_a.astype(BF16), zk, v_a, zk], axis=0))
                o_all[cc, hb] = _dot(lo[cc, p], jnp.concatenate([zk, s_b.astype(BF16), zk, v_b], axis=0))
                kdt = mts[p][:, :C].astype(BF16)
                dec = jnp.exp2(mts[p][:, C:C + 1])
                s_ref[ha] = dec[:GLA_DK] * s_a + _dot(kdt[:GLA_DK], v_a)
                s_ref[hb] = dec[GLA_DK:] * s_b + _dot(kdt[GLA_DK:], v_b)

        for cc in range(GLA_UNROLL):
            for h in range(GLA_HEADS):
                gg = gg_ref[rows[cc], vl[h]]
                o_ref[rows[cc], vl[h]] = (_rms(o_all[cc, h], gn) * (gg * _sigmoid(gg))).astype(BF16)
        return carry

    lax.fori_loop(0, GLA_TILE // (C * GLA_UNROLL), body, 0)

    @pl.when(ti == pl.num_programs(1) - 1)
    def _():
        sout_ref[0] = s_ref[...]


def _gla_prompt(q, k, g, v, gg, gn, batch, seq):
    nt = seq // GLA_TILE
    blk = lambda w_: pl.BlockSpec((GLA_TILE, w_), lambda b, t: (b * nt + t, 0))
    r = np.arange(GLA_SUB)
    bias = np.where(r[None, :, None] >= r[:, None, None], 0.0, NEG) * np.ones((1, 1, LANES))
    return pl.pallas_call(
        _gla_kernel,
        grid=(batch, nt),
        in_specs=[blk(GLA_QK), blk(GLA_QK), blk(GLA_QK), blk(GLA_WIDTH), blk(GLA_WIDTH),
                  pl.BlockSpec((1, GLA_DV), lambda b, t: (0, 0)),
                  pl.BlockSpec((GLA_SUB, GLA_SUB, LANES), lambda b, t: (0, 0, 0))],
        out_specs=[blk(GLA_WIDTH),
                   pl.BlockSpec((1, GLA_HEADS, GLA_DK, GLA_DV), lambda b, t: (b, 0, 0, 0))],
        out_shape=[jax.ShapeDtypeStruct((batch * seq, GLA_WIDTH), BF16),
                   jax.ShapeDtypeStruct((batch, GLA_HEADS, GLA_DK, GLA_DV), F32)],
        scratch_shapes=[pltpu.VMEM((GLA_HEADS, GLA_DK, GLA_DV), F32)],
        compiler_params=pltpu.CompilerParams(
            dimension_semantics=("arbitrary", "arbitrary"), vmem_limit_bytes=VMEM_LIMIT),
        name="gla_prompt",
    )(q, k, g, v, gg, gn, jnp.asarray(bias, F32))


def _swa_kernel(sink_ref, q_ref, kp_ref, kc_ref, vp_ref, vc_ref, bias_ref, o_ref,
                ke_scr, ve_scr, s_scr, p_scr, r_scr):
    W = WINDOW
    t = pl.program_id(1)
    nblk = SWA_TILE // W
    kk = jnp.concatenate([kp_ref[...], kc_ref[...]], axis=0)
    vv = jnp.concatenate([vp_ref[...], vc_ref[...]], axis=0)
    lo = lax.broadcasted_iota(jnp.int32, kk.shape, 1) < SWA_HEAD_DIM
    for x, scr in ((kk, ke_scr), (vv, ve_scr)):
        x_sw = pltpu.roll(x, SWA_HEAD_DIM, 1)
        scr[0, 0] = jnp.where(lo, x, 0.0).astype(BF16)
        scr[0, 1] = jnp.where(lo, 0.0, x_sw).astype(BF16)
        scr[1, 0] = jnp.where(lo, x_sw, 0.0).astype(BF16)
        scr[1, 1] = jnp.where(lo, 0.0, x).astype(BF16)

    lane_q = lax.broadcasted_iota(jnp.int32, (SWA_ROWS, LANES), 1) < SWA_HEAD_DIM
    npair = SWA_HEADS // 2

    def block(j, carry):
        qrows = pl.ds(pl.multiple_of(j * W, W), W)
        krows = pl.ds(pl.multiple_of(j * W, W), 2 * W)
        bias_blk = bias_ref.at[jnp.minimum(t * nblk + j, 1)]
        for p in range(npair):
            kv = p // (SWA_GROUP // 2)
            q = q_ref[qrows, p * LANES:(p + 1) * LANES]
            for hh in range(2):
                s_scr[p, :, hh * 2 * W:(hh + 1) * 2 * W] = _dot_nt(q, ke_scr[kv, hh, krows, :])
        for p in range(npair):
            for r0 in range(0, W, SWA_ROWS):
                rs = slice(r0, r0 + SWA_ROWS)
                rden = []
                for hh in range(2):
                    cs = slice(hh * 2 * W, (hh + 1) * 2 * W)
                    sink = sink_ref[2 * p + hh] * LOG2E
                    sh = s_scr[p, rs, cs] + bias_blk[rs, :]
                    m = jnp.maximum(jnp.max(sh, axis=-1, keepdims=True), sink)
                    e = jnp.exp2(sh - m)
                    den = jnp.sum(e, axis=-1, keepdims=True) + jnp.exp2(sink - m)
                    p_scr[p, rs, cs] = e.astype(BF16)
                    rden.append(1.0 / den)
                r_scr[p, rs, :] = jnp.where(lane_q, rden[0], rden[1])
        for p in range(npair):
            kv = p // (SWA_GROUP // 2)
            o = (_dot(p_scr[p, :, :2 * W], ve_scr[kv, 0, krows, :])
                 + _dot(p_scr[p, :, 2 * W:], ve_scr[kv, 1, krows, :])) * r_scr[p]
            o_ref[qrows, p * LANES:(p + 1) * LANES] = o.astype(BF16)
        return carry

    lax.fori_loop(0, nblk, block, 0)


def _swa_prompt(sinks, sq, sk, sv, batch, seq):
    W = WINDOW
    nt = seq // SWA_TILE
    nblk = SWA_TILE // W
    cur = lambda b, t: (b * nt + t, 0)
    prev = lambda b, t: (b * nt * nblk + jnp.maximum(t * nblk - 1, 0), 0)
    qi, kj = np.arange(W)[:, None], np.arange(2 * W)[None, :]
    band = (kj > qi) & (kj <= qi + W)
    bias = np.stack([np.where(band & (kj >= W), 0.0, NEG), np.where(band, 0.0, NEG)])
    ext_scr = pltpu.VMEM((SWA_KV_HEADS, 2, W + SWA_TILE, LANES), BF16)
    return pl.pallas_call(
        _swa_kernel,
        grid=(batch, nt),
        in_specs=[pl.BlockSpec(memory_space=pltpu.SMEM),
                  pl.BlockSpec((SWA_TILE, SWA_Q), cur),
                  pl.BlockSpec((W, SWA_KV), prev), pl.BlockSpec((SWA_TILE, SWA_KV), cur),
                  pl.BlockSpec((W, SWA_KV), prev), pl.BlockSpec((SWA_TILE, SWA_KV), cur),
                  pl.BlockSpec((2, W, 2 * W), lambda b, t: (0, 0, 0))],
        out_specs=pl.BlockSpec((SWA_TILE, SWA_Q), cur),
        out_shape=jax.ShapeDtypeStruct((batch * seq, SWA_Q), BF16),
        scratch_shapes=[ext_scr, ext_scr,
                        pltpu.VMEM((SWA_HEADS // 2, W, 4 * W), F32), pltpu.VMEM((SWA_HEADS // 2, W, 4 * W), BF16),
                        pltpu.VMEM((SWA_HEADS // 2, W, LANES), F32)],
        compiler_params=pltpu.CompilerParams(dimension_semantics=("arbitrary", "arbitrary"),
                                             vmem_limit_bytes=VMEM_LIMIT),
        name="swa_prompt",
    )(sinks, sq, sk, sk, sv, sv, jnp.asarray(bias, F32))


def _out_ffn_kernel(x_ref, og_ref, os_ref, wo_ref, nw_ref, wg_ref, wu_ref, wd_ref, nf_ref, y_ref, *, final):
    x = (x_ref[...] + _dot(og_ref[...], wo_ref[:GLA_WIDTH, :]) + _dot(os_ref[...], wo_ref[GLA_WIDTH:, :]))
    h = _rms(x, nw_ref[...]).astype(BF16)
    acc = x
    for c0 in range(0, D_FF, FF_CHUNK):
        a = _dot(h, wg_ref[:, c0:c0 + FF_CHUNK])
        u = _dot(h, wu_ref[:, c0:c0 + FF_CHUNK])
        acc = acc + _dot((a * _sigmoid(a) * u).astype(BF16), wd_ref[c0:c0 + FF_CHUNK, :])
    if final:
        acc = _rms(acc, nf_ref[...])
    y_ref[...] = acc


def _out_ffn(x, og, osw, wo, nw, wg, wu, wd, nf, layer, final):
    n = x.shape[0]
    row = lambda w_: pl.BlockSpec((TOK_TILE, w_), lambda i: (i, 0))
    const = lambda shp: pl.BlockSpec((None,) + shp, lambda i: (layer, 0, 0), pipeline_mode=pl.Buffered(1))
    return pl.pallas_call(
        functools.partial(_out_ffn_kernel, final=final),
        grid=(n // TOK_TILE,),
        in_specs=[row(D_MODEL), row(GLA_WIDTH), row(SWA_Q), const((D_MODEL, D_MODEL)), const((1, D_MODEL)),
                  const((D_MODEL, D_FF)), const((D_MODEL, D_FF)), const((D_FF, D_MODEL)),
                  pl.BlockSpec((1, D_MODEL), lambda i: (0, 0))],
        out_specs=row(D_MODEL),
        out_shape=jax.ShapeDtypeStruct((n, D_MODEL), F32),
        compiler_params=pltpu.CompilerParams(dimension_semantics=("arbitrary",),
                                             vmem_limit_bytes=VMEM_LIMIT),
        name="out_ffn",
    )(x, og, osw, wo, nw, wg, wu, wd, nf)


def _sample_kernel(*refs, nt, n_carried):
    (q_ref, k_ref, g_ref, v_ref, gg_ref, s0_ref, gn_ref,
     qx_ref, kn_ref, vn_ref, ck_ref, cv_ref, sink_ref) = refs[:13]
    og_ref, os_ref, sn_ref, cko_ref, cvo_ref = refs[13 + n_carried:]
    TP = SUBLANES
    q, k, g, v = q_ref[...], k_ref[...], g_ref[...], v_ref[...]
    s0 = s0_ref[...]
    tt = lax.broadcasted_iota(jnp.int32, (1, TP, 1), 1)
    b = jnp.zeros_like(g)
    for s in range(nt):
        b = b + jnp.where(tt >= s, g[:, s:s + 1, :], 0.0)
    o = jnp.einsum('gtk,gkv->gtv', (q * jnp.exp(b)).astype(BF16), s0.astype(BF16),
                   preferred_element_type=F32)
    for s in range(nt):
        dec = jnp.exp(jnp.where(tt >= s, b - b[:, s:s + 1, :], NEG))
        a = jnp.sum(q * k[:, s:s + 1, :] * dec, axis=-1, keepdims=True)
        o = o + a * v[:, s:s + 1, :]
    gg = gg_ref[...]
    og_ref[...] = _rms(o, gn_ref[...]) * (gg * _sigmoid(gg))

    b_last = b[:, nt - 1:nt, :]
    d = jnp.exp(b_last)
    d_hi = d.astype(BF16).astype(F32)
    x = jnp.where(tt < nt, k * jnp.exp(b_last - b), jnp.where(tt == nt, d_hi, jnp.where(tt == nt + 1, d - d_hi, 0.0)))
    ones = jnp.where((tt == nt) | (tt == nt + 1), 1.0, 0.0)
    y = jnp.concatenate([jnp.where(tt < nt, v, 0.0), jnp.broadcast_to(ones, v.shape)], axis=2)
    r = lax.dot_general(x.astype(BF16), y.astype(BF16), (((1,), (1,)), ((0,), (0,))),
                        preferred_element_type=F32)
    sn_ref[...] = r[:, :, GLA_DV:] * s0 + r[:, :, :GLA_DV]

    W = WINDOW
    qx = qx_ref[...]
    ck, cv = ck_ref[...], cv_ref[...]
    kn, vn = kn_ref[...], vn_ref[...]
    sink = sink_ref[...] * LOG2E
    nrow = SWA_KV_HEADS * nt * SWA_GROUP
    r = lax.broadcasted_iota(jnp.int32, (1, nrow, W), 1)
    tq = (r % (nt * SWA_GROUP)) // SWA_GROUP
    j = lax.broadcasted_iota(jnp.int32, (1, nrow, W), 2)
    sc = jnp.einsum('bqd,bsd->bqs', qx, ck.astype(BF16), preferred_element_type=F32)
    sc = jnp.where(j > tq, sc, NEG)
    tq1 = tq[:, :, 0:1]
    qf = qx.astype(F32)
    sn = [jnp.where(tq1 >= s, jnp.sum(qf * kn[:, s:s + 1, :], axis=-1, keepdims=True), NEG)
          for s in range(nt)]
    m = jnp.maximum(jnp.max(sc, axis=-1, keepdims=True), sink)
    for s in range(nt):
        m = jnp.maximum(m, sn[s])
    pc = jnp.exp2(sc - m)
    pn = [jnp.exp2(sn[s] - m) for s in range(nt)]
    den = jnp.sum(pc, axis=-1, keepdims=True) + jnp.exp2(sink - m)
    for s in range(nt):
        den = den + pn[s]
    o = jnp.einsum('bqs,bsd->bqd', pc.astype(BF16), cv.astype(BF16), preferred_element_type=F32)
    for s in range(nt):
        o = o + pn[s] * vn[:, s:s + 1, :]
    os_ref[...] = o * (1.0 / den)

    cko_ref[:, :W - nt, :] = ck[:, nt:, :]
    cko_ref[:, W - nt:, :] = kn
    cvo_ref[:, :W - nt, :] = cv[:, nt:, :]
    cvo_ref[:, W - nt:, :] = vn


def _sample_mix(q4, k4, g4, v4, gg4, state, gn, qx, kn, vn, ck, cv, sink_rows, carried, layer, nbatch, nt):
    bt = SAMPLE_BT
    G = bt * GLA_HEADS
    TP = SUBLANES
    nrow = SWA_KV_HEADS * nt * SWA_GROUP
    b3 = lambda a, c: pl.BlockSpec((G, a, c), lambda i: (i, 0, 0))
    s3 = lambda a, c: pl.BlockSpec((bt, a, c), lambda i: (i, 0, 0))
    lay_g = pl.BlockSpec((None, G, GLA_DK, GLA_DV), lambda i: (layer, i, 0, 0))
    lay_c = pl.BlockSpec((None, bt, WINDOW, SWA_KV), lambda i: (layer, i, 0, 0))
    ng = nbatch * GLA_HEADS
    n_in = 13
    return pl.pallas_call(
        functools.partial(_sample_kernel, nt=nt, n_carried=len(carried)),
        grid=(nbatch // bt,),
        in_specs=[b3(TP, GLA_DK), b3(TP, GLA_DK), b3(TP, GLA_DK),
                  b3(TP, GLA_DV), b3(TP, GLA_DV), lay_g,
                  pl.BlockSpec((1, 1, GLA_DV), lambda i: (0, 0, 0)),
                  s3(nrow, SWA_KV), s3(nt, SWA_KV), s3(nt, SWA_KV), lay_c, lay_c,
                  pl.BlockSpec((1, nrow, 1), lambda i: (0, 0, 0))]
                 + [pl.BlockSpec(memory_space=pl.ANY)] * len(carried),
        out_specs=[b3(TP, GLA_DV), s3(nrow, SWA_KV), lay_g, lay_c, lay_c],
        out_shape=[jax.ShapeDtypeStruct((ng, TP, GLA_DV), F32),
                   jax.ShapeDtypeStruct((nbatch, nrow, SWA_KV), F32),
                   jax.ShapeDtypeStruct((DEPTH, ng, GLA_DK, GLA_DV), F32),
                   jax.ShapeDtypeStruct((DEPTH, nbatch, WINDOW, SWA_KV), F32),
                   jax.ShapeDtypeStruct((DEPTH, nbatch, WINDOW, SWA_KV), F32)],
        input_output_aliases={n_in + j: 2 + j for j in range(len(carried))},
        compiler_params=pltpu.CompilerParams(dimension_semantics=("arbitrary",),
                                             vmem_limit_bytes=VMEM_LIMIT),
        name="sample_mix",
    )(q4, k4, g4, v4, gg4, state, gn, qx, kn, vn, ck, cv, sink_rows, *carried)


def _rope_tables(pos):
    half = SWA_HEAD_DIM // 2
    inv = ROPE_THETA ** (-jnp.arange(half, dtype=F32) / half)
    ang = pos.astype(F32)[:, None] * inv[None, :]
    cos, sin = jnp.cos(ang), jnp.sin(ang)
    zero = jnp.zeros_like(sin)
    rep = LANES // SWA_HEAD_DIM
    return (jnp.tile(jnp.concatenate([cos, cos], axis=1), (1, rep)),
            jnp.tile(jnp.concatenate([-sin, zero], axis=1), (1, rep)),
            jnp.tile(jnp.concatenate([zero, sin], axis=1), (1, rep)))


def kernel(x_prompt, x_sample, state_gla, cache_swa_k, cache_swa_v, norm_attn, w_in, w_gk2, b_gk2,
           gla_norm, attn_sinks, w_o, norm_ffn, w_gate, w_up, w_down, norm_final):
    batch, seq, _ = x_prompt.shape
    nbatch, nt, _ = x_sample.shape
    n_s = nbatch * nt

    splits = np.cumsum([GLA_QK, GLA_QK, GLA_WIDTH, GATE_RANK, GLA_WIDTH, SWA_Q, SWA_KV, SWA_KV])
    c_ga0, c_ga1 = int(splits[2]), int(splits[3])
    w_in_p = jnp.concatenate(
        [w_in[:, :, :c_ga0], w_in[:, :, c_ga1:], w_in[:, :, c_ga0:c_ga1],
         jnp.zeros((DEPTH, D_MODEL, LANES - GATE_RANK), w_in.dtype)], axis=2).astype(BF16)
    wg2_p = jnp.concatenate([w_gk2, jnp.zeros((DEPTH, LANES - GATE_RANK, GLA_QK), w_gk2.dtype)],
                            axis=1).astype(BF16)
    bg2 = b_gk2.reshape(DEPTH, 1, GLA_QK)
    nw_a = norm_attn.reshape(DEPTH, 1, D_MODEL)
    nw_f = norm_ffn.reshape(DEPTH, 1, D_MODEL)
    wo_b, wg_b, wu_b, wd_b = (w.astype(BF16) for w in (w_o, w_gate, w_up, w_down))
    nf = norm_final.reshape(1, D_MODEL)

    tabs_p = _rope_tables(jnp.arange(seq))
    tabs_s = _rope_tables(jnp.tile(PAST_LEN + jnp.arange(nt), TOK_TILE // nt))

    state_r = state_gla.reshape(DEPTH, nbatch * GLA_HEADS, GLA_DK, GLA_DV)
    ck_r = cache_swa_k.reshape(DEPTH, nbatch, WINDOW, SWA_KV)
    cv_r = cache_swa_v.reshape(DEPTH, nbatch, WINDOW, SWA_KV)

    xp = x_prompt.reshape(batch * seq, D_MODEL)
    xs = x_sample.reshape(n_s, D_MODEL)
    sp_l, kp_l, vp_l, carried = [], [], [], []
    tpad = ((0, 0), (0, SUBLANES - nt), (0, 0))
    for l in range(DEPTH):
        final = l == DEPTH - 1
        q, k, g, v, gg, sq, sk, sv = _in_proj(xp, nw_a, w_in_p, wg2_p, bg2, tabs_p, l)
        og, s_fin = _gla_prompt(q, k, g, v, gg, gla_norm[l].reshape(1, GLA_DV), batch, seq)
        osw = _swa_prompt(attn_sinks[l], sq, sk, sv, batch, seq)
        xp = _out_ffn(xp, og, osw, wo_b, nw_f, wg_b, wu_b, wd_b, nf, l, final)
        sp_l.append(s_fin)
        last = lambda a: a.reshape(batch, seq, SWA_KV)[:, seq - WINDOW:].reshape(
            batch, WINDOW, SWA_KV_HEADS, SWA_HEAD_DIM)
        kp_l.append(last(sk))
        vp_l.append(last(sv))

        q, k, g, v, gg, sq, sk, sv = _in_proj(xs, nw_a, w_in_p, wg2_p, bg2, tabs_s, l)

        def bhtk(a, w_):
            return a.reshape(nbatch, nt, GLA_HEADS, w_).transpose(0, 2, 1, 3).reshape(nbatch * GLA_HEADS, nt, w_)

        q4, k4, g4 = (jnp.pad(bhtk(a, GLA_DK), tpad) for a in (q, k, g))
        v4 = jnp.pad(bhtk(v.astype(F32), GLA_DV), tpad)
        gg4 = jnp.pad(bhtk(gg, GLA_DV), tpad)
        qk = sq.reshape(nbatch, nt, SWA_KV_HEADS, SWA_GROUP, SWA_HEAD_DIM).transpose(0, 2, 1, 3, 4)
        zq = jnp.zeros_like(qk[:, 0])
        qx = jnp.stack([jnp.concatenate([qk[:, 0], zq], axis=-1), jnp.concatenate([zq, qk[:, 1]], axis=-1)],
                       axis=1).reshape(nbatch, SWA_KV_HEADS * nt * SWA_GROUP, SWA_KV)
        sink_rows = jnp.broadcast_to(attn_sinks[l].reshape(SWA_KV_HEADS, 1, SWA_GROUP),
                                     (SWA_KV_HEADS, nt, SWA_GROUP)).reshape(1, -1, 1)
        og4, os4, *carried = _sample_mix(
            q4, k4, g4, v4, gg4, state_r, gla_norm[l].reshape(1, 1, GLA_DV),
            qx, sk.reshape(nbatch, nt, SWA_KV), sv.reshape(nbatch, nt, SWA_KV), ck_r, cv_r, sink_rows,
            carried, l, nbatch, nt)
        og = og4[:, :nt].reshape(nbatch, GLA_HEADS, nt, GLA_DV).transpose(0, 2, 1, 3).reshape(n_s, GLA_WIDTH)
        os5 = os4.reshape(nbatch, SWA_KV_HEADS, nt, SWA_GROUP, SWA_KV)
        osw = jnp.stack([os5[:, 0, :, :, :SWA_HEAD_DIM], os5[:, 1, :, :, SWA_HEAD_DIM:]], axis=2)
        osw = osw.reshape(n_s, SWA_Q)
        xs = _out_ffn(xs, og.astype(BF16), osw.astype(BF16), wo_b, nw_f, wg_b, wu_b, wd_b, nf, l, final)

    s_all, ck_all, cv_all = carried
    cache_shape = (DEPTH, nbatch, WINDOW, SWA_KV_HEADS, SWA_HEAD_DIM)
    return (xp.reshape(batch, seq, D_MODEL), xs.reshape(nbatch, nt, D_MODEL),
            jnp.stack(sp_l), jnp.stack(kp_l), jnp.stack(vp_l),
            s_all.reshape(DEPTH, nbatch, GLA_HEADS, GLA_DK, GLA_DV),
            ck_all.reshape(cache_shape), cv_all.reshape(cache_shape))
```

```python
import functools

import jax
import jax.numpy as jnp
import numpy as np
from jax import lax
from jax.experimental import pallas as pl
from jax.experimental.pallas import tpu as pltpu

F32 = jnp.float32
BF16 = jnp.bfloat16

D_MODEL = 1024
DEPTH = 4
PAST_LEN = 8192
GLA_HEADS = 4
GLA_DV = 128
GLA_DK = 64
GLA_QK = GLA_HEADS * GLA_DK
GLA_WIDTH = GLA_HEADS * GLA_DV
GATE_RANK = 16
GATE_NORMALIZER = 16.0
SWA_HEAD_DIM = 64
SWA_HEADS = 8
SWA_KV_HEADS = 2
SWA_GROUP = 4
SWA_Q = SWA_HEADS * SWA_HEAD_DIM
SWA_KV = SWA_KV_HEADS * SWA_HEAD_DIM
WINDOW = 128
ROPE_THETA = 10000.0
D_FF = 2816
NORM_EPS = 1e-6

LANES = 128
SUBLANES = 8
VMEM_LIMIT = 56 * 1024 * 1024

C_Q, C_K, C_V, C_GG, C_SQ, C_SK, C_SV, C_GA = 0, 256, 512, 1024, 1536, 2048, 2176, 2304
IN_COLS_P = C_GA + LANES

TOK_TILE = 512
GLA_CHUNK = 64
GLA_SUB = 8
GLA_TILE = 512
GLA_UNROLL = 4
FF_CHUNK = D_FF
SWA_ROWS = 32
SWA_TILE = 512
SAMPLE_BT = 8
NEG = -1e30
LOG2E = 1.4426950408889634


def _rms(x, w):
    ms = jnp.mean(x * x, axis=-1, keepdims=True)
    return x * lax.rsqrt(ms + NORM_EPS) * w


def _sigmoid(x):
    return 1.0 / (1.0 + jnp.exp(-x))


def _dot(a, b):
    return jnp.dot(a, b, preferred_element_type=F32)


def _dot_nt(a, b):
    return lax.dot_general(a, b, (((1,), (1,)), ((), ())), preferred_element_type=F32)


def _in_proj_kernel(x_ref, nw_ref, w_ref, wg2_ref, bg2_ref, cos_ref, sa_ref, sb_ref,
                    q_ref, k_ref, g_ref, v_ref, gg_ref, sq_ref, sk_ref, sv_ref):
    h = _rms(x_ref[...], nw_ref[...]).astype(BF16)
    z = _dot(h, w_ref[...])

    def proj(c0, c1):
        return z[:, c0:c1]

    q_ref[...] = proj(C_Q, C_K) * (GLA_DK ** -0.5)
    k_ref[...] = proj(C_K, C_V)
    v_ref[...] = proj(C_V, C_GG).astype(BF16)
    gg_ref[...] = proj(C_GG, C_SQ)
    ga = proj(C_GA, IN_COLS_P).astype(BF16)
    gx = _dot(ga, wg2_ref[...]) + bg2_ref[...]
    g_ref[...] = (jnp.minimum(gx, 0.0) - jnp.log1p(jnp.exp(-jnp.abs(gx)))) * (1.0 / GATE_NORMALIZER)

    cos, sa, sb = cos_ref[...], sa_ref[...], sb_ref[...]

    def rope(z):
        return z * cos + pltpu.roll(z, LANES - 32, 1) * sa + pltpu.roll(z, 32, 1) * sb

    for j in range(SWA_Q // (2 * LANES)):
        zq = proj(C_SQ + 2 * j * LANES, C_SQ + 2 * (j + 1) * LANES)
        for jj in range(2):
            sq_ref[:, (2 * j + jj) * LANES:(2 * j + jj + 1) * LANES] = (
                rope(zq[:, jj * LANES:(jj + 1) * LANES]) * (SWA_HEAD_DIM ** -0.5 * LOG2E)).astype(BF16)
    zkv = proj(C_SK, C_GA)
    sk_ref[...] = rope(zkv[:, :SWA_KV])
    sv_ref[...] = zkv[:, SWA_KV:]


def _in_proj(x, nw, w, wg2, bg2, tabs, layer):
    n = x.shape[0]
    nt = n // TOK_TILE
    cos, sa, sb = tabs
    ntab = cos.shape[0] // TOK_TILE
    row = lambda w_: pl.BlockSpec((TOK_TILE, w_), lambda i: (i, 0))
    tab = pl.BlockSpec((TOK_TILE, LANES), lambda i: (i % ntab, 0))
    const = lambda shp: pl.BlockSpec((None,) + shp, lambda i: (layer, 0, 0), pipeline_mode=pl.Buffered(1))
    outs = [(GLA_QK, F32), (GLA_QK, F32), (GLA_QK, F32), (GLA_WIDTH, BF16), (GLA_WIDTH, F32),
            (SWA_Q, BF16), (SWA_KV, F32), (SWA_KV, F32)]
    return pl.pallas_call(
        _in_proj_kernel,
        grid=(nt,),
        in_specs=[row(D_MODEL), const((1, D_MODEL)), const((D_MODEL, IN_COLS_P)),
                  const((LANES, GLA_QK)), const((1, GLA_QK)), tab, tab, tab],
        out_specs=[row(w_) for w_, _ in outs],
        out_shape=[jax.ShapeDtypeStruct((n, w_), dt) for w_, dt in outs],
        compiler_params=pltpu.CompilerParams(dimension_semantics=("arbitrary",),
                                             vmem_limit_bytes=VMEM_LIMIT),
        name="in_proj",
    )(x, nw, w, wg2, bg2, cos, sa, sb)


def _gla_kernel(q_ref, k_ref, g_ref, v_ref, gg_ref, gn_ref, bias_ref, o_ref, sout_ref, s_ref):
    ti = pl.program_id(1)
    C, SB = GLA_CHUNK, GLA_SUB
    NB = C // SB
    NP = GLA_HEADS // 2

    @pl.when(ti == 0)
    def _():
        s_ref[...] = jnp.zeros_like(s_ref)

    ltri = (lax.broadcasted_iota(jnp.int32, (C, 2 * C), 0) >= lax.broadcasted_iota(jnp.int32, (C, 2 * C), 1) % C
            ).astype(F32).astype(BF16)
    lane = lax.broadcasted_iota(jnp.int32, (1, LANES), 1)
    mask_a = (lane < GLA_DK).astype(F32)
    mask_b = 1.0 - mask_a
    col8 = lax.broadcasted_iota(jnp.int32, (SB, LANES), 1) % C
    colblk, colmod = col8 // SB, col8 % SB
    gn = gn_ref[...]
    zk = jnp.zeros((GLA_DK, GLA_DV), BF16)
    off1 = [sum(C - SB * (jj + 1) for jj in range(j)) for j in range(NB - 1)]

    def sub_bcast(x, j):
        x3 = x.reshape(NB, SB, LANES)
        return jnp.broadcast_to(x3[:, j:j + 1, :], (NB, SB, LANES)).reshape(C, LANES)

    def heads(x):
        return jnp.concatenate([x * mask_a, x * mask_b], axis=0).astype(BF16)

    def body(it, carry):
        rows = [pl.ds(pl.multiple_of((it * GLA_UNROLL + cc) * C, C), C) for cc in range(GLA_UNROLL)]
        units = [(cc, p) for cc in range(GLA_UNROLL) for p in range(NP)]
        kl = [slice(p * LANES, (p + 1) * LANES) for p in range(NP)]
        vl = [slice(h * GLA_DV, (h + 1) * GLA_DV) for h in range(GLA_HEADS)]

        b_all = []
        for cc in range(GLA_UNROLL):
            g2 = g_ref[rows[cc], :] * LOG2E
            g_hi = g2.astype(BF16)
            g_lo = (g2 - g_hi.astype(F32)).astype(BF16)
            b_all.append(_dot(ltri, jnp.concatenate([g_hi, g_lo], axis=0)))

        pre = {}
        for cc, p in units:
            q, k, b = q_ref[rows[cc], kl[p]], k_ref[rows[cc], kl[p]], b_all[cc][:, kl[p]]
            b_last = b[C - 1:C, :]
            k1 = k * jnp.exp2(sub_bcast(b, SB - 1) - b)
            lhs1 = jnp.concatenate(
                [q[SB * (j + 1):, :] * jnp.exp2(b[SB * (j + 1):, :] - b[SB * j + SB - 1:SB * j + SB, :])
                 for j in range(NB - 1)], axis=0)
            lhs2 = jnp.concatenate(
                [q * jnp.exp2((b - sub_bcast(b, jj)) + jnp.tile(bias_ref[jj], (NB, 1))) for jj in range(SB)],
                axis=0)
            kd = k * jnp.exp2(b_last - b)
            pre[cc, p] = (lhs1.astype(BF16), heads(k1), lhs2.astype(BF16), heads(k),
                          (q * jnp.exp2(b)).astype(BF16), kd, b_last)

        outs = {u: (_dot_nt(pre[u][0], pre[u][1]), _dot_nt(pre[u][2], pre[u][3])) for u in units}

        lo = {}
        for u in units:
            out1, out2 = outs[u]
            a_rows = []
            for i in range(NB):
                d = out2[(SB - 1) * C + SB * i:(SB - 1) * C + SB * (i + 1), :]
                for jj in range(SB - 2, -1, -1):
                    d = jnp.where(colmod == jj, out2[jj * C + SB * i:jj * C + SB * (i + 1), :], d)
                acc = jnp.where(colblk == i, d, 0.0)
                for j in range(i):
                    r0 = off1[j] + SB * (i - j - 1)
                    acc = jnp.where(colblk == j, out1[r0:r0 + SB, :], acc)
                a_rows.append(acc)
            a = jnp.concatenate(a_rows, axis=0)
            lo[u] = jnp.concatenate([pre[u][4], a.astype(BF16)], axis=1)

        o_all = {}
        for cc in range(GLA_UNROLL):
            mts = [jnp.concatenate([pre[cc, p][5], jnp.broadcast_to(pre[cc, p][6], (C, LANES))], axis=0).T
                   for p in range(NP)]
            for p in range(NP):
                ha, hb = 2 * p, 2 * p + 1
                v_a, v_b = v_ref[rows[cc], vl[ha]], v_ref[rows[cc], vl[hb]]
                s_a, s_b = s_ref[ha], s_ref[hb]
                o_all[cc, ha] = _dot(lo[cc, p], jnp.concatenate([s_a.astype(BF16), zk, v_a, zk], axis=0))
                o_all[cc, hb] = _dot(lo[cc, p], jnp.concatenate([zk, s_b.astype(BF16), zk, v_b], axis=0))
                kdt = mts[p][:, :C].astype(BF16)
                dec = jnp.exp2(mts[p][:, C:C + 1])
                s_ref[ha] = dec[:GLA_DK] * s_a + _dot(kdt[:GLA_DK], v_a)
                s_ref[hb] = dec[GLA_DK:] * s_b + _dot(kdt[GLA_DK:], v_b)

        for cc in range(GLA_UNROLL):
            for h in range(GLA_HEADS):
                gg = gg_ref[rows[cc], vl[h]]
                o_ref[rows[cc], vl[h]] = (_rms(o_all[cc, h], gn) * (gg * _sigmoid(gg))).astype(BF16)
        return carry

    lax.fori_loop(0, GLA_TILE // (C * GLA_UNROLL), body, 0)

    @pl.when(ti == pl.num_programs(1) - 1)
    def _():
        sout_ref[0] = s_ref[...]


def _gla_prompt(q, k, g, v, gg, gn, batch, seq):
    nt = seq // GLA_TILE
    blk = lambda w_: pl.BlockSpec((GLA_TILE, w_), lambda b, t: (b * nt + t, 0))
    r = np.arange(GLA_SUB)
    bias = np.where(r[None, :, None] >= r[:, None, None], 0.0, NEG) * np.ones((1, 1, LANES))
    return pl.pallas_call(
        _gla_kernel,
        grid=(batch, nt),
        in_specs=[blk(GLA_QK), blk(GLA_QK), blk(GLA_QK), blk(GLA_WIDTH), blk(GLA_WIDTH),
                  pl.BlockSpec((1, GLA_DV), lambda b, t: (0, 0)),
                  pl.BlockSpec((GLA_SUB, GLA_SUB, LANES), lambda b, t: (0, 0, 0))],
        out_specs=[blk(GLA_WIDTH),
                   pl.BlockSpec((1, GLA_HEADS, GLA_DK, GLA_DV), lambda b, t: (b, 0, 0, 0))],
        out_shape=[jax.ShapeDtypeStruct((batch * seq, GLA_WIDTH), BF16),
                   jax.ShapeDtypeStruct((batch, GLA_HEADS, GLA_DK, GLA_DV), F32)],
        scratch_shapes=[pltpu.VMEM((GLA_HEADS, GLA_DK, GLA_DV), F32)],
        compiler_params=pltpu.CompilerParams(
            dimension_semantics=("arbitrary", "arbitrary"), vmem_limit_bytes=VMEM_LIMIT),
        name="gla_prompt",
    )(q, k, g, v, gg, gn, jnp.asarray(bias, F32))


def _swa_kernel(sink_ref, q_ref, kp_ref, kc_ref, vp_ref, vc_ref, bias_ref, o_ref,
                ke_scr, ve_scr, s_scr, p_scr, r_scr):
    W = WINDOW
    t = pl.program_id(1)
    nblk = SWA_TILE // W
    kk = jnp.concatenate([kp_ref[...], kc_ref[...]], axis=0)
    vv = jnp.concatenate([vp_ref[...], vc_ref[...]], axis=0)
    lo = lax.broadcasted_iota(jnp.int32, kk.shape, 1) < SWA_HEAD_DIM
    for x, scr in ((kk, ke_scr), (vv, ve_scr)):
        x_sw = pltpu.roll(x, SWA_HEAD_DIM, 1)
        scr[0, 0] = jnp.where(lo, x, 0.0).astype(BF16)
        scr[0, 1] = jnp.where(lo, 0.0, x_sw).astype(BF16)
        scr[1, 0] = jnp.where(lo, x_sw, 0.0).astype(BF16)
        scr[1, 1] = jnp.where(lo, 0.0, x).astype(BF16)

    lane_q = lax.broadcasted_iota(jnp.int32, (SWA_ROWS, LANES), 1) < SWA_HEAD_DIM
    npair = SWA_HEADS // 2

    def block(j, carry):
        qrows = pl.ds(pl.multiple_of(j * W, W), W)
        krows = pl.ds(pl.multiple_of(j * W, W), 2 * W)
        bias_blk = bias_ref.at[jnp.minimum(t * nblk + j, 1)]
        for p in range(npair):
            kv = p // (SWA_GROUP // 2)
            q = q_ref[qrows, p * LANES:(p + 1) * LANES]
            for hh in range(2):
                s_scr[p, :, hh * 2 * W:(hh + 1) * 2 * W] = _dot_nt(q, ke_scr[kv, hh, krows, :])
        for p in range(npair):
            for r0 in range(0, W, SWA_ROWS):
                rs = slice(r0, r0 + SWA_ROWS)
                rden = []
                for hh in range(2):
                    cs = slice(hh * 2 * W, (hh + 1) * 2 * W)
                    sink = sink_ref[2 * p + hh] * LOG2E
                    sh = s_scr[p, rs, cs] + bias_blk[rs, :]
                    m = jnp.maximum(jnp.max(sh, axis=-1, keepdims=True), sink)
                    e = jnp.exp2(sh - m)
                    den = jnp.sum(e, axis=-1, keepdims=True) + jnp.exp2(sink - m)
                    p_scr[p, rs, cs] = e.astype(BF16)
                    rden.append(1.0 / den)
                r_scr[p, rs, :] = jnp.where(lane_q, rden[0], rden[1])
        for p in range(npair):
            kv = p // (SWA_GROUP // 2)
            o = (_dot(p_scr[p, :, :2 * W], ve_scr[kv, 0, krows, :])
                 + _dot(p_scr[p, :, 2 * W:], ve_scr[kv, 1, krows, :])) * r_scr[p]
            o_ref[qrows, p * LANES:(p + 1) * LANES] = o.astype(BF16)
        return carry

    lax.fori_loop(0, nblk, block, 0)


def _swa_prompt(sinks, sq, sk, sv, batch, seq):
    W = WINDOW
    nt = seq // SWA_TILE
    nblk = SWA_TILE // W
    cur = lambda b, t: (b * nt + t, 0)
    prev = lambda b, t: (b * nt * nblk + jnp.maximum(t * nblk - 1, 0), 0)
    qi, kj = np.arange(W)[:, None], np.arange(2 * W)[None, :]
    band = (kj > qi) & (kj <= qi + W)
    bias = np.stack([np.where(band & (kj >= W), 0.0, NEG), np.where(band, 0.0, NEG)])
    ext_scr = pltpu.VMEM((SWA_KV_HEADS, 2, W + SWA_TILE, LANES), BF16)
    return pl.pallas_call(
        _swa_kernel,
        grid=(batch, nt),
        in_specs=[pl.BlockSpec(memory_space=pltpu.SMEM),
                  pl.BlockSpec((SWA_TILE, SWA_Q), cur),
                  pl.BlockSpec((W, SWA_KV), prev), pl.BlockSpec((SWA_TILE, SWA_KV), cur),
                  pl.BlockSpec((W, SWA_KV), prev), pl.BlockSpec((SWA_TILE, SWA_KV), cur),
                  pl.BlockSpec((2, W, 2 * W), lambda b, t: (0, 0, 0))],
        out_specs=pl.BlockSpec((SWA_TILE, SWA_Q), cur),
        out_shape=jax.ShapeDtypeStruct((batch * seq, SWA_Q), BF16),
        scratch_shapes=[ext_scr, ext_scr,
                        pltpu.VMEM((SWA_HEADS // 2, W, 4 * W), F32), pltpu.VMEM((SWA_HEADS // 2, W, 4 * W), BF16),
                        pltpu.VMEM((SWA_HEADS // 2, W, LANES), F32)],
        compiler_params=pltpu.CompilerParams(dimension_semantics=("arbitrary", "arbitrary"),
                                             vmem_limit_bytes=VMEM_LIMIT),
        name="swa_prompt",
    )(sinks, sq, sk, sk, sv, sv, jnp.asarray(bias, F32))


def _out_ffn_kernel(x_ref, og_ref, os_ref, wo_ref, nw_ref, wg_ref, wu_ref, wd_ref, nf_ref, y_ref, *, final):
    x = (x_ref[...] + _dot(og_ref[...], wo_ref[:GLA_WIDTH, :]) + _dot(os_ref[...], wo_ref[GLA_WIDTH:, :]))
    h = _rms(x, nw_ref[...]).astype(BF16)
    acc = x
    for c0 in range(0, D_FF, FF_CHUNK):
        a = _dot(h, wg_ref[:, c0:c0 + FF_CHUNK])
        u = _dot(h, wu_ref[:, c0:c0 + FF_CHUNK])
        acc = acc + _dot((a * _sigmoid(a) * u).astype(BF16), wd_ref[c0:c0 + FF_CHUNK, :])
    if final:
        acc = _rms(acc, nf_ref[...])
    y_ref[...] = acc


def _out_ffn(x, og, osw, wo, nw, wg, wu, wd, nf, layer, final):
    n = x.shape[0]
    row = lambda w_: pl.BlockSpec((TOK_TILE, w_), lambda i: (i, 0))
    const = lambda shp: pl.BlockSpec((None,) + shp, lambda i: (layer, 0, 0), pipeline_mode=pl.Buffered(1))
    return pl.pallas_call(
        functools.partial(_out_ffn_kernel, final=final),
        grid=(n // TOK_TILE,),
        in_specs=[row(D_MODEL), row(GLA_WIDTH), row(SWA_Q), const((D_MODEL, D_MODEL)), const((1, D_MODEL)),
                  const((D_MODEL, D_FF)), const((D_MODEL, D_FF)), const((D_FF, D_MODEL)),
                  pl.BlockSpec((1, D_MODEL), lambda i: (0, 0))],
        out_specs=row(D_MODEL),
        out_shape=jax.ShapeDtypeStruct((n, D_MODEL), F32),
        compiler_params=pltpu.CompilerParams(dimension_semantics=("arbitrary",),
                                             vmem_limit_bytes=VMEM_LIMIT),
        name="out_ffn",
    )(x, og, osw, wo, nw, wg, wu, wd, nf)


def _sample_kernel(q_ref, k_ref, g_ref, v_ref, gg_ref, s0_ref, gn_ref,
                   qx_ref, kn_ref, vn_ref, ck_ref, cv_ref, sink_ref,
                   s_prev_ref, ck_prev_ref, cv_prev_ref,
                   og_ref, os_ref, sn_ref, cko_ref, cvo_ref, *, nt):
    del s_prev_ref, ck_prev_ref, cv_prev_ref
    TP = SUBLANES
    q, k, g, v = q_ref[...], k_ref[...], g_ref[...], v_ref[...]
    s0 = s0_ref[...]
    tt = lax.broadcasted_iota(jnp.int32, (1, TP, 1), 1)
    b = jnp.zeros_like(g)
    for s in range(nt):
        b = b + jnp.where(tt >= s, g[:, s:s + 1, :], 0.0)
    o = jnp.einsum('gtk,gkv->gtv', (q * jnp.exp(b)).astype(BF16), s0.astype(BF16),
                   preferred_element_type=F32)
    for s in range(nt):
        dec = jnp.exp(jnp.where(tt >= s, b - b[:, s:s + 1, :], NEG))
        a = jnp.sum(q * k[:, s:s + 1, :] * dec, axis=-1, keepdims=True)
        o = o + a * v[:, s:s + 1, :]
    gg = gg_ref[...]
    og_ref[...] = _rms(o, gn_ref[...]) * (gg * _sigmoid(gg))

    b_last = b[:, nt - 1:nt, :]
    d = jnp.exp(b_last)
    d_hi = d.astype(BF16).astype(F32)
    x = jnp.where(tt < nt, k * jnp.exp(b_last - b), jnp.where(tt == nt, d_hi, jnp.where(tt == nt + 1, d - d_hi, 0.0)))
    ones = jnp.where((tt == nt) | (tt == nt + 1), 1.0, 0.0)
    y = jnp.concatenate([jnp.where(tt < nt, v, 0.0), jnp.broadcast_to(ones, v.shape)], axis=2)
    r = lax.dot_general(x.astype(BF16), y.astype(BF16), (((1,), (1,)), ((0,), (0,))),
                        preferred_element_type=F32)
    sn_ref[...] = r[:, :, GLA_DV:] * s0 + r[:, :, :GLA_DV]

    W = WINDOW
    qx = qx_ref[...]
    ck, cv = ck_ref[...], cv_ref[...]
    kn, vn = kn_ref[...], vn_ref[...]
    sink = sink_ref[...] * LOG2E
    nrow = SWA_KV_HEADS * nt * SWA_GROUP
    r = lax.broadcasted_iota(jnp.int32, (1, nrow, W), 1)
    tq = (r % (nt * SWA_GROUP)) // SWA_GROUP
    j = lax.broadcasted_iota(jnp.int32, (1, nrow, W), 2)
    sc = jnp.einsum('bqd,bds->bqs', qx, ck.astype(BF16), preferred_element_type=F32)
    sc = jnp.where(j > tq, sc, NEG)
    tq1 = tq[:, :, 0:1]
    qf = qx.astype(F32)
    sn = [jnp.where(tq1 >= s, jnp.sum(qf * kn[:, s:s + 1, :], axis=-1, keepdims=True), NEG)
          for s in range(nt)]
    m = jnp.maximum(jnp.max(sc, axis=-1, keepdims=True), sink)
    for s in range(nt):
        m = jnp.maximum(m, sn[s])
    pc = jnp.exp2(sc - m)
    pn = [jnp.exp2(sn[s] - m) for s in range(nt)]
    den = jnp.sum(pc, axis=-1, keepdims=True) + jnp.exp2(sink - m)
    for s in range(nt):
        den = den + pn[s]
    o = jnp.einsum('bqs,bds->bqd', pc.astype(BF16), cv.astype(BF16), preferred_element_type=F32)
    for s in range(nt):
        o = o + pn[s] * vn[:, s:s + 1, :]
    os_ref[...] = o * (1.0 / den)

    wl = lax.broadcasted_iota(jnp.int32, (1, 1, W), 2)
    pr = lax.broadcasted_iota(jnp.int32, (1, 2 * nt, W), 1)
    pw = lax.broadcasted_iota(jnp.int32, (1, 2 * nt, W), 2)
    place = jnp.broadcast_to(jnp.where(pw == W - nt + pr % nt, 1.0, 0.0), (ck.shape[0], 2 * nt, W)).astype(BF16)
    for src, new, dst in ((ck, kn, cko_ref), (cv, vn, cvo_ref)):
        hi = new.astype(BF16).astype(F32)
        x = jnp.concatenate([hi, new - hi], axis=1).astype(BF16)
        placed = lax.dot_general(x, place, (((1,), (1,)), ((0,), (0,))), preferred_element_type=F32)
        dst[...] = jnp.where(wl >= W - nt, placed, pltpu.roll(src, W - nt, 2))


def _sample_mix(q4, k4, g4, v4, gg4, state, gn, qx, kn, vn, ck, cv, sink_rows, carried, layer, nbatch, nt):
    bt = SAMPLE_BT
    G = bt * GLA_HEADS
    TP = SUBLANES
    nrow = SWA_KV_HEADS * nt * SWA_GROUP
    b3 = lambda a, c: pl.BlockSpec((G, a, c), lambda i: (i, 0, 0))
    s3 = lambda a, c: pl.BlockSpec((bt, a, c), lambda i: (i, 0, 0))
    lay_g = pl.BlockSpec((None, G, GLA_DK, GLA_DV), lambda i: (layer, i, 0, 0))
    lay_c = pl.BlockSpec((None, bt, SWA_KV, WINDOW), lambda i: (layer, i, 0, 0))
    ng = nbatch * GLA_HEADS
    n_in = 13
    return pl.pallas_call(
        functools.partial(_sample_kernel, nt=nt),
        grid=(nbatch // bt,),
        in_specs=[b3(TP, GLA_DK), b3(TP, GLA_DK), b3(TP, GLA_DK),
                  b3(TP, GLA_DV), b3(TP, GLA_DV), lay_g,
                  pl.BlockSpec((1, 1, GLA_DV), lambda i: (0, 0, 0)),
                  s3(nrow, SWA_KV), s3(nt, SWA_KV), s3(nt, SWA_KV), lay_c, lay_c,
                  pl.BlockSpec((1, nrow, 1), lambda i: (0, 0, 0))]
                 + [pl.BlockSpec(memory_space=pl.ANY)] * 3,
        out_specs=[b3(TP, GLA_DV), s3(nrow, SWA_KV), lay_g, lay_c, lay_c],
        out_shape=[jax.ShapeDtypeStruct((ng, TP, GLA_DV), F32),
                   jax.ShapeDtypeStruct((nbatch, nrow, SWA_KV), F32),
                   jax.ShapeDtypeStruct((DEPTH, ng, GLA_DK, GLA_DV), F32),
                   jax.ShapeDtypeStruct((DEPTH, nbatch, SWA_KV, WINDOW), F32),
                   jax.ShapeDtypeStruct((DEPTH, nbatch, SWA_KV, WINDOW), F32)],
        input_output_aliases={n_in + j: 2 + j for j in range(3)},
        compiler_params=pltpu.CompilerParams(dimension_semantics=("arbitrary",),
                                             vmem_limit_bytes=VMEM_LIMIT),
        name="sample_mix",
    )(q4, k4, g4, v4, gg4, state, gn, qx, kn, vn, ck, cv, sink_rows, *carried)


def _rope_tables(pos):
    half = SWA_HEAD_DIM // 2
    inv = ROPE_THETA ** (-jnp.arange(half, dtype=F32) / half)
    ang = pos.astype(F32)[:, None] * inv[None, :]
    cos, sin = jnp.cos(ang), jnp.sin(ang)
    zero = jnp.zeros_like(sin)
    rep = LANES // SWA_HEAD_DIM
    return (jnp.tile(jnp.concatenate([cos, cos], axis=1), (1, rep)),
            jnp.tile(jnp.concatenate([-sin, zero], axis=1), (1, rep)),
            jnp.tile(jnp.concatenate([zero, sin], axis=1), (1, rep)))


def kernel(x_prompt, x_sample, state_gla, cache_swa_k, cache_swa_v, norm_attn, w_in, w_gk2, b_gk2,
           gla_norm, attn_sinks, w_o, norm_ffn, w_gate, w_up, w_down, norm_final):
    batch, seq, _ = x_prompt.shape
    nbatch, nt, _ = x_sample.shape
    n_s = nbatch * nt

    splits = np.cumsum([GLA_QK, GLA_QK, GLA_WIDTH, GATE_RANK, GLA_WIDTH, SWA_Q, SWA_KV, SWA_KV])
    c_ga0, c_ga1 = int(splits[2]), int(splits[3])
    w_in_p = jnp.concatenate(
        [w_in[:, :, :c_ga0], w_in[:, :, c_ga1:], w_in[:, :, c_ga0:c_ga1],
         jnp.zeros((DEPTH, D_MODEL, LANES - GATE_RANK), w_in.dtype)], axis=2).astype(BF16)
    wg2_p = jnp.concatenate([w_gk2, jnp.zeros((DEPTH, LANES - GATE_RANK, GLA_QK), w_gk2.dtype)],
                            axis=1).astype(BF16)
    bg2 = b_gk2.reshape(DEPTH, 1, GLA_QK)
    nw_a = norm_attn.reshape(DEPTH, 1, D_MODEL)
    nw_f = norm_ffn.reshape(DEPTH, 1, D_MODEL)
    wo_b, wg_b, wu_b, wd_b = (w.astype(BF16) for w in (w_o, w_gate, w_up, w_down))
    nf = norm_final.reshape(1, D_MODEL)

    tabs_p = _rope_tables(jnp.arange(seq))
    tabs_s = _rope_tables(jnp.tile(PAST_LEN + jnp.arange(nt), TOK_TILE // nt))

    state_r = state_gla.reshape(DEPTH, nbatch * GLA_HEADS, GLA_DK, GLA_DV)
    pos_minor = lambda c: c.transpose(0, 1, 3, 4, 2).reshape(DEPTH, nbatch, SWA_KV, WINDOW)
    ck_r, cv_r = pos_minor(cache_swa_k), pos_minor(cache_swa_v)

    xp = x_prompt.reshape(batch * seq, D_MODEL)
    xs = x_sample.reshape(n_s, D_MODEL)
    sp_l, kp_l, vp_l = [], [], []
    carried = [jnp.zeros(state_r.shape, F32), jnp.zeros(ck_r.shape, F32), jnp.zeros(cv_r.shape, F32)]
    tpad = ((0, 0), (0, SUBLANES - nt), (0, 0))
    for l in range(DEPTH):
        final = l == DEPTH - 1
        q, k, g, v, gg, sq, sk, sv = _in_proj(xp, nw_a, w_in_p, wg2_p, bg2, tabs_p, l)
        og, s_fin = _gla_prompt(q, k, g, v, gg, gla_norm[l].reshape(1, GLA_DV), batch, seq)
        osw = _swa_prompt(attn_sinks[l], sq, sk, sv, batch, seq)
        xp = _out_ffn(xp, og, osw, wo_b, nw_f, wg_b, wu_b, wd_b, nf, l, final)
        sp_l.append(s_fin)
        last = lambda a: a.reshape(batch, seq, SWA_KV)[:, seq - WINDOW:].reshape(
            batch, WINDOW, SWA_KV_HEADS, SWA_HEAD_DIM)
        kp_l.append(last(sk))
        vp_l.append(last(sv))

        q, k, g, v, gg, sq, sk, sv = _in_proj(xs, nw_a, w_in_p, wg2_p, bg2, tabs_s, l)

        def bhtk(a, w_):
            return a.reshape(nbatch, nt, GLA_HEADS, w_).transpose(0, 2, 1, 3).reshape(nbatch * GLA_HEADS, nt, w_)

        q4, k4, g4 = (jnp.pad(bhtk(a, GLA_DK), tpad) for a in (q, k, g))
        v4 = jnp.pad(bhtk(v.astype(F32), GLA_DV), tpad)
        gg4 = jnp.pad(bhtk(gg, GLA_DV), tpad)
        qk = sq.reshape(nbatch, nt, SWA_KV_HEADS, SWA_GROUP, SWA_HEAD_DIM).transpose(0, 2, 1, 3, 4)
        zq = jnp.zeros_like(qk[:, 0])
        qx = jnp.stack([jnp.concatenate([qk[:, 0], zq], axis=-1), jnp.concatenate([zq, qk[:, 1]], axis=-1)],
                       axis=1).reshape(nbatch, SWA_KV_HEADS * nt * SWA_GROUP, SWA_KV)
        sink_rows = jnp.broadcast_to(attn_sinks[l].reshape(SWA_KV_HEADS, 1, SWA_GROUP),
                                     (SWA_KV_HEADS, nt, SWA_GROUP)).reshape(1, -1, 1)
        og4, os4, *carried = _sample_mix(
            q4, k4, g4, v4, gg4, state_r, gla_norm[l].reshape(1, 1, GLA_DV),
            qx, sk.reshape(nbatch, nt, SWA_KV), sv.reshape(nbatch, nt, SWA_KV),
            ck_r, cv_r, sink_rows, carried, l, nbatch, nt)
        og = og4[:, :nt].reshape(nbatch, GLA_HEADS, nt, GLA_DV).transpose(0, 2, 1, 3).reshape(n_s, GLA_WIDTH)
        os5 = os4.reshape(nbatch, SWA_KV_HEADS, nt, SWA_GROUP, SWA_KV)
        osw = jnp.stack([os5[:, 0, :, :, :SWA_HEAD_DIM], os5[:, 1, :, :, SWA_HEAD_DIM:]], axis=2)
        osw = osw.reshape(n_s, SWA_Q)
        xs = _out_ffn(xs, og.astype(BF16), osw.astype(BF16), wo_b, nw_f, wg_b, wu_b, wd_b, nf, l, final)

    s_all, ck_all, cv_all = carried
    pos_major = lambda c: c.reshape(DEPTH, nbatch, SWA_KV_HEADS, SWA_HEAD_DIM, WINDOW).transpose(0, 1, 4, 2, 3)
    return (xp.reshape(batch, seq, D_MODEL), xs.reshape(nbatch, nt, D_MODEL),
            jnp.stack(sp_l), jnp.stack(kp_l), jnp.stack(vp_l),
            s_all.reshape(DEPTH, nbatch, GLA_HEADS, GLA_DK, GLA_DV), pos_major(ck_all), pos_major(cv_all))
```

```python
import functools

import jax
import jax.numpy as jnp
import numpy as np
from jax import lax
from jax.experimental import pallas as pl
from jax.experimental.pallas import tpu as pltpu

F32 = jnp.float32
BF16 = jnp.bfloat16

D_MODEL = 1024
DEPTH = 4
PAST_LEN = 8192
GLA_HEADS = 4
GLA_DV = 128
GLA_DK = 64
GLA_QK = GLA_HEADS * GLA_DK
GLA_WIDTH = GLA_HEADS * GLA_DV
GATE_RANK = 16
GATE_NORMALIZER = 16.0
SWA_HEAD_DIM = 64
SWA_HEADS = 8
SWA_KV_HEADS = 2
SWA_GROUP = 4
SWA_Q = SWA_HEADS * SWA_HEAD_DIM
SWA_KV = SWA_KV_HEADS * SWA_HEAD_DIM
WINDOW = 128
ROPE_THETA = 10000.0
D_FF = 2816
NORM_EPS = 1e-6

LANES = 128
SUBLANES = 8
VMEM_LIMIT = 56 * 1024 * 1024

C_Q, C_K, C_V, C_GG, C_SQ, C_SK, C_SV, C_GA = 0, 256, 512, 1024, 1536, 2048, 2176, 2304
IN_COLS_P = C_GA + LANES

TOK_TILE = 512
GLA_CHUNK = 64
GLA_SUB = 8
GLA_TILE = 512
GLA_UNROLL = 8
FF_CHUNK = D_FF
SWA_ROWS = 32
SWA_TILE = 512
SAMPLE_BT = 8
NEG = -1e30
LOG2E = 1.4426950408889634


def _rms(x, w):
    ms = jnp.mean(x * x, axis=-1, keepdims=True)
    return x * lax.rsqrt(ms + NORM_EPS) * w


def _sigmoid(x):
    return 1.0 / (1.0 + jnp.exp(-x))


def _dot(a, b):
    return jnp.dot(a, b, preferred_element_type=F32)


def _dot_nt(a, b):
    return lax.dot_general(a, b, (((1,), (1,)), ((), ())), preferred_element_type=F32)


def _in_proj_kernel(x_ref, nw_ref, w_ref, wg2_ref, bg2_ref, cos_ref, sa_ref, sb_ref,
                    q_ref, k_ref, ghi_ref, glo_ref, v_ref, gate_ref, sq_ref, sk_ref, sv_ref):
    h = _rms(x_ref[...], nw_ref[...]).astype(BF16)
    z = _dot(h, w_ref[...])

    def proj(c0, c1):
        return z[:, c0:c1]

    q_ref[...] = proj(C_Q, C_K) * (GLA_DK ** -0.5)
    k_ref[...] = proj(C_K, C_V)
    v_ref[...] = proj(C_V, C_GG).astype(BF16)
    gg = proj(C_GG, C_SQ)
    gate_ref[...] = gg * _sigmoid(gg)
    ga = proj(C_GA, IN_COLS_P).astype(BF16)
    gx = _dot(ga, wg2_ref[...]) + bg2_ref[...]
    g2 = (jnp.minimum(gx, 0.0) - jnp.log1p(jnp.exp(-jnp.abs(gx)))) * (LOG2E / GATE_NORMALIZER)
    g_hi = g2.astype(BF16)
    ghi_ref[...] = g_hi
    glo_ref[...] = (g2 - g_hi.astype(F32)).astype(BF16)

    cos, sa, sb = cos_ref[...], sa_ref[...], sb_ref[...]

    def rope(z):
        return z * cos + pltpu.roll(z, LANES - 32, 1) * sa + pltpu.roll(z, 32, 1) * sb

    for j in range(SWA_Q // (2 * LANES)):
        zq = proj(C_SQ + 2 * j * LANES, C_SQ + 2 * (j + 1) * LANES)
        for jj in range(2):
            sq_ref[:, (2 * j + jj) * LANES:(2 * j + jj + 1) * LANES] = (
                rope(zq[:, jj * LANES:(jj + 1) * LANES]) * (SWA_HEAD_DIM ** -0.5 * LOG2E)).astype(BF16)
    zkv = proj(C_SK, C_GA)
    sk_ref[...] = rope(zkv[:, :SWA_KV])
    sv_ref[...] = zkv[:, SWA_KV:]


def _in_proj(x, nw, w, wg2, bg2, tabs, layer):
    n = x.shape[0]
    nt = n // TOK_TILE
    cos, sa, sb = tabs
    ntab = cos.shape[0] // TOK_TILE
    row = lambda w_: pl.BlockSpec((TOK_TILE, w_), lambda i: (i, 0))
    tab = pl.BlockSpec((TOK_TILE, LANES), lambda i: (i % ntab, 0))
    const = lambda shp: pl.BlockSpec((None,) + shp, lambda i: (layer, 0, 0), pipeline_mode=pl.Buffered(1))
    outs = [(GLA_QK, F32), (GLA_QK, F32), (GLA_QK, BF16), (GLA_QK, BF16), (GLA_WIDTH, BF16), (GLA_WIDTH, F32),
            (SWA_Q, BF16), (SWA_KV, F32), (SWA_KV, F32)]
    return pl.pallas_call(
        _in_proj_kernel,
        grid=(nt,),
        in_specs=[row(D_MODEL), const((1, D_MODEL)), const((D_MODEL, IN_COLS_P)),
                  const((LANES, GLA_QK)), const((1, GLA_QK)), tab, tab, tab],
        out_specs=[row(w_) for w_, _ in outs],
        out_shape=[jax.ShapeDtypeStruct((n, w_), dt) for w_, dt in outs],
        compiler_params=pltpu.CompilerParams(dimension_semantics=("arbitrary",),
                                             vmem_limit_bytes=VMEM_LIMIT),
        name="in_proj",
    )(x, nw, w, wg2, bg2, cos, sa, sb)


def _gla_kernel(q_ref, k_ref, ghi_ref, glo_ref, v_ref, bias_ref, o_ref, sout_ref, s_ref):
    ti = pl.program_id(1)
    C, SB = GLA_CHUNK, GLA_SUB
    NB = C // SB
    NP = GLA_HEADS // 2

    @pl.when(ti == 0)
    def _():
        s_ref[...] = jnp.zeros_like(s_ref)

    ltri = (lax.broadcasted_iota(jnp.int32, (C, 2 * C), 0) >= lax.broadcasted_iota(jnp.int32, (C, 2 * C), 1) % C
            ).astype(F32).astype(BF16)
    lane = lax.broadcasted_iota(jnp.int32, (1, LANES), 1)
    mask_a = (lane < GLA_DK).astype(F32)
    mask_b = 1.0 - mask_a
    col8 = lax.broadcasted_iota(jnp.int32, (SB, LANES), 1) % C
    colblk, colmod = col8 // SB, col8 % SB
    zk = jnp.zeros((GLA_DK, GLA_DV), BF16)
    off1 = [sum(C - SB * (jj + 1) for jj in range(j)) for j in range(NB - 1)]

    def sub_bcast(x, j):
        x3 = x.reshape(NB, SB, LANES)
        return jnp.broadcast_to(x3[:, j:j + 1, :], (NB, SB, LANES)).reshape(C, LANES)

    def heads(x):
        return jnp.concatenate([x * mask_a, x * mask_b], axis=0).astype(BF16)

    def body(it, carry):
        rows = [pl.ds(pl.multiple_of((it * GLA_UNROLL + cc) * C, C), C) for cc in range(GLA_UNROLL)]
        units = [(cc, p) for cc in range(GLA_UNROLL) for p in range(NP)]
        kl = [slice(p * LANES, (p + 1) * LANES) for p in range(NP)]
        vl = [slice(h * GLA_DV, (h + 1) * GLA_DV) for h in range(GLA_HEADS)]

        b_all = [_dot(ltri, jnp.concatenate([ghi_ref[rows[cc], :], glo_ref[rows[cc], :]], axis=0))
                 for cc in range(GLA_UNROLL)]

        qe, outs, dec, upd = {}, {}, {}, {}
        for cc, p in units:
            q, k, b = q_ref[rows[cc], kl[p]], k_ref[rows[cc], kl[p]], b_all[cc][:, kl[p]]
            b_last = b[C - 1:C, :]
            k1 = k * jnp.exp2(sub_bcast(b, SB - 1) - b)
            lhs1 = jnp.concatenate(
                [q[SB * (j + 1):, :] * jnp.exp2(b[SB * (j + 1):, :] - b[SB * j + SB - 1:SB * j + SB, :])
                 for j in range(NB - 1)], axis=0)
            lhs2 = jnp.concatenate(
                [q * jnp.exp2((b - sub_bcast(b, jj)) + jnp.tile(bias_ref[jj], (NB, 1))) for jj in range(SB)],
                axis=0)
            qe[cc, p] = (q * jnp.exp2(b)).astype(BF16)
            outs[cc, p] = (_dot_nt(lhs1.astype(BF16), heads(k1)), _dot_nt(lhs2.astype(BF16), heads(k)))
            kd = k * jnp.exp2(b_last - b)
            mt = jnp.concatenate([kd, jnp.broadcast_to(b_last, (C, LANES))], axis=0).T
            kdt = mt[:, :C].astype(BF16)
            dec[cc, p] = jnp.exp2(mt[:, C:C + 1])
            upd[cc, p] = (_dot(kdt[:GLA_DK], v_ref[rows[cc], vl[2 * p]]),
                          _dot(kdt[GLA_DK:], v_ref[rows[cc], vl[2 * p + 1]]))

        lo = {}
        for u in units:
            out1, out2 = outs[u]
            a_rows = []
            for i in range(NB):
                d = out2[(SB - 1) * C + SB * i:(SB - 1) * C + SB * (i + 1), :]
                for jj in range(SB - 2, -1, -1):
                    d = jnp.where(colmod == jj, out2[jj * C + SB * i:jj * C + SB * (i + 1), :], d)
                acc = jnp.where(colblk == i, d, 0.0)
                for j in range(i):
                    r0 = off1[j] + SB * (i - j - 1)
                    acc = jnp.where(colblk == j, out1[r0:r0 + SB, :], acc)
                a_rows.append(acc)
            a = jnp.concatenate(a_rows, axis=0)
            lo[u] = jnp.concatenate([qe[u], a.astype(BF16)], axis=1)

        state = [s_ref[h] for h in range(GLA_HEADS)]
        o_all = {}
        for cc in range(GLA_UNROLL):
            for p in range(NP):
                ha, hb = 2 * p, 2 * p + 1
                v_a, v_b = v_ref[rows[cc], vl[ha]], v_ref[rows[cc], vl[hb]]
                o_all[cc, ha] = _dot(lo[cc, p], jnp.concatenate([state[ha].astype(BF16), zk, v_a, zk], axis=0))
                o_all[cc, hb] = _dot(lo[cc, p], jnp.concatenate([zk, state[hb].astype(BF16), zk, v_b], axis=0))
                state[ha] = dec[cc, p][:GLA_DK] * state[ha] + upd[cc, p][0]
                state[hb] = dec[cc, p][GLA_DK:] * state[hb] + upd[cc, p][1]
        for h in range(GLA_HEADS):
            s_ref[h] = state[h]

        for cc in range(GLA_UNROLL):
            for h in range(GLA_HEADS):
                o_ref[rows[cc], vl[h]] = o_all[cc, h].astype(BF16)
        return carry

    lax.fori_loop(0, GLA_TILE // (C * GLA_UNROLL), body, 0)

    @pl.when(ti == pl.num_programs(1) - 1)
    def _():
        sout_ref[0] = s_ref[...]


def _gla_prompt(q, k, g_hi, g_lo, v, batch, seq):
    nt = seq // GLA_TILE
    blk = lambda w_: pl.BlockSpec((GLA_TILE, w_), lambda b, t: (b * nt + t, 0))
    r = np.arange(GLA_SUB)
    bias = np.where(r[None, :, None] >= r[:, None, None], 0.0, NEG) * np.ones((1, 1, LANES))
    return pl.pallas_call(
        _gla_kernel,
        grid=(batch, nt),
        in_specs=[blk(GLA_QK), blk(GLA_QK), blk(GLA_QK), blk(GLA_QK), blk(GLA_WIDTH),
                  pl.BlockSpec((GLA_SUB, GLA_SUB, LANES), lambda b, t: (0, 0, 0))],
        out_specs=[blk(GLA_WIDTH),
                   pl.BlockSpec((1, GLA_HEADS, GLA_DK, GLA_DV), lambda b, t: (b, 0, 0, 0))],
        out_shape=[jax.ShapeDtypeStruct((batch * seq, GLA_WIDTH), BF16),
                   jax.ShapeDtypeStruct((batch, GLA_HEADS, GLA_DK, GLA_DV), F32)],
        scratch_shapes=[pltpu.VMEM((GLA_HEADS, GLA_DK, GLA_DV), F32)],
        compiler_params=pltpu.CompilerParams(
            dimension_semantics=("arbitrary", "arbitrary"), vmem_limit_bytes=VMEM_LIMIT),
        name="gla_prompt",
    )(q, k, g_hi, g_lo, v, jnp.asarray(bias, F32))


def _swa_kernel(sink_ref, q_ref, kp_ref, kc_ref, vp_ref, vc_ref, bias_ref, o_ref,
                ke_scr, ve_scr, s_scr, p_scr, r_scr):
    W = WINDOW
    t = pl.program_id(1)
    nblk = SWA_TILE // W
    kk = jnp.concatenate([kp_ref[...], kc_ref[...]], axis=0)
    vv = jnp.concatenate([vp_ref[...], vc_ref[...]], axis=0)
    lo = lax.broadcasted_iota(jnp.int32, kk.shape, 1) < SWA_HEAD_DIM
    for x, scr in ((kk, ke_scr), (vv, ve_scr)):
        x_sw = pltpu.roll(x, SWA_HEAD_DIM, 1)
        scr[0, 0] = jnp.where(lo, x, 0.0).astype(BF16)
        scr[0, 1] = jnp.where(lo, 0.0, x_sw).astype(BF16)
        scr[1, 0] = jnp.where(lo, x_sw, 0.0).astype(BF16)
        scr[1, 1] = jnp.where(lo, 0.0, x).astype(BF16)

    lane_q = lax.broadcasted_iota(jnp.int32, (SWA_ROWS, LANES), 1) < SWA_HEAD_DIM
    npair = SWA_HEADS // 2

    def block(j, carry):
        qrows = pl.ds(pl.multiple_of(j * W, W), W)
        krows = pl.ds(pl.multiple_of(j * W, W), 2 * W)
        bias_blk = bias_ref.at[jnp.minimum(t * nblk + j, 1)]
        for p in range(npair):
            kv = p // (SWA_GROUP // 2)
            q = q_ref[qrows, p * LANES:(p + 1) * LANES]
            for hh in range(2):
                s_scr[p, :, hh * 2 * W:(hh + 1) * 2 * W] = _dot_nt(q, ke_scr[kv, hh, krows, :])
        for p in range(npair):
            for r0 in range(0, W, SWA_ROWS):
                rs = slice(r0, r0 + SWA_ROWS)
                rden = []
                for hh in range(2):
                    cs = slice(hh * 2 * W, (hh + 1) * 2 * W)
                    sink = sink_ref[2 * p + hh] * LOG2E
                    sh = s_scr[p, rs, cs] + bias_blk[rs, :]
                    m = jnp.maximum(jnp.max(sh, axis=-1, keepdims=True), sink)
                    e = jnp.exp2(sh - m)
                    den = jnp.sum(e, axis=-1, keepdims=True) + jnp.exp2(sink - m)
                    p_scr[p, rs, cs] = e.astype(BF16)
                    rden.append(1.0 / den)
                r_scr[p, rs, :] = jnp.where(lane_q, rden[0], rden[1])
        for p in range(npair):
            kv = p // (SWA_GROUP // 2)
            o = (_dot(p_scr[p, :, :2 * W], ve_scr[kv, 0, krows, :])
                 + _dot(p_scr[p, :, 2 * W:], ve_scr[kv, 1, krows, :])) * r_scr[p]
            o_ref[qrows, p * LANES:(p + 1) * LANES] = o.astype(BF16)
        return carry

    lax.fori_loop(0, nblk, block, 0)


def _swa_prompt(sinks, sq, sk, sv, batch, seq):
    W = WINDOW
    nt = seq // SWA_TILE
    nblk = SWA_TILE // W
    cur = lambda b, t: (b * nt + t, 0)
    prev = lambda b, t: (b * nt * nblk + jnp.maximum(t * nblk - 1, 0), 0)
    qi, kj = np.arange(W)[:, None], np.arange(2 * W)[None, :]
    band = (kj > qi) & (kj <= qi + W)
    bias = np.stack([np.where(band & (kj >= W), 0.0, NEG), np.where(band, 0.0, NEG)])
    ext_scr = pltpu.VMEM((SWA_KV_HEADS, 2, W + SWA_TILE, LANES), BF16)
    return pl.pallas_call(
        _swa_kernel,
        grid=(batch, nt),
        in_specs=[pl.BlockSpec(memory_space=pltpu.SMEM),
                  pl.BlockSpec((SWA_TILE, SWA_Q), cur),
                  pl.BlockSpec((W, SWA_KV), prev), pl.BlockSpec((SWA_TILE, SWA_KV), cur),
                  pl.BlockSpec((W, SWA_KV), prev), pl.BlockSpec((SWA_TILE, SWA_KV), cur),
                  pl.BlockSpec((2, W, 2 * W), lambda b, t: (0, 0, 0))],
        out_specs=pl.BlockSpec((SWA_TILE, SWA_Q), cur),
        out_shape=jax.ShapeDtypeStruct((batch * seq, SWA_Q), BF16),
        scratch_shapes=[ext_scr, ext_scr,
                        pltpu.VMEM((SWA_HEADS // 2, W, 4 * W), F32), pltpu.VMEM((SWA_HEADS // 2, W, 4 * W), BF16),
                        pltpu.VMEM((SWA_HEADS // 2, W, LANES), F32)],
        compiler_params=pltpu.CompilerParams(dimension_semantics=("arbitrary", "arbitrary"),
                                             vmem_limit_bytes=VMEM_LIMIT),
        name="swa_prompt",
    )(sinks, sq, sk, sk, sv, sv, jnp.asarray(bias, F32))


def _out_ffn_kernel(x_ref, og_ref, gate_ref, gn_ref, os_ref, wo_ref, nw_ref, wg_ref, wu_ref, wd_ref, nf_ref,
                    y_ref, *, final):
    gn = gn_ref[...]
    og = jnp.concatenate([_rms(og_ref[:, h * GLA_DV:(h + 1) * GLA_DV].astype(F32), gn)
                          for h in range(GLA_HEADS)], axis=1) * gate_ref[...]
    x = (x_ref[...] + _dot(og.astype(BF16), wo_ref[:GLA_WIDTH, :]) + _dot(os_ref[...], wo_ref[GLA_WIDTH:, :]))
    h = _rms(x, nw_ref[...]).astype(BF16)
    acc = x
    for c0 in range(0, D_FF, FF_CHUNK):
        a = _dot(h, wg_ref[:, c0:c0 + FF_CHUNK])
        u = _dot(h, wu_ref[:, c0:c0 + FF_CHUNK])
        acc = acc + _dot((a * _sigmoid(a) * u).astype(BF16), wd_ref[c0:c0 + FF_CHUNK, :])
    if final:
        acc = _rms(acc, nf_ref[...])
    y_ref[...] = acc


def _out_ffn(x, og, gate, gn, osw, wo, nw, wg, wu, wd, nf, layer, final):
    n = x.shape[0]
    row = lambda w_: pl.BlockSpec((TOK_TILE, w_), lambda i: (i, 0))
    const = lambda shp: pl.BlockSpec((None,) + shp, lambda i: (layer, 0, 0), pipeline_mode=pl.Buffered(1))
    return pl.pallas_call(
        functools.partial(_out_ffn_kernel, final=final),
        grid=(n // TOK_TILE,),
        in_specs=[row(D_MODEL), row(GLA_WIDTH), row(GLA_WIDTH), const((1, GLA_DV)), row(SWA_Q),
                  const((D_MODEL, D_MODEL)), const((1, D_MODEL)),
                  const((D_MODEL, D_FF)), const((D_MODEL, D_FF)), const((D_FF, D_MODEL)),
                  pl.BlockSpec((1, D_MODEL), lambda i: (0, 0))],
        out_specs=row(D_MODEL),
        out_shape=jax.ShapeDtypeStruct((n, D_MODEL), F32),
        compiler_params=pltpu.CompilerParams(dimension_semantics=("arbitrary",),
                                             vmem_limit_bytes=VMEM_LIMIT),
        name="out_ffn",
    )(x, og, gate, gn, osw, wo, nw, wg, wu, wd, nf)


def _sample_kernel(q_ref, k_ref, g_ref, v_ref, s0_ref,
                   qx_ref, kn_ref, vn_ref, ck_ref, cv_ref, sink_ref,
                   s_prev_ref, ck_prev_ref, cv_prev_ref,
                   og_ref, os_ref, sn_ref, cko_ref, cvo_ref, *, nt):
    del s_prev_ref, ck_prev_ref, cv_prev_ref
    TP = SUBLANES
    q, k, g, v = q_ref[...], k_ref[...], g_ref[...], v_ref[...]
    s0 = s0_ref[...]
    tt = lax.broadcasted_iota(jnp.int32, (1, TP, 1), 1)
    b = jnp.zeros_like(g)
    for s in range(nt):
        b = b + jnp.where(tt >= s, g[:, s:s + 1, :], 0.0)
    o = jnp.einsum('gtk,gkv->gtv', (q * jnp.exp2(b)).astype(BF16), s0.astype(BF16),
                   preferred_element_type=F32)
    for s in range(nt):
        dec = jnp.exp2(jnp.where(tt >= s, b - b[:, s:s + 1, :], NEG))
        a = jnp.sum(q * k[:, s:s + 1, :] * dec, axis=-1, keepdims=True)
        o = o + a * v[:, s:s + 1, :]
    og_ref[...] = o

    b_last = b[:, nt - 1:nt, :]
    d = jnp.exp2(b_last)
    d_hi = d.astype(BF16).astype(F32)
    x = jnp.where(tt < nt, k * jnp.exp2(b_last - b), jnp.where(tt == nt, d_hi, jnp.where(tt == nt + 1, d - d_hi, 0.0)))
    ones = jnp.where((tt == nt) | (tt == nt + 1), 1.0, 0.0)
    y = jnp.concatenate([jnp.where(tt < nt, v, 0.0), jnp.broadcast_to(ones, v.shape)], axis=2)
    r = lax.dot_general(x.astype(BF16), y.astype(BF16), (((1,), (1,)), ((0,), (0,))),
                        preferred_element_type=F32)
    sn_ref[...] = r[:, :, GLA_DV:] * s0 + r[:, :, :GLA_DV]

    W = WINDOW
    qx = qx_ref[...]
    ck, cv = ck_ref[...], cv_ref[...]
    kn, vn = kn_ref[...], vn_ref[...]
    sink = sink_ref[...] * LOG2E
    nrow = SWA_KV_HEADS * nt * SWA_GROUP
    r = lax.broadcasted_iota(jnp.int32, (1, nrow, W), 1)
    tq = (r % (nt * SWA_GROUP)) // SWA_GROUP
    j = lax.broadcasted_iota(jnp.int32, (1, nrow, W), 2)
    sc = jnp.einsum('bqd,bds->bqs', qx, ck.astype(BF16), preferred_element_type=F32)
    sc = jnp.where(j > tq, sc, NEG)
    tq1 = tq[:, :, 0:1]
    qf = qx.astype(F32)
    sn = [jnp.where(tq1 >= s, jnp.sum(qf * kn[:, s:s + 1, :], axis=-1, keepdims=True), NEG)
          for s in range(nt)]
    m = jnp.maximum(jnp.max(sc, axis=-1, keepdims=True), sink)
    for s in range(nt):
        m = jnp.maximum(m, sn[s])
    pc = jnp.exp2(sc - m)
    pn = [jnp.exp2(sn[s] - m) for s in range(nt)]
    den = jnp.sum(pc, axis=-1, keepdims=True) + jnp.exp2(sink - m)
    for s in range(nt):
        den = den + pn[s]
    o = jnp.einsum('bqs,bds->bqd', pc.astype(BF16), cv.astype(BF16), preferred_element_type=F32)
    for s in range(nt):
        o = o + pn[s] * vn[:, s:s + 1, :]
    os_ref[...] = o * (1.0 / den)

    wl = lax.broadcasted_iota(jnp.int32, (1, 1, W), 2)
    pr = lax.broadcasted_iota(jnp.int32, (1, 2 * nt, W), 1)
    pw = lax.broadcasted_iota(jnp.int32, (1, 2 * nt, W), 2)
    place = jnp.broadcast_to(jnp.where(pw == W - nt + pr % nt, 1.0, 0.0), (ck.shape[0], 2 * nt, W)).astype(BF16)
    for src, new, dst in ((ck, kn, cko_ref), (cv, vn, cvo_ref)):
        hi = new.astype(BF16).astype(F32)
        x = jnp.concatenate([hi, new - hi], axis=1).astype(BF16)
        placed = lax.dot_general(x, place, (((1,), (1,)), ((0,), (0,))), preferred_element_type=F32)
        dst[...] = jnp.where(wl >= W - nt, placed, pltpu.roll(src, W - nt, 2))


def _sample_mix(q4, k4, g4, v4, state, qx, kn, vn, ck, cv, sink_rows, carried, layer, nbatch, nt):
    bt = SAMPLE_BT
    G = bt * GLA_HEADS
    TP = SUBLANES
    nrow = SWA_KV_HEADS * nt * SWA_GROUP
    b3 = lambda a, c: pl.BlockSpec((G, a, c), lambda i: (i, 0, 0))
    s3 = lambda a, c: pl.BlockSpec((bt, a, c), lambda i: (i, 0, 0))
    lay_g = pl.BlockSpec((None, G, GLA_DK, GLA_DV), lambda i: (layer, i, 0, 0))
    lay_c = pl.BlockSpec((None, bt, SWA_KV, WINDOW), lambda i: (layer, i, 0, 0))
    ng = nbatch * GLA_HEADS
    n_in = 11
    return pl.pallas_call(
        functools.partial(_sample_kernel, nt=nt),
        grid=(nbatch // bt,),
        in_specs=[b3(TP, GLA_DK), b3(TP, GLA_DK), b3(TP, GLA_DK), b3(TP, GLA_DV), lay_g,
                  s3(nrow, SWA_KV), s3(nt, SWA_KV), s3(nt, SWA_KV), lay_c, lay_c,
                  pl.BlockSpec((1, nrow, 1), lambda i: (0, 0, 0))]
                 + [pl.BlockSpec(memory_space=pl.ANY)] * 3,
        out_specs=[b3(TP, GLA_DV), s3(nrow, SWA_KV), lay_g, lay_c, lay_c],
        out_shape=[jax.ShapeDtypeStruct((ng, TP, GLA_DV), F32),
                   jax.ShapeDtypeStruct((nbatch, nrow, SWA_KV), F32),
                   jax.ShapeDtypeStruct((DEPTH, ng, GLA_DK, GLA_DV), F32),
                   jax.ShapeDtypeStruct((DEPTH, nbatch, SWA_KV, WINDOW), F32),
                   jax.ShapeDtypeStruct((DEPTH, nbatch, SWA_KV, WINDOW), F32)],
        input_output_aliases={n_in + j: 2 + j for j in range(3)},
        compiler_params=pltpu.CompilerParams(dimension_semantics=("arbitrary",),
                                             vmem_limit_bytes=VMEM_LIMIT),
        name="sample_mix",
    )(q4, k4, g4, v4, state, qx, kn, vn, ck, cv, sink_rows, *carried)


def _rope_tables(pos):
    half = SWA_HEAD_DIM // 2
    inv = ROPE_THETA ** (-jnp.arange(half, dtype=F32) / half)
    ang = pos.astype(F32)[:, None] * inv[None, :]
    cos, sin = jnp.cos(ang), jnp.sin(ang)
    zero = jnp.zeros_like(sin)
    rep = LANES // SWA_HEAD_DIM
    return (jnp.tile(jnp.concatenate([cos, cos], axis=1), (1, rep)),
            jnp.tile(jnp.concatenate([-sin, zero], axis=1), (1, rep)),
            jnp.tile(jnp.concatenate([zero, sin], axis=1), (1, rep)))


def kernel(x_prompt, x_sample, state_gla, cache_swa_k, cache_swa_v, norm_attn, w_in, w_gk2, b_gk2,
           gla_norm, attn_sinks, w_o, norm_ffn, w_gate, w_up, w_down, norm_final):
    batch, seq, _ = x_prompt.shape
    nbatch, nt, _ = x_sample.shape
    n_s = nbatch * nt

    splits = np.cumsum([GLA_QK, GLA_QK, GLA_WIDTH, GATE_RANK, GLA_WIDTH, SWA_Q, SWA_KV, SWA_KV])
    c_ga0, c_ga1 = int(splits[2]), int(splits[3])
    w_in_b = w_in.astype(BF16)
    w_in_p = jnp.concatenate(
        [w_in_b[:, :, :c_ga0], w_in_b[:, :, c_ga1:], w_in_b[:, :, c_ga0:c_ga1],
         jnp.zeros((DEPTH, D_MODEL, LANES - GATE_RANK), BF16)], axis=2)
    wg2_p = jnp.concatenate([w_gk2, jnp.zeros((DEPTH, LANES - GATE_RANK, GLA_QK), w_gk2.dtype)],
                            axis=1).astype(BF16)
    bg2 = b_gk2.reshape(DEPTH, 1, GLA_QK)
    nw_a = norm_attn.reshape(DEPTH, 1, D_MODEL)
    nw_f = norm_ffn.reshape(DEPTH, 1, D_MODEL)
    wo_b, wg_b, wu_b, wd_b = (w.astype(BF16) for w in (w_o, w_gate, w_up, w_down))
    nf = norm_final.reshape(1, D_MODEL)
    gn = gla_norm.reshape(DEPTH, 1, GLA_DV)

    tabs_p = _rope_tables(jnp.arange(seq))
    tabs_s = _rope_tables(jnp.tile(PAST_LEN + jnp.arange(nt), TOK_TILE // nt))

    state_r = state_gla.reshape(DEPTH, nbatch * GLA_HEADS, GLA_DK, GLA_DV)
    pos_minor = lambda c: c.transpose(0, 1, 3, 4, 2).reshape(DEPTH, nbatch, SWA_KV, WINDOW)
    ck_r, cv_r = pos_minor(cache_swa_k), pos_minor(cache_swa_v)

    xp = x_prompt.reshape(batch * seq, D_MODEL)
    xs = x_sample.reshape(n_s, D_MODEL)
    sp_l, kp_l, vp_l = [], [], []
    carried = [jnp.zeros(state_r.shape, F32), jnp.zeros(ck_r.shape, F32), jnp.zeros(cv_r.shape, F32)]
    tpad = ((0, 0), (0, SUBLANES - nt), (0, 0))
    for l in range(DEPTH):
        final = l == DEPTH - 1
        q, k, g_hi, g_lo, v, gate, sq, sk, sv = _in_proj(xp, nw_a, w_in_p, wg2_p, bg2, tabs_p, l)
        og, s_fin = _gla_prompt(q, k, g_hi, g_lo, v, batch, seq)
        osw = _swa_prompt(attn_sinks[l], sq, sk, sv, batch, seq)
        xp = _out_ffn(xp, og, gate, gn, osw, wo_b, nw_f, wg_b, wu_b, wd_b, nf, l, final)
        sp_l.append(s_fin)
        last = lambda a: a.reshape(batch, seq, SWA_KV)[:, seq - WINDOW:].reshape(
            batch, WINDOW, SWA_KV_HEADS, SWA_HEAD_DIM)
        kp_l.append(last(sk))
        vp_l.append(last(sv))

        q, k, g_hi, g_lo, v, gate, sq, sk, sv = _in_proj(xs, nw_a, w_in_p, wg2_p, bg2, tabs_s, l)
        g = g_hi.astype(F32) + g_lo.astype(F32)

        def bhtk(a, w_):
            return a.reshape(nbatch, nt, GLA_HEADS, w_).transpose(0, 2, 1, 3).reshape(nbatch * GLA_HEADS, nt, w_)

        q4, k4, g4 = (jnp.pad(bhtk(a, GLA_DK), tpad) for a in (q, k, g))
        v4 = jnp.pad(bhtk(v.astype(F32), GLA_DV), tpad)
        qk = sq.reshape(nbatch, nt, SWA_KV_HEADS, SWA_GROUP, SWA_HEAD_DIM).transpose(0, 2, 1, 3, 4)
        zq = jnp.zeros_like(qk[:, 0])
        qx = jnp.stack([jnp.concatenate([qk[:, 0], zq], axis=-1), jnp.concatenate([zq, qk[:, 1]], axis=-1)],
                       axis=1).reshape(nbatch, SWA_KV_HEADS * nt * SWA_GROUP, SWA_KV)
        sink_rows = jnp.broadcast_to(attn_sinks[l].reshape(SWA_KV_HEADS, 1, SWA_GROUP),
                                     (SWA_KV_HEADS, nt, SWA_GROUP)).reshape(1, -1, 1)
        og4, os4, *carried = _sample_mix(
            q4, k4, g4, v4, state_r, qx, sk.reshape(nbatch, nt, SWA_KV), sv.reshape(nbatch, nt, SWA_KV),
            ck_r, cv_r, sink_rows, carried, l, nbatch, nt)
        og = og4[:, :nt].reshape(nbatch, GLA_HEADS, nt, GLA_DV).transpose(0, 2, 1, 3).reshape(n_s, GLA_WIDTH)
        os5 = os4.reshape(nbatch, SWA_KV_HEADS, nt, SWA_GROUP, SWA_KV)
        osw = jnp.stack([os5[:, 0, :, :, :SWA_HEAD_DIM], os5[:, 1, :, :, SWA_HEAD_DIM:]], axis=2)
        osw = osw.reshape(n_s, SWA_Q)
        xs = _out_ffn(xs, og.astype(BF16), gate, gn, osw.astype(BF16), wo_b, nw_f, wg_b, wu_b, wd_b, nf, l, final)

    s_all, ck_all, cv_all = carried
    pos_major = lambda c: c.reshape(DEPTH, nbatch, SWA_KV_HEADS, SWA_HEAD_DIM, WINDOW).transpose(0, 1, 4, 2, 3)
    return (xp.reshape(batch, seq, D_MODEL), xs.reshape(nbatch, nt, D_MODEL),
            jnp.stack(sp_l), jnp.stack(kp_l), jnp.stack(vp_l),
            s_all.reshape(DEPTH, nbatch, GLA_HEADS, GLA_DK, GLA_DV), pos_major(ck_all), pos_major(cv_all))
```

```python
import functools

import jax
import jax.numpy as jnp
import numpy as np
from jax import lax
from jax.experimental import pallas as pl
from jax.experimental.pallas import tpu as pltpu

F32 = jnp.float32
BF16 = jnp.bfloat16

D_MODEL = 1024
DEPTH = 4
PAST_LEN = 8192
GLA_HEADS = 4
GLA_DV = 128
GLA_DK = 64
GLA_QK = GLA_HEADS * GLA_DK
GLA_WIDTH = GLA_HEADS * GLA_DV
GATE_RANK = 16
GATE_NORMALIZER = 16.0
SWA_HEAD_DIM = 64
SWA_HEADS = 8
SWA_KV_HEADS = 2
SWA_GROUP = 4
SWA_Q = SWA_HEADS * SWA_HEAD_DIM
SWA_KV = SWA_KV_HEADS * SWA_HEAD_DIM
WINDOW = 128
ROPE_THETA = 10000.0
D_FF = 2816
NORM_EPS = 1e-6

LANES = 128
SUBLANES = 8
VMEM_LIMIT = 56 * 1024 * 1024

C_Q, C_K, C_V, C_GG, C_SQ, C_SK, C_SV, C_GA = 0, 256, 512, 1024, 1536, 2048, 2176, 2304
IN_COLS_P = C_GA + LANES

TOK_TILE = 512
GLA_CHUNK = 64
GLA_SUB = 8
GLA_TILE = 512
GLA_UNROLL = 8
FF_CHUNK = D_FF
FF_SPLITS = (0, 768, 1536, 2304, D_FF)
SWA_ROWS = 32
SAMPLE_BT = 8
NEG = -1e30
LOG2E = 1.4426950408889634


def _rms(x, w):
    ms = jnp.mean(x * x, axis=-1, keepdims=True)
    return x * lax.rsqrt(ms + NORM_EPS) * w


def _sigmoid(x):
    return 1.0 / (1.0 + jnp.exp(-x))


def _dot(a, b):
    return jnp.dot(a, b, preferred_element_type=F32)


def _dot_nt(a, b):
    return lax.dot_general(a, b, (((1,), (1,)), ((), ())), preferred_element_type=F32)


def _in_proj_kernel(x_ref, nw_ref, w_ref, wg2_ref, bg2_ref, cos_ref, sa_ref, sb_ref,
                    q_ref, k_ref, ghi_ref, glo_ref, v_ref, gate_ref, sq_ref, sk_ref, sv_ref):
    h = _rms(x_ref[...], nw_ref[...]).astype(BF16)
    z = _dot(h, w_ref[...])

    def proj(c0, c1):
        return z[:, c0:c1]

    q_ref[...] = proj(C_Q, C_K) * (GLA_DK ** -0.5)
    k_ref[...] = proj(C_K, C_V)
    v_ref[...] = proj(C_V, C_GG).astype(BF16)
    gg = proj(C_GG, C_SQ)
    gate_ref[...] = gg * _sigmoid(gg)
    ga = proj(C_GA, IN_COLS_P).astype(BF16)
    gx = _dot(ga, wg2_ref[...]) + bg2_ref[...]
    g2 = (jnp.minimum(gx, 0.0) - jnp.log1p(jnp.exp(-jnp.abs(gx)))) * (LOG2E / GATE_NORMALIZER)
    g_hi = g2.astype(BF16)
    ghi_ref[...] = g_hi
    glo_ref[...] = (g2 - g_hi.astype(F32)).astype(BF16)

    cos, sa, sb = cos_ref[...], sa_ref[...], sb_ref[...]

    def rope(z):
        return z * cos + pltpu.roll(z, LANES - 32, 1) * sa + pltpu.roll(z, 32, 1) * sb

    for j in range(SWA_Q // (2 * LANES)):
        zq = proj(C_SQ + 2 * j * LANES, C_SQ + 2 * (j + 1) * LANES)
        for jj in range(2):
            sq_ref[:, (2 * j + jj) * LANES:(2 * j + jj + 1) * LANES] = (
                rope(zq[:, jj * LANES:(jj + 1) * LANES]) * (SWA_HEAD_DIM ** -0.5 * LOG2E)).astype(BF16)
    zkv = proj(C_SK, C_GA)
    sk_ref[...] = rope(zkv[:, :SWA_KV])
    sv_ref[...] = zkv[:, SWA_KV:]


def _in_proj(x, nw, w, wg2, bg2, tabs, layer):
    n = x.shape[0]
    nt = n // TOK_TILE
    cos, sa, sb = tabs
    ntab = cos.shape[0] // TOK_TILE
    row = lambda w_: pl.BlockSpec((TOK_TILE, w_), lambda i: (i, 0))
    tab = pl.BlockSpec((TOK_TILE, LANES), lambda i: (i % ntab, 0))
    const = lambda shp: pl.BlockSpec((None,) + shp, lambda i: (layer, 0, 0), pipeline_mode=pl.Buffered(1))
    outs = [(GLA_QK, F32), (GLA_QK, F32), (GLA_QK, BF16), (GLA_QK, BF16), (GLA_WIDTH, BF16), (GLA_WIDTH, F32),
            (SWA_Q, BF16), (SWA_KV, F32), (SWA_KV, F32)]
    return pl.pallas_call(
        _in_proj_kernel,
        grid=(nt,),
        in_specs=[row(D_MODEL), const((1, D_MODEL)), const((D_MODEL, IN_COLS_P)),
                  const((LANES, GLA_QK)), const((1, GLA_QK)), tab, tab, tab],
        out_specs=[row(w_) for w_, _ in outs],
        out_shape=[jax.ShapeDtypeStruct((n, w_), dt) for w_, dt in outs],
        compiler_params=pltpu.CompilerParams(dimension_semantics=("arbitrary",),
                                             vmem_limit_bytes=VMEM_LIMIT),
        name="in_proj",
    )(x, nw, w, wg2, bg2, cos, sa, sb)


def _gla_kernel(q_ref, k_ref, ghi_ref, glo_ref, v_ref, bias_ref, o_ref, sout_ref, s_ref):
    ti = pl.program_id(1)
    C, SB = GLA_CHUNK, GLA_SUB
    NB = C // SB
    NP = GLA_HEADS // 2

    @pl.when(ti == 0)
    def _():
        s_ref[...] = jnp.zeros_like(s_ref)

    ltri = (lax.broadcasted_iota(jnp.int32, (C, 2 * C), 0) >= lax.broadcasted_iota(jnp.int32, (C, 2 * C), 1) % C
            ).astype(F32).astype(BF16)
    lane = lax.broadcasted_iota(jnp.int32, (1, LANES), 1)
    mask_a = (lane < GLA_DK).astype(F32)
    mask_b = 1.0 - mask_a
    col8 = lax.broadcasted_iota(jnp.int32, (SB, LANES), 1) % C
    colblk, colmod = col8 // SB, col8 % SB
    zk = jnp.zeros((GLA_DK, GLA_DV), BF16)
    off1 = [sum(C - SB * (jj + 1) for jj in range(j)) for j in range(NB - 1)]

    def sub_bcast(x, j):
        x3 = x.reshape(NB, SB, LANES)
        return jnp.broadcast_to(x3[:, j:j + 1, :], (NB, SB, LANES)).reshape(C, LANES)

    def heads(x):
        return jnp.concatenate([x * mask_a, x * mask_b], axis=0).astype(BF16)

    def body(it, carry):
        rows = [pl.ds(pl.multiple_of((it * GLA_UNROLL + cc) * C, C), C) for cc in range(GLA_UNROLL)]
        units = [(cc, p) for cc in range(GLA_UNROLL) for p in range(NP)]
        kl = [slice(p * LANES, (p + 1) * LANES) for p in range(NP)]
        vl = [slice(h * GLA_DV, (h + 1) * GLA_DV) for h in range(GLA_HEADS)]

        b_all = [_dot(ltri, jnp.concatenate([ghi_ref[rows[cc], :], glo_ref[rows[cc], :]], axis=0))
                 for cc in range(GLA_UNROLL)]

        qe, outs, dec, upd = {}, {}, {}, {}
        for cc, p in units:
            q, k, b = q_ref[rows[cc], kl[p]], k_ref[rows[cc], kl[p]], b_all[cc][:, kl[p]]
            b_last = b[C - 1:C, :]
            k1 = k * jnp.exp2(sub_bcast(b, SB - 1) - b)
            lhs1 = jnp.concatenate(
                [q[SB * (j + 1):, :] * jnp.exp2(b[SB * (j + 1):, :] - b[SB * j + SB - 1:SB * j + SB, :])
                 for j in range(NB - 1)], axis=0)
            lhs2 = jnp.concatenate(
                [q * jnp.exp2((b - sub_bcast(b, jj)) + jnp.tile(bias_ref[jj], (NB, 1))) for jj in range(SB)],
                axis=0)
            qe[cc, p] = (q * jnp.exp2(b)).astype(BF16)
            outs[cc, p] = (_dot_nt(lhs1.astype(BF16), heads(k1)), _dot_nt(lhs2.astype(BF16), heads(k)))
            kd = k * jnp.exp2(b_last - b)
            mt = jnp.concatenate([kd, jnp.broadcast_to(b_last, (C, LANES))], axis=0).T
            kdt = mt[:, :C].astype(BF16)
            dec[cc, p] = jnp.exp2(mt[:, C:C + 1])
            upd[cc, p] = (_dot(kdt[:GLA_DK], v_ref[rows[cc], vl[2 * p]]),
                          _dot(kdt[GLA_DK:], v_ref[rows[cc], vl[2 * p + 1]]))

        lo = {}
        for u in units:
            out1, out2 = outs[u]
            a_rows = []
            for i in range(NB):
                d = out2[(SB - 1) * C + SB * i:(SB - 1) * C + SB * (i + 1), :]
                for jj in range(SB - 2, -1, -1):
                    d = jnp.where(colmod == jj, out2[jj * C + SB * i:jj * C + SB * (i + 1), :], d)
                acc = jnp.where(colblk == i, d, 0.0)
                for j in range(i):
                    r0 = off1[j] + SB * (i - j - 1)
                    acc = jnp.where(colblk == j, out1[r0:r0 + SB, :], acc)
                a_rows.append(acc)
            a = jnp.concatenate(a_rows, axis=0)
            lo[u] = jnp.concatenate([qe[u], a.astype(BF16)], axis=1)

        state = [s_ref[h] for h in range(GLA_HEADS)]
        o_all = {}
        for cc in range(GLA_UNROLL):
            for p in range(NP):
                ha, hb = 2 * p, 2 * p + 1
                v_a, v_b = v_ref[rows[cc], vl[ha]], v_ref[rows[cc], vl[hb]]
                o_all[cc, ha] = _dot(lo[cc, p], jnp.concatenate([state[ha].astype(BF16), zk, v_a, zk], axis=0))
                o_all[cc, hb] = _dot(lo[cc, p], jnp.concatenate([zk, state[hb].astype(BF16), zk, v_b], axis=0))
                state[ha] = dec[cc, p][:GLA_DK] * state[ha] + upd[cc, p][0]
                state[hb] = dec[cc, p][GLA_DK:] * state[hb] + upd[cc, p][1]
        for h in range(GLA_HEADS):
            s_ref[h] = state[h]

        for cc in range(GLA_UNROLL):
            for h in range(GLA_HEADS):
                o_ref[rows[cc], vl[h]] = o_all[cc, h].astype(BF16)
        return carry

    lax.fori_loop(0, GLA_TILE // (C * GLA_UNROLL), body, 0)

    @pl.when(ti == pl.num_programs(1) - 1)
    def _():
        sout_ref[0] = s_ref[...]


def _gla_prompt(q, k, g_hi, g_lo, v, batch, seq):
    nt = seq // GLA_TILE
    blk = lambda w_: pl.BlockSpec((GLA_TILE, w_), lambda b, t: (b * nt + t, 0))
    r = np.arange(GLA_SUB)
    bias = np.where(r[None, :, None] >= r[:, None, None], 0.0, NEG) * np.ones((1, 1, LANES))
    return pl.pallas_call(
        _gla_kernel,
        grid=(batch, nt),
        in_specs=[blk(GLA_QK), blk(GLA_QK), blk(GLA_QK), blk(GLA_QK), blk(GLA_WIDTH),
                  pl.BlockSpec((GLA_SUB, GLA_SUB, LANES), lambda b, t: (0, 0, 0))],
        out_specs=[blk(GLA_WIDTH),
                   pl.BlockSpec((1, GLA_HEADS, GLA_DK, GLA_DV), lambda b, t: (b, 0, 0, 0))],
        out_shape=[jax.ShapeDtypeStruct((batch * seq, GLA_WIDTH), BF16),
                   jax.ShapeDtypeStruct((batch, GLA_HEADS, GLA_DK, GLA_DV), F32)],
        scratch_shapes=[pltpu.VMEM((GLA_HEADS, GLA_DK, GLA_DV), F32)],
        compiler_params=pltpu.CompilerParams(
            dimension_semantics=("arbitrary", "arbitrary"), vmem_limit_bytes=VMEM_LIMIT),
        name="gla_prompt",
    )(q, k, g_hi, g_lo, v, jnp.asarray(bias, F32))


def _out_ffn_kernel(x_ref, og_ref, gate_ref, gn_ref, os_ref, wo_ref, nw_ref, wg_ref, wu_ref, wd_ref, nf_ref,
                    y_ref, *, final):
    gn = gn_ref[...]
    og = jnp.concatenate([_rms(og_ref[:, h * GLA_DV:(h + 1) * GLA_DV].astype(F32), gn)
                          for h in range(GLA_HEADS)], axis=1) * gate_ref[...]
    x = (x_ref[...] + _dot(og.astype(BF16), wo_ref[:GLA_WIDTH, :]) + _dot(os_ref[...], wo_ref[GLA_WIDTH:, :]))
    h = _rms(x, nw_ref[...]).astype(BF16)
    acc = x
    for c0 in range(0, D_FF, FF_CHUNK):
        a = _dot(h, wg_ref[:, c0:c0 + FF_CHUNK])
        u = _dot(h, wu_ref[:, c0:c0 + FF_CHUNK])
        acc = acc + _dot((a * _sigmoid(a) * u).astype(BF16), wd_ref[c0:c0 + FF_CHUNK, :])
    if final:
        acc = _rms(acc, nf_ref[...])
    y_ref[...] = acc


def _out_ffn(x, og, gate, gn, osw, wo, nw, wg, wu, wd, nf, layer, final):
    n = x.shape[0]
    row = lambda w_: pl.BlockSpec((TOK_TILE, w_), lambda i: (i, 0))
    const = lambda shp: pl.BlockSpec((None,) + shp, lambda i: (layer, 0, 0), pipeline_mode=pl.Buffered(1))
    return pl.pallas_call(
        functools.partial(_out_ffn_kernel, final=final),
        grid=(n // TOK_TILE,),
        in_specs=[row(D_MODEL), row(GLA_WIDTH), row(GLA_WIDTH), const((1, GLA_DV)), row(SWA_Q),
                  const((D_MODEL, D_MODEL)), const((1, D_MODEL)),
                  const((D_MODEL, D_FF)), const((D_MODEL, D_FF)), const((D_FF, D_MODEL)),
                  pl.BlockSpec((1, D_MODEL), lambda i: (0, 0))],
        out_specs=row(D_MODEL),
        out_shape=jax.ShapeDtypeStruct((n, D_MODEL), F32),
        compiler_params=pltpu.CompilerParams(dimension_semantics=("arbitrary",),
                                             vmem_limit_bytes=VMEM_LIMIT),
        name="out_ffn",
    )(x, og, gate, gn, osw, wo, nw, wg, wu, wd, nf)


def _swa_setup(kp_ref, kc_ref, vp_ref, vc_ref, ke_scr, ve_scr):
    kk = jnp.concatenate([kp_ref[...], kc_ref[...]], axis=0)
    vv = jnp.concatenate([vp_ref[...], vc_ref[...]], axis=0)
    lo = lax.broadcasted_iota(jnp.int32, kk.shape, 1) < SWA_HEAD_DIM
    for x, scr in ((kk, ke_scr), (vv, ve_scr)):
        x_sw = pltpu.roll(x, SWA_HEAD_DIM, 1)
        scr[0, 0] = jnp.where(lo, x, 0.0).astype(BF16)
        scr[0, 1] = jnp.where(lo, 0.0, x_sw).astype(BF16)
        scr[1, 0] = jnp.where(lo, x_sw, 0.0).astype(BF16)
        scr[1, 1] = jnp.where(lo, 0.0, x).astype(BF16)


def _swa_scores(j, q_ref, ke_scr, s_scr):
    W = WINDOW
    for p in range(SWA_HEADS // 2):
        kv = p // (SWA_GROUP // 2)
        q = q_ref[j * W:(j + 1) * W, p * LANES:(p + 1) * LANES]
        for hh in range(2):
            s_scr[p, :, hh * 2 * W:(hh + 1) * 2 * W] = _dot_nt(q, ke_scr[kv, hh, j * W:(j + 2) * W, :])


def _swa_softmax(sink_ref, bias_blk, s_scr, p_scr, r_scr):
    W = WINDOW
    lane_q = lax.broadcasted_iota(jnp.int32, (SWA_ROWS, LANES), 1) < SWA_HEAD_DIM
    for p in range(SWA_HEADS // 2):
        for r0 in range(0, W, SWA_ROWS):
            rs = slice(r0, r0 + SWA_ROWS)
            rden = []
            for hh in range(2):
                cs = slice(hh * 2 * W, (hh + 1) * 2 * W)
                sink = sink_ref[2 * p + hh] * LOG2E
                sh = s_scr[p, rs, cs] + bias_blk[rs, :]
                m = jnp.maximum(jnp.max(sh, axis=-1, keepdims=True), sink)
                e = jnp.exp2(sh - m)
                den = jnp.sum(e, axis=-1, keepdims=True) + jnp.exp2(sink - m)
                p_scr[p, rs, cs] = e.astype(BF16)
                rden.append(1.0 / den)
            r_scr[p, rs, :] = jnp.where(lane_q, rden[0], rden[1])


def _swa_values(j, p_scr, ve_scr, r_scr, o_ref):
    W = WINDOW
    for p in range(SWA_HEADS // 2):
        kv = p // (SWA_GROUP // 2)
        o = (_dot(p_scr[p, :, :2 * W], ve_scr[kv, 0, j * W:(j + 2) * W, :])
             + _dot(p_scr[p, :, 2 * W:], ve_scr[kv, 1, j * W:(j + 2) * W, :])) * r_scr[p]
        o_ref[j * W:(j + 1) * W, p * LANES:(p + 1) * LANES] = o.astype(BF16)


def _ffn_swa_kernel(sink_ref, x_ref, og_ref, gate_ref, gn_ref, wo_ref, nw_ref, wg_ref, wu_ref, wd_ref, nf_ref,
                    q_ref, kp_ref, kc_ref, vp_ref, vc_ref, bias_ref, y_ref,
                    osw_scr, ke_scr, ve_scr, s_scr, p_scr, r_scr, *, final, tiles_per_seq, n_tiles):
    s = pl.program_id(0)
    nblk = TOK_TILE // WINDOW

    @pl.when(s == 0)
    def _():
        osw_scr[1] = jnp.zeros(osw_scr.shape[1:], BF16)

    _swa_setup(kp_ref, kc_ref, vp_ref, vc_ref, ke_scr, ve_scr)
    t_seq = jnp.minimum(s, n_tiles - 1) % tiles_per_seq
    osw_cur = osw_scr.at[s % 2]

    gn = gn_ref[...]
    og = jnp.concatenate([_rms(og_ref[:, h * GLA_DV:(h + 1) * GLA_DV].astype(F32), gn)
                          for h in range(GLA_HEADS)], axis=1) * gate_ref[...]
    x = (x_ref[...] + _dot(og.astype(BF16), wo_ref[:GLA_WIDTH, :])
         + _dot(osw_scr[(s + 1) % 2], wo_ref[GLA_WIDTH:, :]))
    h = _rms(x, nw_ref[...]).astype(BF16)

    t_parts = []
    for j in range(nblk):
        c0, c1 = FF_SPLITS[j], FF_SPLITS[j + 1]
        _swa_scores(j, q_ref, ke_scr, s_scr)
        a = _dot(h, wg_ref[:, c0:c1])
        u = _dot(h, wu_ref[:, c0:c1])
        t_parts.append((a * _sigmoid(a) * u).astype(BF16))
        _swa_softmax(sink_ref, bias_ref.at[jnp.minimum(t_seq * nblk + j, 1)], s_scr, p_scr, r_scr)
        _swa_values(j, p_scr, ve_scr, r_scr, osw_cur)
    acc = x + _dot(jnp.concatenate(t_parts, axis=1), wd_ref[...])
    if final:
        acc = _rms(acc, nf_ref[...])
    y_ref[...] = acc


def _ffn_swa(x, og, gate, gn, sinks, sq, sk, sv, wo, nw, wg, wu, wd, nf, layer, final, seq):
    n = x.shape[0]
    W = WINDOW
    n_tiles = n // TOK_TILE
    tiles_per_seq = seq // TOK_TILE
    nblk = TOK_TILE // W
    ffn_i = lambda s: (jnp.maximum(s - 1, 0), 0)
    swa_i = lambda s: (jnp.minimum(s, n_tiles - 1), 0)

    def prev_i(s):
        t = jnp.minimum(s, n_tiles - 1)
        return (t - t % tiles_per_seq) * nblk + jnp.maximum((t % tiles_per_seq) * nblk - 1, 0), 0

    row = lambda w_, im: pl.BlockSpec((TOK_TILE, w_), im)
    const = lambda shp: pl.BlockSpec((None,) + shp, lambda s: (layer, 0, 0), pipeline_mode=pl.Buffered(1))
    qi, kj = np.arange(W)[:, None], np.arange(2 * W)[None, :]
    band = (kj > qi) & (kj <= qi + W)
    bias = np.stack([np.where(band & (kj >= W), 0.0, NEG), np.where(band, 0.0, NEG)])
    ext_scr = pltpu.VMEM((SWA_KV_HEADS, 2, W + TOK_TILE, LANES), BF16)
    npair = SWA_HEADS // 2
    return pl.pallas_call(
        functools.partial(_ffn_swa_kernel, final=final, tiles_per_seq=tiles_per_seq, n_tiles=n_tiles),
        grid=(n_tiles + 1,),
        in_specs=[pl.BlockSpec(memory_space=pltpu.SMEM),
                  row(D_MODEL, ffn_i), row(GLA_WIDTH, ffn_i), row(GLA_WIDTH, ffn_i), const((1, GLA_DV)),
                  const((D_MODEL, D_MODEL)), const((1, D_MODEL)),
                  const((D_MODEL, D_FF)), const((D_MODEL, D_FF)), const((D_FF, D_MODEL)),
                  pl.BlockSpec((1, D_MODEL), lambda s: (0, 0)),
                  row(SWA_Q, swa_i),
                  pl.BlockSpec((W, SWA_KV), prev_i), row(SWA_KV, swa_i),
                  pl.BlockSpec((W, SWA_KV), prev_i), row(SWA_KV, swa_i),
                  pl.BlockSpec((2, W, 2 * W), lambda s: (0, 0, 0))],
        out_specs=row(D_MODEL, ffn_i),
        out_shape=jax.ShapeDtypeStruct((n, D_MODEL), F32),
        scratch_shapes=[pltpu.VMEM((2, TOK_TILE, SWA_Q), BF16), ext_scr, ext_scr,
                        pltpu.VMEM((npair, W, 4 * W), F32), pltpu.VMEM((npair, W, 4 * W), BF16),
                        pltpu.VMEM((npair, W, LANES), F32)],
        compiler_params=pltpu.CompilerParams(dimension_semantics=("arbitrary",),
                                             vmem_limit_bytes=VMEM_LIMIT),
        name="ffn_swa",
    )(sinks, x, og, gate, gn, wo, nw, wg, wu, wd, nf, sq, sk, sk, sv, sv, jnp.asarray(bias, F32))


def _sample_kernel(q_ref, k_ref, g_ref, v_ref, s0_ref,
                   qx_ref, kn_ref, vn_ref, ck_ref, cv_ref, sink_ref,
                   s_prev_ref, ck_prev_ref, cv_prev_ref,
                   og_ref, os_ref, sn_ref, cko_ref, cvo_ref, *, nt):
    del s_prev_ref, ck_prev_ref, cv_prev_ref
    TP = SUBLANES
    q, k, g, v = q_ref[...], k_ref[...], g_ref[...], v_ref[...]
    s0 = s0_ref[...]
    tt = lax.broadcasted_iota(jnp.int32, (1, TP, 1), 1)
    b = jnp.zeros_like(g)
    for s in range(nt):
        b = b + jnp.where(tt >= s, g[:, s:s + 1, :], 0.0)
    o = jnp.einsum('gtk,gkv->gtv', (q * jnp.exp2(b)).astype(BF16), s0.astype(BF16),
                   preferred_element_type=F32)
    for s in range(nt):
        dec = jnp.exp2(jnp.where(tt >= s, b - b[:, s:s + 1, :], NEG))
        a = jnp.sum(q * k[:, s:s + 1, :] * dec, axis=-1, keepdims=True)
        o = o + a * v[:, s:s + 1, :]
    og_ref[...] = o

    b_last = b[:, nt - 1:nt, :]
    d = jnp.exp2(b_last)
    d_hi = d.astype(BF16).astype(F32)
    x = jnp.where(tt < nt, k * jnp.exp2(b_last - b), jnp.where(tt == nt, d_hi, jnp.where(tt == nt + 1, d - d_hi, 0.0)))
    ones = jnp.where((tt == nt) | (tt == nt + 1), 1.0, 0.0)
    y = jnp.concatenate([jnp.where(tt < nt, v, 0.0), jnp.broadcast_to(ones, v.shape)], axis=2)
    r = lax.dot_general(x.astype(BF16), y.astype(BF16), (((1,), (1,)), ((0,), (0,))),
                        preferred_element_type=F32)
    sn_ref[...] = r[:, :, GLA_DV:] * s0 + r[:, :, :GLA_DV]

    W = WINDOW
    qx = qx_ref[...]
    ck, cv = ck_ref[...], cv_ref[...]
    kn, vn = kn_ref[...], vn_ref[...]
    sink = sink_ref[...] * LOG2E
    nrow = SWA_KV_HEADS * nt * SWA_GROUP
    r = lax.broadcasted_iota(jnp.int32, (1, nrow, W), 1)
    tq = (r % (nt * SWA_GROUP)) // SWA_GROUP
    j = lax.broadcasted_iota(jnp.int32, (1, nrow, W), 2)
    sc = jnp.einsum('bqd,bds->bqs', qx, ck.astype(BF16), preferred_element_type=F32)
    sc = jnp.where(j > tq, sc, NEG)
    tq1 = tq[:, :, 0:1]
    qf = qx.astype(F32)
    sn = [jnp.where(tq1 >= s, jnp.sum(qf * kn[:, s:s + 1, :], axis=-1, keepdims=True), NEG)
          for s in range(nt)]
    m = jnp.maximum(jnp.max(sc, axis=-1, keepdims=True), sink)
    for s in range(nt):
        m = jnp.maximum(m, sn[s])
    pc = jnp.exp2(sc - m)
    pn = [jnp.exp2(sn[s] - m) for s in range(nt)]
    den = jnp.sum(pc, axis=-1, keepdims=True) + jnp.exp2(sink - m)
    for s in range(nt):
        den = den + pn[s]
    o = jnp.einsum('bqs,bds->bqd', pc.astype(BF16), cv.astype(BF16), preferred_element_type=F32)
    for s in range(nt):
        o = o + pn[s] * vn[:, s:s + 1, :]
    os_ref[...] = o * (1.0 / den)

    wl = lax.broadcasted_iota(jnp.int32, (1, 1, W), 2)
    pr = lax.broadcasted_iota(jnp.int32, (1, 2 * nt, W), 1)
    pw = lax.broadcasted_iota(jnp.int32, (1, 2 * nt, W), 2)
    place = jnp.broadcast_to(jnp.where(pw == W - nt + pr % nt, 1.0, 0.0), (ck.shape[0], 2 * nt, W)).astype(BF16)
    for src, new, dst in ((ck, kn, cko_ref), (cv, vn, cvo_ref)):
        hi = new.astype(BF16).astype(F32)
        x = jnp.concatenate([hi, new - hi], axis=1).astype(BF16)
        placed = lax.dot_general(x, place, (((1,), (1,)), ((0,), (0,))), preferred_element_type=F32)
        dst[...] = jnp.where(wl >= W - nt, placed, pltpu.roll(src, W - nt, 2))


def _sample_mix(q4, k4, g4, v4, state, qx, kn, vn, ck, cv, sink_rows, carried, layer, nbatch, nt):
    bt = SAMPLE_BT
    G = bt * GLA_HEADS
    TP = SUBLANES
    nrow = SWA_KV_HEADS * nt * SWA_GROUP
    b3 = lambda a, c: pl.BlockSpec((G, a, c), lambda i: (i, 0, 0))
    s3 = lambda a, c: pl.BlockSpec((bt, a, c), lambda i: (i, 0, 0))
    lay_g = pl.BlockSpec((None, G, GLA_DK, GLA_DV), lambda i: (layer, i, 0, 0))
    lay_c = pl.BlockSpec((None, bt, SWA_KV, WINDOW), lambda i: (layer, i, 0, 0))
    ng = nbatch * GLA_HEADS
    n_in = 11
    return pl.pallas_call(
        functools.partial(_sample_kernel, nt=nt),
        grid=(nbatch // bt,),
        in_specs=[b3(TP, GLA_DK), b3(TP, GLA_DK), b3(TP, GLA_DK), b3(TP, GLA_DV), lay_g,
                  s3(nrow, SWA_KV), s3(nt, SWA_KV), s3(nt, SWA_KV), lay_c, lay_c,
                  pl.BlockSpec((1, nrow, 1), lambda i: (0, 0, 0))]
                 + [pl.BlockSpec(memory_space=pl.ANY)] * 3,
        out_specs=[b3(TP, GLA_DV), s3(nrow, SWA_KV), lay_g, lay_c, lay_c],
        out_shape=[jax.ShapeDtypeStruct((ng, TP, GLA_DV), F32),
                   jax.ShapeDtypeStruct((nbatch, nrow, SWA_KV), F32),
                   jax.ShapeDtypeStruct((DEPTH, ng, GLA_DK, GLA_DV), F32),
                   jax.ShapeDtypeStruct((DEPTH, nbatch, SWA_KV, WINDOW), F32),
                   jax.ShapeDtypeStruct((DEPTH, nbatch, SWA_KV, WINDOW), F32)],
        input_output_aliases={n_in + j: 2 + j for j in range(3)},
        compiler_params=pltpu.CompilerParams(dimension_semantics=("arbitrary",),
                                             vmem_limit_bytes=VMEM_LIMIT),
        name="sample_mix",
    )(q4, k4, g4, v4, state, qx, kn, vn, ck, cv, sink_rows, *carried)


def _rope_tables(pos):
    half = SWA_HEAD_DIM // 2
    inv = ROPE_THETA ** (-jnp.arange(half, dtype=F32) / half)
    ang = pos.astype(F32)[:, None] * inv[None, :]
    cos, sin = jnp.cos(ang), jnp.sin(ang)
    zero = jnp.zeros_like(sin)
    rep = LANES // SWA_HEAD_DIM
    return (jnp.tile(jnp.concatenate([cos, cos], axis=1), (1, rep)),
            jnp.tile(jnp.concatenate([-sin, zero], axis=1), (1, rep)),
            jnp.tile(jnp.concatenate([zero, sin], axis=1), (1, rep)))


def kernel(x_prompt, x_sample, state_gla, cache_swa_k, cache_swa_v, norm_attn, w_in, w_gk2, b_gk2,
           gla_norm, attn_sinks, w_o, norm_ffn, w_gate, w_up, w_down, norm_final):
    batch, seq, _ = x_prompt.shape
    nbatch, nt, _ = x_sample.shape
    n_s = nbatch * nt

    splits = np.cumsum([GLA_QK, GLA_QK, GLA_WIDTH, GATE_RANK, GLA_WIDTH, SWA_Q, SWA_KV, SWA_KV])
    c_ga0, c_ga1 = int(splits[2]), int(splits[3])
    w_in_b = w_in.astype(BF16)
    w_in_p = jnp.concatenate(
        [w_in_b[:, :, :c_ga0], w_in_b[:, :, c_ga1:], w_in_b[:, :, c_ga0:c_ga1],
         jnp.zeros((DEPTH, D_MODEL, LANES - GATE_RANK), BF16)], axis=2)
    wg2_p = jnp.concatenate([w_gk2, jnp.zeros((DEPTH, LANES - GATE_RANK, GLA_QK), w_gk2.dtype)],
                            axis=1).astype(BF16)
    bg2 = b_gk2.reshape(DEPTH, 1, GLA_QK)
    nw_a = norm_attn.reshape(DEPTH, 1, D_MODEL)
    nw_f = norm_ffn.reshape(DEPTH, 1, D_MODEL)
    wo_b, wg_b, wu_b, wd_b = (w.astype(BF16) for w in (w_o, w_gate, w_up, w_down))
    nf = norm_final.reshape(1, D_MODEL)
    gn = gla_norm.reshape(DEPTH, 1, GLA_DV)

    tabs_p = _rope_tables(jnp.arange(seq))
    tabs_s = _rope_tables(jnp.tile(PAST_LEN + jnp.arange(nt), TOK_TILE // nt))

    state_r = state_gla.reshape(DEPTH, nbatch * GLA_HEADS, GLA_DK, GLA_DV)
    pos_minor = lambda c: c.transpose(0, 1, 3, 4, 2).reshape(DEPTH, nbatch, SWA_KV, WINDOW)
    ck_r, cv_r = pos_minor(cache_swa_k), pos_minor(cache_swa_v)

    xp = x_prompt.reshape(batch * seq, D_MODEL)
    xs = x_sample.reshape(n_s, D_MODEL)
    sp_l, kp_l, vp_l = [], [], []
    carried = [jnp.zeros(state_r.shape, F32), jnp.zeros(ck_r.shape, F32), jnp.zeros(cv_r.shape, F32)]
    tpad = ((0, 0), (0, SUBLANES - nt), (0, 0))
    for l in range(DEPTH):
        final = l == DEPTH - 1
        q, k, g_hi, g_lo, v, gate, sq, sk, sv = _in_proj(xp, nw_a, w_in_p, wg2_p, bg2, tabs_p, l)
        og, s_fin = _gla_prompt(q, k, g_hi, g_lo, v, batch, seq)
        xp = _ffn_swa(xp, og, gate, gn, attn_sinks[l], sq, sk, sv, wo_b, nw_f, wg_b, wu_b, wd_b, nf, l, final, seq)
        sp_l.append(s_fin)
        last = lambda a: a.reshape(batch, seq, SWA_KV)[:, seq - WINDOW:].reshape(
            batch, WINDOW, SWA_KV_HEADS, SWA_HEAD_DIM)
        kp_l.append(last(sk))
        vp_l.append(last(sv))

        q, k, g_hi, g_lo, v, gate, sq, sk, sv = _in_proj(xs, nw_a, w_in_p, wg2_p, bg2, tabs_s, l)
        g = g_hi.astype(F32) + g_lo.astype(F32)

        def bhtk(a, w_):
            return a.reshape(nbatch, nt, GLA_HEADS, w_).transpose(0, 2, 1, 3).reshape(nbatch * GLA_HEADS, nt, w_)

        q4, k4, g4 = (jnp.pad(bhtk(a, GLA_DK), tpad) for a in (q, k, g))
        v4 = jnp.pad(bhtk(v.astype(F32), GLA_DV), tpad)
        qk = sq.reshape(nbatch, nt, SWA_KV_HEADS, SWA_GROUP, SWA_HEAD_DIM).transpose(0, 2, 1, 3, 4)
        zq = jnp.zeros_like(qk[:, 0])
        qx = jnp.stack([jnp.concatenate([qk[:, 0], zq], axis=-1), jnp.concatenate([zq, qk[:, 1]], axis=-1)],
                       axis=1).reshape(nbatch, SWA_KV_HEADS * nt * SWA_GROUP, SWA_KV)
        sink_rows = jnp.broadcast_to(attn_sinks[l].reshape(SWA_KV_HEADS, 1, SWA_GROUP),
                                     (SWA_KV_HEADS, nt, SWA_GROUP)).reshape(1, -1, 1)
        og4, os4, *carried = _sample_mix(
            q4, k4, g4, v4, state_r, qx, sk.reshape(nbatch, nt, SWA_KV), sv.reshape(nbatch, nt, SWA_KV),
            ck_r, cv_r, sink_rows, carried, l, nbatch, nt)
        og = og4[:, :nt].reshape(nbatch, GLA_HEADS, nt, GLA_DV).transpose(0, 2, 1, 3).reshape(n_s, GLA_WIDTH)
        os5 = os4.reshape(nbatch, SWA_KV_HEADS, nt, SWA_GROUP, SWA_KV)
        osw = jnp.stack([os5[:, 0, :, :, :SWA_HEAD_DIM], os5[:, 1, :, :, SWA_HEAD_DIM:]], axis=2)
        osw = osw.reshape(n_s, SWA_Q)
        xs = _out_ffn(xs, og.astype(BF16), gate, gn, osw.astype(BF16), wo_b, nw_f, wg_b, wu_b, wd_b, nf, l, final)

    s_all, ck_all, cv_all = carried
    pos_major = lambda c: c.reshape(DEPTH, nbatch, SWA_KV_HEADS, SWA_HEAD_DIM, WINDOW).transpose(0, 1, 4, 2, 3)
    return (xp.reshape(batch, seq, D_MODEL), xs.reshape(nbatch, nt, D_MODEL),
            jnp.stack(sp_l), jnp.stack(kp_l), jnp.stack(vp_l),
            s_all.reshape(DEPTH, nbatch, GLA_HEADS, GLA_DK, GLA_DV), pos_major(ck_all), pos_major(cv_all))
```

```python
import functools

import jax
import jax.numpy as jnp
import numpy as np
from jax import lax
from jax.experimental import pallas as pl
from jax.experimental.pallas import tpu as pltpu

F32 = jnp.float32
BF16 = jnp.bfloat16

D_MODEL = 1024
DEPTH = 4
PAST_LEN = 8192
GLA_HEADS = 4
GLA_DV = 128
GLA_DK = 64
GLA_QK = GLA_HEADS * GLA_DK
GLA_WIDTH = GLA_HEADS * GLA_DV
GATE_RANK = 16
GATE_NORMALIZER = 16.0
SWA_HEAD_DIM = 64
SWA_HEADS = 8
SWA_KV_HEADS = 2
SWA_GROUP = 4
SWA_Q = SWA_HEADS * SWA_HEAD_DIM
SWA_KV = SWA_KV_HEADS * SWA_HEAD_DIM
WINDOW = 128
ROPE_THETA = 10000.0
D_FF = 2816
NORM_EPS = 1e-6

LANES = 128
SUBLANES = 8
VMEM_LIMIT = 56 * 1024 * 1024

C_Q, C_K, C_V, C_GG, C_SQ, C_SK, C_SV, C_GA = 0, 256, 512, 1024, 1536, 2048, 2176, 2304
IN_COLS_P = C_GA + LANES

TOK_TILE = 512
GLA_CHUNK = 64
GLA_SUB = 8
GLA_TILE = 512
GLA_UNROLL = 8
FF_CHUNK = D_FF
FF_SPLITS = (0, 768, 1536, 2304, D_FF)
SWA_ROWS = 32
SAMPLE_BT = 16
NEG = -1e30
LOG2E = 1.4426950408889634


def _rms(x, w):
    ms = jnp.mean(x * x, axis=-1, keepdims=True)
    return x * lax.rsqrt(ms + NORM_EPS) * w


def _sigmoid(x):
    return 1.0 / (1.0 + jnp.exp(-x))


def _dot(a, b):
    return jnp.dot(a, b, preferred_element_type=F32)


def _dot_nt(a, b):
    return lax.dot_general(a, b, (((1,), (1,)), ((), ())), preferred_element_type=F32)


def _in_proj_kernel(x_ref, nw_ref, w_ref, wg2_ref, bg2_ref, cos_ref, sa_ref, sb_ref,
                    q_ref, k_ref, ghi_ref, glo_ref, v_ref, gate_ref, sq_ref, sk_ref, sv_ref):
    h = _rms(x_ref[...], nw_ref[...]).astype(BF16)
    z = _dot_nt(h, w_ref[...])

    def proj(c0, c1):
        return z[:, c0:c1]

    q_ref[...] = proj(C_Q, C_K) * (GLA_DK ** -0.5)
    k_ref[...] = proj(C_K, C_V)
    v_ref[...] = proj(C_V, C_GG).astype(BF16)
    gg = proj(C_GG, C_SQ)
    gate_ref[...] = gg * _sigmoid(gg)
    ga = proj(C_GA, IN_COLS_P).astype(BF16)
    gx = _dot(ga, wg2_ref[...]) + bg2_ref[...]
    g2 = (jnp.minimum(gx, 0.0) - jnp.log1p(jnp.exp(-jnp.abs(gx)))) * (LOG2E / GATE_NORMALIZER)
    g_hi = g2.astype(BF16)
    ghi_ref[...] = g_hi
    glo_ref[...] = (g2 - g_hi.astype(F32)).astype(BF16)

    cos, sa, sb = cos_ref[...], sa_ref[...], sb_ref[...]

    def rope(z):
        return z * cos + pltpu.roll(z, LANES - 32, 1) * sa + pltpu.roll(z, 32, 1) * sb

    for j in range(SWA_Q // (2 * LANES)):
        zq = proj(C_SQ + 2 * j * LANES, C_SQ + 2 * (j + 1) * LANES)
        for jj in range(2):
            sq_ref[:, (2 * j + jj) * LANES:(2 * j + jj + 1) * LANES] = (
                rope(zq[:, jj * LANES:(jj + 1) * LANES]) * (SWA_HEAD_DIM ** -0.5 * LOG2E)).astype(BF16)
    zkv = proj(C_SK, C_GA)
    sk_ref[...] = rope(zkv[:, :SWA_KV])
    sv_ref[...] = zkv[:, SWA_KV:]


def _in_proj(x, nw, w, wg2, bg2, tabs, layer):
    n = x.shape[0]
    nt = n // TOK_TILE
    cos, sa, sb = tabs
    ntab = cos.shape[0] // TOK_TILE
    row = lambda w_: pl.BlockSpec((TOK_TILE, w_), lambda i: (i, 0))
    tab = pl.BlockSpec((TOK_TILE, LANES), lambda i: (i % ntab, 0))
    const = lambda shp: pl.BlockSpec((None,) + shp, lambda i: (layer, 0, 0), pipeline_mode=pl.Buffered(1))
    outs = [(GLA_QK, F32), (GLA_QK, F32), (GLA_QK, BF16), (GLA_QK, BF16), (GLA_WIDTH, BF16), (GLA_WIDTH, F32),
            (SWA_Q, BF16), (SWA_KV, F32), (SWA_KV, F32)]
    return pl.pallas_call(
        _in_proj_kernel,
        grid=(nt,),
        in_specs=[row(D_MODEL), const((1, D_MODEL)), const((IN_COLS_P, D_MODEL)),
                  const((LANES, GLA_QK)), const((1, GLA_QK)), tab, tab, tab],
        out_specs=[row(w_) for w_, _ in outs],
        out_shape=[jax.ShapeDtypeStruct((n, w_), dt) for w_, dt in outs],
        compiler_params=pltpu.CompilerParams(dimension_semantics=("arbitrary",),
                                             vmem_limit_bytes=VMEM_LIMIT),
        name="in_proj",
    )(x, nw, w, wg2, bg2, cos, sa, sb)


def _gla_kernel(q_ref, k_ref, ghi_ref, glo_ref, v_ref, bias_ref, o_ref, sout_ref, s_ref):
    ti = pl.program_id(1)
    C, SB = GLA_CHUNK, GLA_SUB
    NB = C // SB
    NP = GLA_HEADS // 2

    @pl.when(ti == 0)
    def _():
        s_ref[...] = jnp.zeros_like(s_ref)

    ltri = (lax.broadcasted_iota(jnp.int32, (C, 2 * C), 0) >= lax.broadcasted_iota(jnp.int32, (C, 2 * C), 1) % C
            ).astype(F32).astype(BF16)
    lane = lax.broadcasted_iota(jnp.int32, (1, LANES), 1)
    mask_a = (lane < GLA_DK).astype(F32)
    mask_b = 1.0 - mask_a
    col8 = lax.broadcasted_iota(jnp.int32, (SB, LANES), 1) % C
    colblk, colmod = col8 // SB, col8 % SB
    zk = jnp.zeros((GLA_DK, GLA_DV), BF16)
    off1 = [sum(C - SB * (jj + 1) for jj in range(j)) for j in range(NB - 1)]

    def sub_bcast(x, j):
        x3 = x.reshape(NB, SB, LANES)
        return jnp.broadcast_to(x3[:, j:j + 1, :], (NB, SB, LANES)).reshape(C, LANES)

    def heads(x):
        return jnp.concatenate([x * mask_a, x * mask_b], axis=0).astype(BF16)

    def body(it, carry):
        rows = [pl.ds(pl.multiple_of((it * GLA_UNROLL + cc) * C, C), C) for cc in range(GLA_UNROLL)]
        units = [(cc, p) for cc in range(GLA_UNROLL) for p in range(NP)]
        kl = [slice(p * LANES, (p + 1) * LANES) for p in range(NP)]
        vl = [slice(h * GLA_DV, (h + 1) * GLA_DV) for h in range(GLA_HEADS)]

        b_all = [_dot(ltri, jnp.concatenate([ghi_ref[rows[cc], :], glo_ref[rows[cc], :]], axis=0))
                 for cc in range(GLA_UNROLL)]

        qe, outs, dec, upd = {}, {}, {}, {}
        for cc, p in units:
            q, k, b = q_ref[rows[cc], kl[p]], k_ref[rows[cc], kl[p]], b_all[cc][:, kl[p]]
            b_last = b[C - 1:C, :]
            k1 = k * jnp.exp2(sub_bcast(b, SB - 1) - b)
            lhs1 = jnp.concatenate(
                [q[SB * (j + 1):, :] * jnp.exp2(b[SB * (j + 1):, :] - b[SB * j + SB - 1:SB * j + SB, :])
                 for j in range(NB - 1)], axis=0)
            lhs2 = jnp.concatenate(
                [q * jnp.exp2((b - sub_bcast(b, jj)) + jnp.tile(bias_ref[jj], (NB, 1))) for jj in range(SB)],
                axis=0)
            qe[cc, p] = (q * jnp.exp2(b)).astype(BF16)
            outs[cc, p] = (_dot_nt(lhs1.astype(BF16), heads(k1)), _dot_nt(lhs2.astype(BF16), heads(k)))
            kd = k * jnp.exp2(b_last - b)
            mt = jnp.concatenate([kd, jnp.broadcast_to(b_last, (C, LANES))], axis=0).T
            kdt = mt[:, :C].astype(BF16)
            dec[cc, p] = jnp.exp2(mt[:, C:C + 1])
            upd[cc, p] = (_dot(kdt[:GLA_DK], v_ref[rows[cc], vl[2 * p]]),
                          _dot(kdt[GLA_DK:], v_ref[rows[cc], vl[2 * p + 1]]))

        lo = {}
        for u in units:
            out1, out2 = outs[u]
            a_rows = []
            for i in range(NB):
                d = out2[(SB - 1) * C + SB * i:(SB - 1) * C + SB * (i + 1), :]
                for jj in range(SB - 2, -1, -1):
                    d = jnp.where(colmod == jj, out2[jj * C + SB * i:jj * C + SB * (i + 1), :], d)
                acc = jnp.where(colblk == i, d, 0.0)
                for j in range(i):
                    r0 = off1[j] + SB * (i - j - 1)
                    acc = jnp.where(colblk == j, out1[r0:r0 + SB, :], acc)
                a_rows.append(acc)
            a = jnp.concatenate(a_rows, axis=0)
            lo[u] = jnp.concatenate([qe[u], a.astype(BF16)], axis=1)

        state = [s_ref[h] for h in range(GLA_HEADS)]
        o_all = {}
        for cc in range(GLA_UNROLL):
            for p in range(NP):
                ha, hb = 2 * p, 2 * p + 1
                v_a, v_b = v_ref[rows[cc], vl[ha]], v_ref[rows[cc], vl[hb]]
                o_all[cc, ha] = _dot(lo[cc, p], jnp.concatenate([state[ha].astype(BF16), zk, v_a, zk], axis=0))
                o_all[cc, hb] = _dot(lo[cc, p], jnp.concatenate([zk, state[hb].astype(BF16), zk, v_b], axis=0))
                state[ha] = dec[cc, p][:GLA_DK] * state[ha] + upd[cc, p][0]
                state[hb] = dec[cc, p][GLA_DK:] * state[hb] + upd[cc, p][1]
        for h in range(GLA_HEADS):
            s_ref[h] = state[h]

        for cc in range(GLA_UNROLL):
            for h in range(GLA_HEADS):
                o_ref[rows[cc], vl[h]] = o_all[cc, h].astype(BF16)
        return carry

    lax.fori_loop(0, GLA_TILE // (C * GLA_UNROLL), body, 0)

    @pl.when(ti == pl.num_programs(1) - 1)
    def _():
        sout_ref[0] = s_ref[...]


def _gla_prompt(q, k, g_hi, g_lo, v, batch, seq):
    nt = seq // GLA_TILE
    blk = lambda w_: pl.BlockSpec((GLA_TILE, w_), lambda b, t: (b * nt + t, 0))
    r = np.arange(GLA_SUB)
    bias = np.where(r[None, :, None] >= r[:, None, None], 0.0, NEG) * np.ones((1, 1, LANES))
    return pl.pallas_call(
        _gla_kernel,
        grid=(batch, nt),
        in_specs=[blk(GLA_QK), blk(GLA_QK), blk(GLA_QK), blk(GLA_QK), blk(GLA_WIDTH),
                  pl.BlockSpec((GLA_SUB, GLA_SUB, LANES), lambda b, t: (0, 0, 0))],
        out_specs=[blk(GLA_WIDTH),
                   pl.BlockSpec((1, GLA_HEADS, GLA_DK, GLA_DV), lambda b, t: (b, 0, 0, 0))],
        out_shape=[jax.ShapeDtypeStruct((batch * seq, GLA_WIDTH), BF16),
                   jax.ShapeDtypeStruct((batch, GLA_HEADS, GLA_DK, GLA_DV), F32)],
        scratch_shapes=[pltpu.VMEM((GLA_HEADS, GLA_DK, GLA_DV), F32)],
        compiler_params=pltpu.CompilerParams(
            dimension_semantics=("arbitrary", "arbitrary"), vmem_limit_bytes=VMEM_LIMIT),
        name="gla_prompt",
    )(q, k, g_hi, g_lo, v, jnp.asarray(bias, F32))


def _out_ffn_kernel(x_ref, og_ref, gate_ref, gn_ref, os_ref, wo_ref, nw_ref, wg_ref, wu_ref, wd_ref, nf_ref,
                    y_ref, *, final):
    gn = gn_ref[...]
    og = jnp.concatenate([_rms(og_ref[:, h * GLA_DV:(h + 1) * GLA_DV].astype(F32), gn)
                          for h in range(GLA_HEADS)], axis=1) * gate_ref[...]
    x = (x_ref[...] + _dot(og.astype(BF16), wo_ref[:GLA_WIDTH, :]) + _dot(os_ref[...], wo_ref[GLA_WIDTH:, :]))
    h = _rms(x, nw_ref[...]).astype(BF16)
    acc = x
    for c0 in range(0, D_FF, FF_CHUNK):
        a = _dot(h, wg_ref[:, c0:c0 + FF_CHUNK])
        u = _dot(h, wu_ref[:, c0:c0 + FF_CHUNK])
        acc = acc + _dot((a * _sigmoid(a) * u).astype(BF16), wd_ref[c0:c0 + FF_CHUNK, :])
    if final:
        acc = _rms(acc, nf_ref[...])
    y_ref[...] = acc


def _out_ffn(x, og, gate, gn, osw, wo, nw, wg, wu, wd, nf, layer, final):
    n = x.shape[0]
    row = lambda w_: pl.BlockSpec((TOK_TILE, w_), lambda i: (i, 0))
    const = lambda shp: pl.BlockSpec((None,) + shp, lambda i: (layer, 0, 0), pipeline_mode=pl.Buffered(1))
    return pl.pallas_call(
        functools.partial(_out_ffn_kernel, final=final),
        grid=(n // TOK_TILE,),
        in_specs=[row(D_MODEL), row(GLA_WIDTH), row(GLA_WIDTH), const((1, GLA_DV)), row(SWA_Q),
                  const((D_MODEL, D_MODEL)), const((1, D_MODEL)),
                  const((D_MODEL, D_FF)), const((D_MODEL, D_FF)), const((D_FF, D_MODEL)),
                  pl.BlockSpec((1, D_MODEL), lambda i: (0, 0))],
        out_specs=row(D_MODEL),
        out_shape=jax.ShapeDtypeStruct((n, D_MODEL), F32),
        compiler_params=pltpu.CompilerParams(dimension_semantics=("arbitrary",),
                                             vmem_limit_bytes=VMEM_LIMIT),
        name="out_ffn",
    )(x, og, gate, gn, osw, wo, nw, wg, wu, wd, nf)


def _swa_setup(kp_ref, kc_ref, vp_ref, vc_ref, ke_scr, ve_scr):
    kk = jnp.concatenate([kp_ref[...], kc_ref[...]], axis=0)
    vv = jnp.concatenate([vp_ref[...], vc_ref[...]], axis=0)
    lo = lax.broadcasted_iota(jnp.int32, kk.shape, 1) < SWA_HEAD_DIM
    for x, scr in ((kk, ke_scr), (vv, ve_scr)):
        x_sw = pltpu.roll(x, SWA_HEAD_DIM, 1)
        scr[0, 0] = jnp.where(lo, x, 0.0).astype(BF16)
        scr[0, 1] = jnp.where(lo, 0.0, x_sw).astype(BF16)
        scr[1, 0] = jnp.where(lo, x_sw, 0.0).astype(BF16)
        scr[1, 1] = jnp.where(lo, 0.0, x).astype(BF16)


def _swa_scores(j, q_ref, ke_scr, s_scr):
    W = WINDOW
    for p in range(SWA_HEADS // 2):
        kv = p // (SWA_GROUP // 2)
        q = q_ref[j * W:(j + 1) * W, p * LANES:(p + 1) * LANES]
        for hh in range(2):
            s_scr[p, :, hh * 2 * W:(hh + 1) * 2 * W] = _dot_nt(q, ke_scr[kv, hh, j * W:(j + 2) * W, :])


def _swa_softmax(sink_ref, bias_blk, s_scr, p_scr, r_scr):
    W = WINDOW
    lane_q = lax.broadcasted_iota(jnp.int32, (SWA_ROWS, LANES), 1) < SWA_HEAD_DIM
    for p in range(SWA_HEADS // 2):
        for r0 in range(0, W, SWA_ROWS):
            rs = slice(r0, r0 + SWA_ROWS)
            rden = []
            for hh in range(2):
                cs = slice(hh * 2 * W, (hh + 1) * 2 * W)
                sink = sink_ref[2 * p + hh] * LOG2E
                sh = s_scr[p, rs, cs] + bias_blk[rs, :]
                m = jnp.maximum(jnp.max(sh, axis=-1, keepdims=True), sink)
                e = jnp.exp2(sh - m)
                den = jnp.sum(e, axis=-1, keepdims=True) + jnp.exp2(sink - m)
                p_scr[p, rs, cs] = e.astype(BF16)
                rden.append(1.0 / den)
            r_scr[p, rs, :] = jnp.where(lane_q, rden[0], rden[1])


def _swa_values(j, p_scr, ve_scr, r_scr, o_ref):
    W = WINDOW
    for p in range(SWA_HEADS // 2):
        kv = p // (SWA_GROUP // 2)
        o = (_dot(p_scr[p, :, :2 * W], ve_scr[kv, 0, j * W:(j + 2) * W, :])
             + _dot(p_scr[p, :, 2 * W:], ve_scr[kv, 1, j * W:(j + 2) * W, :])) * r_scr[p]
        o_ref[j * W:(j + 1) * W, p * LANES:(p + 1) * LANES] = o.astype(BF16)


def _ffn_swa_kernel(sink_ref, x_ref, og_ref, gate_ref, gn_ref, wo_ref, nw_ref, wg_ref, wu_ref, wd_ref, nf_ref,
                    osw0_ref, q_ref, kp_ref, kc_ref, vp_ref, vc_ref, bias_ref, y_ref,
                    osw_scr, ke_scr, ve_scr, s_scr, p_scr, r_scr, *, final, tiles_per_seq, n_tiles):
    s = pl.program_id(0)
    nblk = TOK_TILE // WINDOW

    @pl.when(s == 0)
    def _():
        osw_scr[0] = osw0_ref[...]

    _swa_setup(kp_ref, kc_ref, vp_ref, vc_ref, ke_scr, ve_scr)
    t_seq = jnp.minimum(s + 1, n_tiles - 1) % tiles_per_seq
    osw_cur = osw_scr.at[(s + 1) % 2]

    gn = gn_ref[...]
    og = jnp.concatenate([_rms(og_ref[:, h * GLA_DV:(h + 1) * GLA_DV].astype(F32), gn)
                          for h in range(GLA_HEADS)], axis=1) * gate_ref[...]
    x = (x_ref[...] + _dot(og.astype(BF16), wo_ref[:GLA_WIDTH, :])
         + _dot(osw_scr[s % 2], wo_ref[GLA_WIDTH:, :]))
    h = _rms(x, nw_ref[...]).astype(BF16)

    t_parts = []
    for j in range(nblk):
        c0, c1 = FF_SPLITS[j], FF_SPLITS[j + 1]
        _swa_scores(j, q_ref, ke_scr, s_scr)
        a = _dot(h, wg_ref[:, c0:c1])
        u = _dot(h, wu_ref[:, c0:c1])
        t_parts.append((a * _sigmoid(a) * u).astype(BF16))
        _swa_softmax(sink_ref, bias_ref.at[jnp.minimum(t_seq * nblk + j, 1)], s_scr, p_scr, r_scr)
        _swa_values(j, p_scr, ve_scr, r_scr, osw_cur)
    acc = x + _dot(jnp.concatenate(t_parts, axis=1), wd_ref[...])
    if final:
        acc = _rms(acc, nf_ref[...])
    y_ref[...] = acc


def _swa_first_kernel(sink_ref, q_ref, kp_ref, kc_ref, vp_ref, vc_ref, bias_ref, o_ref,
                      ke_scr, ve_scr, s_scr, p_scr, r_scr):
    _swa_setup(kp_ref, kc_ref, vp_ref, vc_ref, ke_scr, ve_scr)
    for j in range(TOK_TILE // WINDOW):
        _swa_scores(j, q_ref, ke_scr, s_scr)
        _swa_softmax(sink_ref, bias_ref.at[min(j, 1)], s_scr, p_scr, r_scr)
        _swa_values(j, p_scr, ve_scr, r_scr, o_ref)


def _ffn_swa(x, og, gate, gn, sinks, sq, sk, sv, wo, nw, wg, wu, wd, nf, layer, final, seq):
    n = x.shape[0]
    W = WINDOW
    n_tiles = n // TOK_TILE
    tiles_per_seq = seq // TOK_TILE
    nblk = TOK_TILE // W
    ffn_i = lambda s: (s, 0)
    swa_i = lambda s: (jnp.minimum(s + 1, n_tiles - 1), 0)

    def prev_i(s):
        t = jnp.minimum(s + 1, n_tiles - 1)
        return (t - t % tiles_per_seq) * nblk + jnp.maximum((t % tiles_per_seq) * nblk - 1, 0), 0

    row = lambda w_, im: pl.BlockSpec((TOK_TILE, w_), im)
    const = lambda shp: pl.BlockSpec((None,) + shp, lambda s: (layer, 0, 0), pipeline_mode=pl.Buffered(1))
    first = lambda s: (0, 0)
    qi, kj = np.arange(W)[:, None], np.arange(2 * W)[None, :]
    band = (kj > qi) & (kj <= qi + W)
    bias = jnp.asarray(np.stack([np.where(band & (kj >= W), 0.0, NEG), np.where(band, 0.0, NEG)]), F32)
    bias_spec = pl.BlockSpec((2, W, 2 * W), lambda s: (0, 0, 0))
    ext_scr = pltpu.VMEM((SWA_KV_HEADS, 2, W + TOK_TILE, LANES), BF16)
    npair = SWA_HEADS // 2
    swa_scr = [ext_scr, ext_scr, pltpu.VMEM((npair, W, 4 * W), F32), pltpu.VMEM((npair, W, 4 * W), BF16),
               pltpu.VMEM((npair, W, LANES), F32)]
    osw0 = pl.pallas_call(
        _swa_first_kernel,
        grid=(1,),
        in_specs=[pl.BlockSpec(memory_space=pltpu.SMEM), row(SWA_Q, first),
                  pl.BlockSpec((W, SWA_KV), first), row(SWA_KV, first),
                  pl.BlockSpec((W, SWA_KV), first), row(SWA_KV, first), bias_spec],
        out_specs=row(SWA_Q, first),
        out_shape=jax.ShapeDtypeStruct((TOK_TILE, SWA_Q), BF16),
        scratch_shapes=swa_scr,
        compiler_params=pltpu.CompilerParams(dimension_semantics=("arbitrary",), vmem_limit_bytes=VMEM_LIMIT),
        name="swa_first",
    )(sinks, sq, sk, sk, sv, sv, bias)
    return pl.pallas_call(
        functools.partial(_ffn_swa_kernel, final=final, tiles_per_seq=tiles_per_seq, n_tiles=n_tiles),
        grid=(n_tiles,),
        in_specs=[pl.BlockSpec(memory_space=pltpu.SMEM),
                  row(D_MODEL, ffn_i), row(GLA_WIDTH, ffn_i), row(GLA_WIDTH, ffn_i), const((1, GLA_DV)),
                  const((D_MODEL, D_MODEL)), const((1, D_MODEL)),
                  const((D_MODEL, D_FF)), const((D_MODEL, D_FF)), const((D_FF, D_MODEL)),
                  pl.BlockSpec((1, D_MODEL), lambda s: (0, 0)),
                  row(SWA_Q, first), row(SWA_Q, swa_i),
                  pl.BlockSpec((W, SWA_KV), prev_i), row(SWA_KV, swa_i),
                  pl.BlockSpec((W, SWA_KV), prev_i), row(SWA_KV, swa_i), bias_spec],
        out_specs=row(D_MODEL, ffn_i),
        out_shape=jax.ShapeDtypeStruct((n, D_MODEL), F32),
        scratch_shapes=[pltpu.VMEM((2, TOK_TILE, SWA_Q), BF16)] + swa_scr,
        compiler_params=pltpu.CompilerParams(dimension_semantics=("arbitrary",),
                                             vmem_limit_bytes=VMEM_LIMIT),
        name="ffn_swa",
    )(sinks, x, og, gate, gn, wo, nw, wg, wu, wd, nf, osw0, sq, sk, sk, sv, sv, bias)


def _sample_kernel(qkg_ref, v_ref, s0_ref,
                   qx_ref, kn_ref, vn_ref, ck_ref, cv_ref, sink_ref,
                   s_prev_ref, ck_prev_ref, cv_prev_ref,
                   og_ref, os_ref, sn_ref, cko_ref, cvo_ref, *, nt):
    del s_prev_ref, ck_prev_ref, cv_prev_ref
    TP = SUBLANES
    q, k, g, v = qkg_ref[0], qkg_ref[1], qkg_ref[2], v_ref[...]
    s0 = s0_ref[...]
    tt = lax.broadcasted_iota(jnp.int32, (1, TP, 1), 1)
    b = jnp.zeros_like(g)
    for s in range(nt):
        b = b + jnp.where(tt >= s, g[:, s:s + 1, :], 0.0)
    o = jnp.einsum('gtk,gkv->gtv', (q * jnp.exp2(b)).astype(BF16), s0.astype(BF16),
                   preferred_element_type=F32)
    for s in range(nt):
        dec = jnp.exp2(jnp.where(tt >= s, b - b[:, s:s + 1, :], NEG))
        a = jnp.sum(q * k[:, s:s + 1, :] * dec, axis=-1, keepdims=True)
        o = o + a * v[:, s:s + 1, :]
    og_ref[...] = o

    b_last = b[:, nt - 1:nt, :]
    d = jnp.exp2(b_last)
    d_hi = d.astype(BF16).astype(F32)
    x = jnp.where(tt < nt, k * jnp.exp2(b_last - b), jnp.where(tt == nt, d_hi, jnp.where(tt == nt + 1, d - d_hi, 0.0)))
    ones = jnp.where((tt == nt) | (tt == nt + 1), 1.0, 0.0)
    y = jnp.concatenate([jnp.where(tt < nt, v, 0.0), jnp.broadcast_to(ones, v.shape)], axis=2)
    r = lax.dot_general(x.astype(BF16), y.astype(BF16), (((1,), (1,)), ((0,), (0,))),
                        preferred_element_type=F32)
    sn_ref[...] = r[:, :, GLA_DV:] * s0 + r[:, :, :GLA_DV]

    W = WINDOW
    qx = qx_ref[...]
    ck, cv = ck_ref[...], cv_ref[...]
    kn, vn = kn_ref[...], vn_ref[...]
    sink = sink_ref[...] * LOG2E
    nrow = SWA_KV_HEADS * nt * SWA_GROUP
    r = lax.broadcasted_iota(jnp.int32, (1, nrow, W), 1)
    tq = (r % (nt * SWA_GROUP)) // SWA_GROUP
    j = lax.broadcasted_iota(jnp.int32, (1, nrow, W), 2)
    sc = jnp.einsum('bqd,bds->bqs', qx, ck.astype(BF16), preferred_element_type=F32)
    sc = jnp.where(j > tq, sc, NEG)
    tq1 = tq[:, :, 0:1]
    qf = qx.astype(F32)
    sn = [jnp.where(tq1 >= s, jnp.sum(qf * kn[:, s:s + 1, :], axis=-1, keepdims=True), NEG)
          for s in range(nt)]
    m = jnp.maximum(jnp.max(sc, axis=-1, keepdims=True), sink)
    for s in range(nt):
        m = jnp.maximum(m, sn[s])
    pc = jnp.exp2(sc - m)
    pn = [jnp.exp2(sn[s] - m) for s in range(nt)]
    den = jnp.sum(pc, axis=-1, keepdims=True) + jnp.exp2(sink - m)
    for s in range(nt):
        den = den + pn[s]
    o = jnp.einsum('bqs,bds->bqd', pc.astype(BF16), cv.astype(BF16), preferred_element_type=F32)
    for s in range(nt):
        o = o + pn[s] * vn[:, s:s + 1, :]
    os_ref[...] = o * (1.0 / den)

    wl = lax.broadcasted_iota(jnp.int32, (1, 1, W), 2)
    pr = lax.broadcasted_iota(jnp.int32, (1, 2 * nt, W), 1)
    pw = lax.broadcasted_iota(jnp.int32, (1, 2 * nt, W), 2)
    place = jnp.broadcast_to(jnp.where(pw == W - nt + pr % nt, 1.0, 0.0), (ck.shape[0], 2 * nt, W)).astype(BF16)
    for src, new, dst in ((ck, kn, cko_ref), (cv, vn, cvo_ref)):
        hi = new.astype(BF16).astype(F32)
        x = jnp.concatenate([hi, new - hi], axis=1).astype(BF16)
        placed = lax.dot_general(x, place, (((1,), (1,)), ((0,), (0,))), preferred_element_type=F32)
        dst[...] = jnp.where(wl >= W - nt, placed, pltpu.roll(src, W - nt, 2))


def _sample_mix(qkg4, v4, state, qx, kn, vn, ck, cv, sink_rows, carried, layer, nbatch, nt):
    bt = SAMPLE_BT
    G = bt * GLA_HEADS
    TP = SUBLANES
    nrow = SWA_KV_HEADS * nt * SWA_GROUP
    b3 = lambda a, c: pl.BlockSpec((G, a, c), lambda i: (i, 0, 0))
    s3 = lambda a, c: pl.BlockSpec((bt, a, c), lambda i: (i, 0, 0))
    lay_g = pl.BlockSpec((None, G, GLA_DK, GLA_DV), lambda i: (layer, i, 0, 0))
    lay_c = pl.BlockSpec((None, bt, SWA_KV, WINDOW), lambda i: (layer, i, 0, 0))
    ng = nbatch * GLA_HEADS
    n_in = 9
    return pl.pallas_call(
        functools.partial(_sample_kernel, nt=nt),
        grid=(nbatch // bt,),
        in_specs=[pl.BlockSpec((3, G, TP, GLA_DK), lambda i: (0, i, 0, 0)), b3(TP, GLA_DV), lay_g,
                  s3(nrow, SWA_KV), s3(nt, SWA_KV), s3(nt, SWA_KV), lay_c, lay_c,
                  pl.BlockSpec((1, nrow, 1), lambda i: (0, 0, 0))]
                 + [pl.BlockSpec(memory_space=pl.ANY)] * 3,
        out_specs=[b3(TP, GLA_DV), s3(nrow, SWA_KV), lay_g, lay_c, lay_c],
        out_shape=[jax.ShapeDtypeStruct((ng, TP, GLA_DV), F32),
                   jax.ShapeDtypeStruct((nbatch, nrow, SWA_KV), F32),
                   jax.ShapeDtypeStruct((DEPTH, ng, GLA_DK, GLA_DV), F32),
                   jax.ShapeDtypeStruct((DEPTH, nbatch, SWA_KV, WINDOW), F32),
                   jax.ShapeDtypeStruct((DEPTH, nbatch, SWA_KV, WINDOW), F32)],
        input_output_aliases={n_in + j: 2 + j for j in range(3)},
        compiler_params=pltpu.CompilerParams(dimension_semantics=("arbitrary",),
                                             vmem_limit_bytes=VMEM_LIMIT),
        name="sample_mix",
    )(qkg4, v4, state, qx, kn, vn, ck, cv, sink_rows, *carried)


def _rope_tables(pos):
    half = SWA_HEAD_DIM // 2
    inv = ROPE_THETA ** (-jnp.arange(half, dtype=F32) / half)
    ang = pos.astype(F32)[:, None] * inv[None, :]
    cos, sin = jnp.cos(ang), jnp.sin(ang)
    zero = jnp.zeros_like(sin)
    rep = LANES // SWA_HEAD_DIM
    return (jnp.tile(jnp.concatenate([cos, cos], axis=1), (1, rep)),
            jnp.tile(jnp.concatenate([-sin, zero], axis=1), (1, rep)),
            jnp.tile(jnp.concatenate([zero, sin], axis=1), (1, rep)))


def kernel(x_prompt, x_sample, state_gla, cache_swa_k, cache_swa_v, norm_attn, w_in, w_gk2, b_gk2,
           gla_norm, attn_sinks, w_o, norm_ffn, w_gate, w_up, w_down, norm_final):
    batch, seq, _ = x_prompt.shape
    nbatch, nt, _ = x_sample.shape
    n_s = nbatch * nt

    c_ga0, c_ga1 = C_GG, C_GG + GATE_RANK
    w_in_t = jnp.swapaxes(w_in, 1, 2).astype(BF16)
    w_in_p = jnp.concatenate(
        [w_in_t[:, :c_ga0], w_in_t[:, c_ga1:], w_in_t[:, c_ga0:c_ga1],
         jnp.zeros((DEPTH, LANES - GATE_RANK, D_MODEL), BF16)], axis=1)
    wg2_p = jnp.concatenate([w_gk2, jnp.zeros((DEPTH, LANES - GATE_RANK, GLA_QK), w_gk2.dtype)],
                            axis=1).astype(BF16)
    bg2 = b_gk2.reshape(DEPTH, 1, GLA_QK)
    nw_a = norm_attn.reshape(DEPTH, 1, D_MODEL)
    nw_f = norm_ffn.reshape(DEPTH, 1, D_MODEL)
    wo_b, wg_b, wu_b, wd_b = (w.astype(BF16) for w in (w_o, w_gate, w_up, w_down))
    nf = norm_final.reshape(1, D_MODEL)
    gn = gla_norm.reshape(DEPTH, 1, GLA_DV)

    tabs_p = _rope_tables(jnp.arange(seq))
    tabs_s = _rope_tables(jnp.tile(PAST_LEN + jnp.arange(nt), TOK_TILE // nt))

    state_r = state_gla.reshape(DEPTH, nbatch * GLA_HEADS, GLA_DK, GLA_DV)
    pos_minor = lambda c: c.transpose(0, 1, 3, 4, 2).reshape(DEPTH, nbatch, SWA_KV, WINDOW)
    ck_r, cv_r = pos_minor(cache_swa_k), pos_minor(cache_swa_v)

    xp = x_prompt.reshape(batch * seq, D_MODEL)
    xs = x_sample.reshape(n_s, D_MODEL)
    sp_l, kp_l, vp_l = [], [], []
    carried = [jnp.zeros(state_r.shape, F32), jnp.zeros(ck_r.shape, F32), jnp.zeros(cv_r.shape, F32)]
    tpad = ((0, 0), (0, SUBLANES - nt), (0, 0))
    for l in range(DEPTH):
        final = l == DEPTH - 1
        q, k, g_hi, g_lo, v, gate, sq, sk, sv = _in_proj(xp, nw_a, w_in_p, wg2_p, bg2, tabs_p, l)
        og, s_fin = _gla_prompt(q, k, g_hi, g_lo, v, batch, seq)
        xp = _ffn_swa(xp, og, gate, gn, attn_sinks[l], sq, sk, sv, wo_b, nw_f, wg_b, wu_b, wd_b, nf, l, final, seq)
        sp_l.append(s_fin)
        last = lambda a: a.reshape(batch, seq, SWA_KV)[:, seq - WINDOW:].reshape(
            batch, WINDOW, SWA_KV_HEADS, SWA_HEAD_DIM)
        kp_l.append(last(sk))
        vp_l.append(last(sv))

        q, k, g_hi, g_lo, v, gate, sq, sk, sv = _in_proj(xs, nw_a, w_in_p, wg2_p, bg2, tabs_s, l)
        g = g_hi.astype(F32) + g_lo.astype(F32)

        def bhtk(a, w_):
            return a.reshape(nbatch, nt, GLA_HEADS, w_).transpose(0, 2, 1, 3).reshape(nbatch * GLA_HEADS, nt, w_)

        qkg4 = jnp.pad(jnp.stack([bhtk(a, GLA_DK) for a in (q, k, g)]), ((0, 0),) + tpad)
        v4 = jnp.pad(bhtk(v.astype(F32), GLA_DV), tpad)
        qk = sq.reshape(nbatch, nt, SWA_KV_HEADS, SWA_GROUP, SWA_HEAD_DIM).transpose(0, 2, 1, 3, 4)
        zq = jnp.zeros_like(qk[:, 0])
        qx = jnp.stack([jnp.concatenate([qk[:, 0], zq], axis=-1), jnp.concatenate([zq, qk[:, 1]], axis=-1)],
                       axis=1).reshape(nbatch, SWA_KV_HEADS * nt * SWA_GROUP, SWA_KV)
        sink_rows = jnp.broadcast_to(attn_sinks[l].reshape(SWA_KV_HEADS, 1, SWA_GROUP),
                                     (SWA_KV_HEADS, nt, SWA_GROUP)).reshape(1, -1, 1)
        og4, os4, *carried = _sample_mix(
            qkg4, v4, state_r, qx, sk.reshape(nbatch, nt, SWA_KV), sv.reshape(nbatch, nt, SWA_KV),
            ck_r, cv_r, sink_rows, carried, l, nbatch, nt)
        og = og4[:, :nt].reshape(nbatch, GLA_HEADS, nt, GLA_DV).transpose(0, 2, 1, 3).reshape(n_s, GLA_WIDTH)
        os5 = os4.reshape(nbatch, SWA_KV_HEADS, nt, SWA_GROUP, SWA_KV)
        osw = jnp.stack([os5[:, 0, :, :, :SWA_HEAD_DIM], os5[:, 1, :, :, SWA_HEAD_DIM:]], axis=2)
        osw = osw.reshape(n_s, SWA_Q)
        xs = _out_ffn(xs, og.astype(BF16), gate, gn, osw.astype(BF16), wo_b, nw_f, wg_b, wu_b, wd_b, nf, l, final)

    s_all, ck_all, cv_all = carried
    pos_major = lambda c: c.reshape(DEPTH, nbatch, SWA_KV_HEADS, SWA_HEAD_DIM, WINDOW).transpose(0, 1, 4, 2, 3)
    return (xp.reshape(batch, seq, D_MODEL), xs.reshape(nbatch, nt, D_MODEL),
            jnp.stack(sp_l), jnp.stack(kp_l), jnp.stack(vp_l),
            s_all.reshape(DEPTH, nbatch, GLA_HEADS, GLA_DK, GLA_DV), pos_major(ck_all), pos_major(cv_all))
```

```python
import functools

import jax
import jax.numpy as jnp
import numpy as np
from jax import lax
from jax.experimental import pallas as pl
from jax.experimental.pallas import tpu as pltpu

F32 = jnp.float32
BF16 = jnp.bfloat16

D_MODEL = 1024
DEPTH = 4
PAST_LEN = 8192
GLA_HEADS = 4
GLA_DV = 128
GLA_DK = 64
GLA_QK = GLA_HEADS * GLA_DK
GLA_WIDTH = GLA_HEADS * GLA_DV
GATE_RANK = 16
GATE_NORMALIZER = 16.0
SWA_HEAD_DIM = 64
SWA_HEADS = 8
SWA_KV_HEADS = 2
SWA_GROUP = 4
SWA_Q = SWA_HEADS * SWA_HEAD_DIM
SWA_KV = SWA_KV_HEADS * SWA_HEAD_DIM
WINDOW = 128
ROPE_THETA = 10000.0
D_FF = 2816
NORM_EPS = 1e-6

LANES = 128
SUBLANES = 8
VMEM_LIMIT = 56 * 1024 * 1024

C_Q, C_K, C_V, C_GG, C_SQ, C_SK, C_SV, C_GA = 0, 256, 512, 1024, 1536, 2048, 2176, 2304
IN_COLS_P = C_GA + LANES

TOK_TILE = 512
GLA_CHUNK = 64
GLA_SUB = 8
GLA_TILE = 512
GLA_UNROLL = 8
FF_CHUNK = D_FF
FF_SPLITS = (0, 768, 1536, 2304, D_FF)
SWA_ROWS = 32
SAMPLE_BT = 16
NEG = -1e30
LOG2E = 1.4426950408889634


def _rms(x, w):
    ms = jnp.mean(x * x, axis=-1, keepdims=True)
    return x * lax.rsqrt(ms + NORM_EPS) * w


def _sigmoid(x):
    return 1.0 / (1.0 + jnp.exp(-x))


def _dot(a, b):
    return jnp.dot(a, b, preferred_element_type=F32)


def _dot_nt(a, b):
    return lax.dot_general(a, b, (((1,), (1,)), ((), ())), preferred_element_type=F32)


def _in_proj_kernel(x_ref, nw_ref, w_ref, wg2_ref, bg2_ref, cos_ref, sa_ref, sb_ref,
                    q_ref, k_ref, ghi_ref, glo_ref, v_ref, gate_ref, sq_ref, sk_ref, sv_ref):
    h = _rms(x_ref[...], nw_ref[...]).astype(BF16)
    ga0 = C_GG
    z = jnp.concatenate([_dot_nt(h, w_ref[:ga0, :]), _dot_nt(h, w_ref[ga0 + GATE_RANK:, :]),
                         _dot_nt(h, w_ref[ga0:ga0 + LANES, :])], axis=1)

    def proj(c0, c1):
        return z[:, c0:c1]

    q_ref[...] = proj(C_Q, C_K) * (GLA_DK ** -0.5)
    k_ref[...] = proj(C_K, C_V)
    v_ref[...] = proj(C_V, C_GG).astype(BF16)
    gg = proj(C_GG, C_SQ)
    gate_ref[...] = gg * _sigmoid(gg)
    ga = proj(C_GA, IN_COLS_P).astype(BF16)
    gx = _dot(ga, wg2_ref[...]) + bg2_ref[...]
    g2 = (jnp.minimum(gx, 0.0) - jnp.log1p(jnp.exp(-jnp.abs(gx)))) * (LOG2E / GATE_NORMALIZER)
    g_hi = g2.astype(BF16)
    ghi_ref[...] = g_hi
    glo_ref[...] = (g2 - g_hi.astype(F32)).astype(BF16)

    cos, sa, sb = cos_ref[...], sa_ref[...], sb_ref[...]

    def rope(z):
        return z * cos + pltpu.roll(z, LANES - 32, 1) * sa + pltpu.roll(z, 32, 1) * sb

    for j in range(SWA_Q // (2 * LANES)):
        zq = proj(C_SQ + 2 * j * LANES, C_SQ + 2 * (j + 1) * LANES)
        for jj in range(2):
            sq_ref[:, (2 * j + jj) * LANES:(2 * j + jj + 1) * LANES] = (
                rope(zq[:, jj * LANES:(jj + 1) * LANES]) * (SWA_HEAD_DIM ** -0.5 * LOG2E)).astype(BF16)
    zkv = proj(C_SK, C_GA)
    sk_ref[...] = rope(zkv[:, :SWA_KV])
    sv_ref[...] = zkv[:, SWA_KV:]


def _in_proj(x, nw, w, wg2, bg2, tabs, layer):
    n = x.shape[0]
    nt = n // TOK_TILE
    cos, sa, sb = tabs
    ntab = cos.shape[0] // TOK_TILE
    row = lambda w_: pl.BlockSpec((TOK_TILE, w_), lambda i: (i, 0))
    tab = pl.BlockSpec((TOK_TILE, LANES), lambda i: (i % ntab, 0))
    const = lambda shp: pl.BlockSpec((None,) + shp, lambda i: (layer, 0, 0), pipeline_mode=pl.Buffered(1))
    outs = [(GLA_QK, F32), (GLA_QK, F32), (GLA_QK, BF16), (GLA_QK, BF16), (GLA_WIDTH, BF16), (GLA_WIDTH, F32),
            (SWA_Q, BF16), (SWA_KV, F32), (SWA_KV, F32)]
    return pl.pallas_call(
        _in_proj_kernel,
        grid=(nt,),
        in_specs=[row(D_MODEL), const((1, D_MODEL)), const((w.shape[1], D_MODEL)),
                  const((LANES, GLA_QK)), const((1, GLA_QK)), tab, tab, tab],
        out_specs=[row(w_) for w_, _ in outs],
        out_shape=[jax.ShapeDtypeStruct((n, w_), dt) for w_, dt in outs],
        compiler_params=pltpu.CompilerParams(dimension_semantics=("arbitrary",),
                                             vmem_limit_bytes=VMEM_LIMIT),
        name="in_proj",
    )(x, nw, w, wg2, bg2, cos, sa, sb)


def _gla_kernel(q_ref, k_ref, ghi_ref, glo_ref, v_ref, bias_ref, o_ref, sout_ref, s_ref):
    ti = pl.program_id(1)
    C, SB = GLA_CHUNK, GLA_SUB
    NB = C // SB
    NP = GLA_HEADS // 2

    @pl.when(ti == 0)
    def _():
        s_ref[...] = jnp.zeros_like(s_ref)

    ltri = (lax.broadcasted_iota(jnp.int32, (C, 2 * C), 0) >= lax.broadcasted_iota(jnp.int32, (C, 2 * C), 1) % C
            ).astype(F32).astype(BF16)
    lane = lax.broadcasted_iota(jnp.int32, (1, LANES), 1)
    mask_a = (lane < GLA_DK).astype(F32)
    mask_b = 1.0 - mask_a
    col8 = lax.broadcasted_iota(jnp.int32, (SB, LANES), 1) % C
    colblk, colmod = col8 // SB, col8 % SB
    zk = jnp.zeros((GLA_DK, GLA_DV), BF16)
    off1 = [sum(C - SB * (jj + 1) for jj in range(j)) for j in range(NB - 1)]

    def sub_bcast(x, j):
        x3 = x.reshape(NB, SB, LANES)
        return jnp.broadcast_to(x3[:, j:j + 1, :], (NB, SB, LANES)).reshape(C, LANES)

    def heads(x):
        return jnp.concatenate([x * mask_a, x * mask_b], axis=0).astype(BF16)

    def body(it, carry):
        rows = [pl.ds(pl.multiple_of((it * GLA_UNROLL + cc) * C, C), C) for cc in range(GLA_UNROLL)]
        units = [(cc, p) for cc in range(GLA_UNROLL) for p in range(NP)]
        kl = [slice(p * LANES, (p + 1) * LANES) for p in range(NP)]
        vl = [slice(h * GLA_DV, (h + 1) * GLA_DV) for h in range(GLA_HEADS)]

        b_all = [_dot(ltri, jnp.concatenate([ghi_ref[rows[cc], :], glo_ref[rows[cc], :]], axis=0))
                 for cc in range(GLA_UNROLL)]

        qe, outs, dec, upd = {}, {}, {}, {}
        for cc, p in units:
            q, k, b = q_ref[rows[cc], kl[p]], k_ref[rows[cc], kl[p]], b_all[cc][:, kl[p]]
            b_last = b[C - 1:C, :]
            k1 = k * jnp.exp2(sub_bcast(b, SB - 1) - b)
            lhs1 = jnp.concatenate(
                [q[SB * (j + 1):, :] * jnp.exp2(b[SB * (j + 1):, :] - b[SB * j + SB - 1:SB * j + SB, :])
                 for j in range(NB - 1)], axis=0)
            lhs2 = jnp.concatenate(
                [q * jnp.exp2((b - sub_bcast(b, jj)) + jnp.tile(bias_ref[jj], (NB, 1))) for jj in range(SB)],
                axis=0)
            qe[cc, p] = (q * jnp.exp2(b)).astype(BF16)
            outs[cc, p] = (_dot_nt(lhs1.astype(BF16), heads(k1)), _dot_nt(lhs2.astype(BF16), heads(k)))
            kd = k * jnp.exp2(b_last - b)
            mt = jnp.concatenate([kd, jnp.broadcast_to(b_last, (C, LANES))], axis=0).T
            kdt = mt[:, :C].astype(BF16)
            dec[cc, p] = jnp.exp2(mt[:, C:C + 1])
            upd[cc, p] = (_dot(kdt[:GLA_DK], v_ref[rows[cc], vl[2 * p]]),
                          _dot(kdt[GLA_DK:], v_ref[rows[cc], vl[2 * p + 1]]))

        lo = {}
        for u in units:
            out1, out2 = outs[u]
            a_rows = []
            for i in range(NB):
                d = out2[(SB - 1) * C + SB * i:(SB - 1) * C + SB * (i + 1), :]
                for jj in range(SB - 2, -1, -1):
                    d = jnp.where(colmod == jj, out2[jj * C + SB * i:jj * C + SB * (i + 1), :], d)
                acc = jnp.where(colblk == i, d, 0.0)
                for j in range(i):
                    r0 = off1[j] + SB * (i - j - 1)
                    acc = jnp.where(colblk == j, out1[r0:r0 + SB, :], acc)
                a_rows.append(acc)
            a = jnp.concatenate(a_rows, axis=0)
            lo[u] = jnp.concatenate([qe[u], a.astype(BF16)], axis=1)

        state = [s_ref[h] for h in range(GLA_HEADS)]
        o_all = {}
        for cc in range(GLA_UNROLL):
            for p in range(NP):
                ha, hb = 2 * p, 2 * p + 1
                v_a, v_b = v_ref[rows[cc], vl[ha]], v_ref[rows[cc], vl[hb]]
                o_all[cc, ha] = _dot(lo[cc, p], jnp.concatenate([state[ha].astype(BF16), zk, v_a, zk], axis=0))
                o_all[cc, hb] = _dot(lo[cc, p], jnp.concatenate([zk, state[hb].astype(BF16), zk, v_b], axis=0))
                state[ha] = dec[cc, p][:GLA_DK] * state[ha] + upd[cc, p][0]
                state[hb] = dec[cc, p][GLA_DK:] * state[hb] + upd[cc, p][1]
        for h in range(GLA_HEADS):
            s_ref[h] = state[h]

        for cc in range(GLA_UNROLL):
            for h in range(GLA_HEADS):
                o_ref[rows[cc], vl[h]] = o_all[cc, h].astype(BF16)
        return carry

    lax.fori_loop(0, GLA_TILE // (C * GLA_UNROLL), body, 0)

    @pl.when(ti == pl.num_programs(1) - 1)
    def _():
        sout_ref[0] = s_ref[...]


def _gla_prompt(q, k, g_hi, g_lo, v, batch, seq):
    nt = seq // GLA_TILE
    blk = lambda w_: pl.BlockSpec((GLA_TILE, w_), lambda b, t: (b * nt + t, 0))
    r = np.arange(GLA_SUB)
    bias = np.where(r[None, :, None] >= r[:, None, None], 0.0, NEG) * np.ones((1, 1, LANES))
    return pl.pallas_call(
        _gla_kernel,
        grid=(batch, nt),
        in_specs=[blk(GLA_QK), blk(GLA_QK), blk(GLA_QK), blk(GLA_QK), blk(GLA_WIDTH),
                  pl.BlockSpec((GLA_SUB, GLA_SUB, LANES), lambda b, t: (0, 0, 0))],
        out_specs=[blk(GLA_WIDTH),
                   pl.BlockSpec((1, GLA_HEADS, GLA_DK, GLA_DV), lambda b, t: (b, 0, 0, 0))],
        out_shape=[jax.ShapeDtypeStruct((batch * seq, GLA_WIDTH), BF16),
                   jax.ShapeDtypeStruct((batch, GLA_HEADS, GLA_DK, GLA_DV), F32)],
        scratch_shapes=[pltpu.VMEM((GLA_HEADS, GLA_DK, GLA_DV), F32)],
        compiler_params=pltpu.CompilerParams(
            dimension_semantics=("arbitrary", "arbitrary"), vmem_limit_bytes=VMEM_LIMIT),
        name="gla_prompt",
    )(q, k, g_hi, g_lo, v, jnp.asarray(bias, F32))


def _out_ffn_kernel(x_ref, og_ref, gate_ref, gn_ref, os_ref, wo_ref, nw_ref, wg_ref, wu_ref, wd_ref, nf_ref,
                    y_ref, *, final):
    gn = gn_ref[...]
    og = jnp.concatenate([_rms(og_ref[:, h * GLA_DV:(h + 1) * GLA_DV].astype(F32), gn)
                          for h in range(GLA_HEADS)], axis=1) * gate_ref[...]
    x = (x_ref[...] + _dot(og.astype(BF16), wo_ref[:GLA_WIDTH, :]) + _dot(os_ref[...], wo_ref[GLA_WIDTH:, :]))
    h = _rms(x, nw_ref[...]).astype(BF16)
    acc = x
    for c0 in range(0, D_FF, FF_CHUNK):
        a = _dot(h, wg_ref[:, c0:c0 + FF_CHUNK])
        u = _dot(h, wu_ref[:, c0:c0 + FF_CHUNK])
        acc = acc + _dot((a * _sigmoid(a) * u).astype(BF16), wd_ref[c0:c0 + FF_CHUNK, :])
    if final:
        acc = _rms(acc, nf_ref[...])
    y_ref[...] = acc


def _out_ffn(x, og, gate, gn, osw, wo, nw, wg, wu, wd, nf, layer, final):
    n = x.shape[0]
    row = lambda w_: pl.BlockSpec((TOK_TILE, w_), lambda i: (i, 0))
    const = lambda shp: pl.BlockSpec((None,) + shp, lambda i: (layer, 0, 0), pipeline_mode=pl.Buffered(1))
    return pl.pallas_call(
        functools.partial(_out_ffn_kernel, final=final),
        grid=(n // TOK_TILE,),
        in_specs=[row(D_MODEL), row(GLA_WIDTH), row(GLA_WIDTH), const((1, GLA_DV)), row(SWA_Q),
                  const((D_MODEL, D_MODEL)), const((1, D_MODEL)),
                  const((D_MODEL, D_FF)), const((D_MODEL, D_FF)), const((D_FF, D_MODEL)),
                  pl.BlockSpec((1, D_MODEL), lambda i: (0, 0))],
        out_specs=row(D_MODEL),
        out_shape=jax.ShapeDtypeStruct((n, D_MODEL), F32),
        compiler_params=pltpu.CompilerParams(dimension_semantics=("arbitrary",),
                                             vmem_limit_bytes=VMEM_LIMIT),
        name="out_ffn",
    )(x, og, gate, gn, osw, wo, nw, wg, wu, wd, nf)


def _swa_setup(kp_ref, kc_ref, vp_ref, vc_ref, ke_scr, ve_scr):
    kk = jnp.concatenate([kp_ref[...], kc_ref[...]], axis=0)
    vv = jnp.concatenate([vp_ref[...], vc_ref[...]], axis=0)
    lo = lax.broadcasted_iota(jnp.int32, kk.shape, 1) < SWA_HEAD_DIM
    for x, scr in ((kk, ke_scr), (vv, ve_scr)):
        x_sw = pltpu.roll(x, SWA_HEAD_DIM, 1)
        scr[0, 0] = jnp.where(lo, x, 0.0).astype(BF16)
        scr[0, 1] = jnp.where(lo, 0.0, x_sw).astype(BF16)
        scr[1, 0] = jnp.where(lo, x_sw, 0.0).astype(BF16)
        scr[1, 1] = jnp.where(lo, 0.0, x).astype(BF16)


def _swa_scores(j, q_ref, ke_scr, s_scr):
    W = WINDOW
    for p in range(SWA_HEADS // 2):
        kv = p // (SWA_GROUP // 2)
        q = q_ref[j * W:(j + 1) * W, p * LANES:(p + 1) * LANES]
        for hh in range(2):
            s_scr[p, :, hh * 2 * W:(hh + 1) * 2 * W] = _dot_nt(q, ke_scr[kv, hh, j * W:(j + 2) * W, :])


def _swa_softmax(sink_ref, bias_blk, s_scr, p_scr, r_scr):
    W = WINDOW
    lane_q = lax.broadcasted_iota(jnp.int32, (SWA_ROWS, LANES), 1) < SWA_HEAD_DIM
    for p in range(SWA_HEADS // 2):
        for r0 in range(0, W, SWA_ROWS):
            rs = slice(r0, r0 + SWA_ROWS)
            rden = []
            for hh in range(2):
                cs = slice(hh * 2 * W, (hh + 1) * 2 * W)
                sink = sink_ref[2 * p + hh] * LOG2E
                sh = s_scr[p, rs, cs] + bias_blk[rs, :]
                m = jnp.maximum(jnp.max(sh, axis=-1, keepdims=True), sink)
                e = jnp.exp2(sh - m)
                den = jnp.sum(e, axis=-1, keepdims=True) + jnp.exp2(sink - m)
                p_scr[p, rs, cs] = e.astype(BF16)
                rden.append(1.0 / den)
            r_scr[p, rs, :] = jnp.where(lane_q, rden[0], rden[1])


def _swa_values(j, p_scr, ve_scr, r_scr, o_ref):
    W = WINDOW
    for p in range(SWA_HEADS // 2):
        kv = p // (SWA_GROUP // 2)
        o = (_dot(p_scr[p, :, :2 * W], ve_scr[kv, 0, j * W:(j + 2) * W, :])
             + _dot(p_scr[p, :, 2 * W:], ve_scr[kv, 1, j * W:(j + 2) * W, :])) * r_scr[p]
        o_ref[j * W:(j + 1) * W, p * LANES:(p + 1) * LANES] = o.astype(BF16)


def _ffn_swa_kernel(sink_ref, x_ref, og_ref, gate_ref, gn_ref, wo_ref, nw_ref, wg_ref, wu_ref, wd_ref, nf_ref,
                    osw0_ref, q_ref, kp_ref, kc_ref, vp_ref, vc_ref, bias_ref, y_ref,
                    osw_scr, ke_scr, ve_scr, s_scr, p_scr, r_scr, *, final, tiles_per_seq, n_tiles):
    s = pl.program_id(0)
    nblk = TOK_TILE // WINDOW

    @pl.when(s == 0)
    def _():
        osw_scr[0] = osw0_ref[...]

    _swa_setup(kp_ref, kc_ref, vp_ref, vc_ref, ke_scr, ve_scr)
    t_seq = jnp.minimum(s + 1, n_tiles - 1) % tiles_per_seq
    osw_cur = osw_scr.at[(s + 1) % 2]

    gn = gn_ref[...]
    og = jnp.concatenate([_rms(og_ref[:, h * GLA_DV:(h + 1) * GLA_DV].astype(F32), gn)
                          for h in range(GLA_HEADS)], axis=1) * gate_ref[...]
    x = (x_ref[...] + _dot(og.astype(BF16), wo_ref[:GLA_WIDTH, :])
         + _dot(osw_scr[s % 2], wo_ref[GLA_WIDTH:, :]))
    h = _rms(x, nw_ref[...]).astype(BF16)

    t_parts = []
    for j in range(nblk):
        c0, c1 = FF_SPLITS[j], FF_SPLITS[j + 1]
        _swa_scores(j, q_ref, ke_scr, s_scr)
        a = _dot(h, wg_ref[:, c0:c1])
        u = _dot(h, wu_ref[:, c0:c1])
        t_parts.append((a * _sigmoid(a) * u).astype(BF16))
        _swa_softmax(sink_ref, bias_ref.at[jnp.minimum(t_seq * nblk + j, 1)], s_scr, p_scr, r_scr)
        _swa_values(j, p_scr, ve_scr, r_scr, osw_cur)
    acc = x + _dot(jnp.concatenate(t_parts, axis=1), wd_ref[...])
    if final:
        acc = _rms(acc, nf_ref[...])
    y_ref[...] = acc


def _swa_first_kernel(sink_ref, q_ref, kp_ref, kc_ref, vp_ref, vc_ref, bias_ref, o_ref,
                      ke_scr, ve_scr, s_scr, p_scr, r_scr):
    _swa_setup(kp_ref, kc_ref, vp_ref, vc_ref, ke_scr, ve_scr)
    for j in range(TOK_TILE // WINDOW):
        _swa_scores(j, q_ref, ke_scr, s_scr)
        _swa_softmax(sink_ref, bias_ref.at[min(j, 1)], s_scr, p_scr, r_scr)
        _swa_values(j, p_scr, ve_scr, r_scr, o_ref)


def _ffn_swa(x, og, gate, gn, sinks, sq, sk, sv, wo, nw, wg, wu, wd, nf, layer, final, seq):
    n = x.shape[0]
    W = WINDOW
    n_tiles = n // TOK_TILE
    tiles_per_seq = seq // TOK_TILE
    nblk = TOK_TILE // W
    ffn_i = lambda s: (s, 0)
    swa_i = lambda s: (jnp.minimum(s + 1, n_tiles - 1), 0)

    def prev_i(s):
        t = jnp.minimum(s + 1, n_tiles - 1)
        return (t - t % tiles_per_seq) * nblk + jnp.maximum((t % tiles_per_seq) * nblk - 1, 0), 0

    row = lambda w_, im: pl.BlockSpec((TOK_TILE, w_), im)
    const = lambda shp: pl.BlockSpec((None,) + shp, lambda s: (layer, 0, 0), pipeline_mode=pl.Buffered(1))
    first = lambda s: (0, 0)
    qi, kj = np.arange(W)[:, None], np.arange(2 * W)[None, :]
    band = (kj > qi) & (kj <= qi + W)
    bias = jnp.asarray(np.stack([np.where(band & (kj >= W), 0.0, NEG), np.where(band, 0.0, NEG)]), F32)
    bias_spec = pl.BlockSpec((2, W, 2 * W), lambda s: (0, 0, 0))
    ext_scr = pltpu.VMEM((SWA_KV_HEADS, 2, W + TOK_TILE, LANES), BF16)
    npair = SWA_HEADS // 2
    swa_scr = [ext_scr, ext_scr, pltpu.VMEM((npair, W, 4 * W), F32), pltpu.VMEM((npair, W, 4 * W), BF16),
               pltpu.VMEM((npair, W, LANES), F32)]
    osw0 = pl.pallas_call(
        _swa_first_kernel,
        grid=(1,),
        in_specs=[pl.BlockSpec(memory_space=pltpu.SMEM), row(SWA_Q, first),
                  pl.BlockSpec((W, SWA_KV), first), row(SWA_KV, first),
                  pl.BlockSpec((W, SWA_KV), first), row(SWA_KV, first), bias_spec],
        out_specs=row(SWA_Q, first),
        out_shape=jax.ShapeDtypeStruct((TOK_TILE, SWA_Q), BF16),
        scratch_shapes=swa_scr,
        compiler_params=pltpu.CompilerParams(dimension_semantics=("arbitrary",), vmem_limit_bytes=VMEM_LIMIT),
        name="swa_first",
    )(sinks, sq, sk, sk, sv, sv, bias)
    return pl.pallas_call(
        functools.partial(_ffn_swa_kernel, final=final, tiles_per_seq=tiles_per_seq, n_tiles=n_tiles),
        grid=(n_tiles,),
        in_specs=[pl.BlockSpec(memory_space=pltpu.SMEM),
                  row(D_MODEL, ffn_i), row(GLA_WIDTH, ffn_i), row(GLA_WIDTH, ffn_i), const((1, GLA_DV)),
                  const((D_MODEL, D_MODEL)), const((1, D_MODEL)),
                  const((D_MODEL, D_FF)), const((D_MODEL, D_FF)), const((D_FF, D_MODEL)),
                  pl.BlockSpec((1, D_MODEL), lambda s: (0, 0)),
                  row(SWA_Q, first), row(SWA_Q, swa_i),
                  pl.BlockSpec((W, SWA_KV), prev_i), row(SWA_KV, swa_i),
                  pl.BlockSpec((W, SWA_KV), prev_i), row(SWA_KV, swa_i), bias_spec],
        out_specs=row(D_MODEL, ffn_i),
        out_shape=jax.ShapeDtypeStruct((n, D_MODEL), F32),
        scratch_shapes=[pltpu.VMEM((2, TOK_TILE, SWA_Q), BF16)] + swa_scr,
        compiler_params=pltpu.CompilerParams(dimension_semantics=("arbitrary",),
                                             vmem_limit_bytes=VMEM_LIMIT),
        name="ffn_swa",
    )(sinks, x, og, gate, gn, wo, nw, wg, wu, wd, nf, osw0, sq, sk, sk, sv, sv, bias)


def _sample_kernel(qkg_ref, v_ref, s0_ref,
                   qx_ref, kn_ref, vn_ref, ck_ref, cv_ref, sink_ref,
                   s_prev_ref, ck_prev_ref, cv_prev_ref,
                   og_ref, os_ref, sn_ref, cko_ref, cvo_ref, *, nt):
    del s_prev_ref, ck_prev_ref, cv_prev_ref
    TP = SUBLANES
    q, k, g, v = qkg_ref[0], qkg_ref[1], qkg_ref[2], v_ref[...]
    s0 = s0_ref[...]
    tt = lax.broadcasted_iota(jnp.int32, (1, TP, 1), 1)
    b = jnp.zeros_like(g)
    for s in range(nt):
        b = b + jnp.where(tt >= s, g[:, s:s + 1, :], 0.0)
    o = jnp.einsum('gtk,gkv->gtv', (q * jnp.exp2(b)).astype(BF16), s0.astype(BF16),
                   preferred_element_type=F32)
    for s in range(nt):
        dec = jnp.exp2(jnp.where(tt >= s, b - b[:, s:s + 1, :], NEG))
        a = jnp.sum(q * k[:, s:s + 1, :] * dec, axis=-1, keepdims=True)
        o = o + a * v[:, s:s + 1, :]
    og_ref[...] = o

    b_last = b[:, nt - 1:nt, :]
    d = jnp.exp2(b_last)
    d_hi = d.astype(BF16).astype(F32)
    x = jnp.where(tt < nt, k * jnp.exp2(b_last - b), jnp.where(tt == nt, d_hi, jnp.where(tt == nt + 1, d - d_hi, 0.0)))
    ones = jnp.where((tt == nt) | (tt == nt + 1), 1.0, 0.0)
    y = jnp.concatenate([jnp.where(tt < nt, v, 0.0), jnp.broadcast_to(ones, v.shape)], axis=2)
    r = lax.dot_general(x.astype(BF16), y.astype(BF16), (((1,), (1,)), ((0,), (0,))),
                        preferred_element_type=F32)
    sn_ref[...] = r[:, :, GLA_DV:] * s0 + r[:, :, :GLA_DV]

    W = WINDOW
    qx = qx_ref[...]
    ck, cv = ck_ref[...], cv_ref[...]
    kn, vn = kn_ref[...], vn_ref[...]
    sink = sink_ref[...] * LOG2E
    nrow = SWA_KV_HEADS * nt * SWA_GROUP
    r = lax.broadcasted_iota(jnp.int32, (1, nrow, W), 1)
    tq = (r % (nt * SWA_GROUP)) // SWA_GROUP
    j = lax.broadcasted_iota(jnp.int32, (1, nrow, W), 2)
    sc = jnp.einsum('bqd,bds->bqs', qx, ck.astype(BF16), preferred_element_type=F32)
    sc = jnp.where(j > tq, sc, NEG)
    tq1 = tq[:, :, 0:1]
    qf = qx.astype(F32)
    sn = [jnp.where(tq1 >= s, jnp.sum(qf * kn[:, s:s + 1, :], axis=-1, keepdims=True), NEG)
          for s in range(nt)]
    m = jnp.maximum(jnp.max(sc, axis=-1, keepdims=True), sink)
    for s in range(nt):
        m = jnp.maximum(m, sn[s])
    pc = jnp.exp2(sc - m)
    pn = [jnp.exp2(sn[s] - m) for s in range(nt)]
    den = jnp.sum(pc, axis=-1, keepdims=True) + jnp.exp2(sink - m)
    for s in range(nt):
        den = den + pn[s]
    o = jnp.einsum('bqs,bds->bqd', pc.astype(BF16), cv.astype(BF16), preferred_element_type=F32)
    for s in range(nt):
        o = o + pn[s] * vn[:, s:s + 1, :]
    os_ref[...] = o * (1.0 / den)

    wl = lax.broadcasted_iota(jnp.int32, (1, 1, W), 2)
    pr = lax.broadcasted_iota(jnp.int32, (1, 2 * nt, W), 1)
    pw = lax.broadcasted_iota(jnp.int32, (1, 2 * nt, W), 2)
    place = jnp.broadcast_to(jnp.where(pw == W - nt + pr % nt, 1.0, 0.0), (ck.shape[0], 2 * nt, W)).astype(BF16)
    for src, new, dst in ((ck, kn, cko_ref), (cv, vn, cvo_ref)):
        hi = new.astype(BF16).astype(F32)
        x = jnp.concatenate([hi, new - hi], axis=1).astype(BF16)
        placed = lax.dot_general(x, place, (((1,), (1,)), ((0,), (0,))), preferred_element_type=F32)
        dst[...] = jnp.where(wl >= W - nt, placed, pltpu.roll(src, W - nt, 2))


def _sample_mix(qkg4, v4, state, qx, kn, vn, ck, cv, sink_rows, carried, layer, nbatch, nt):
    bt = SAMPLE_BT
    G = bt * GLA_HEADS
    TP = SUBLANES
    nrow = SWA_KV_HEADS * nt * SWA_GROUP
    b3 = lambda a, c: pl.BlockSpec((G, a, c), lambda i: (i, 0, 0))
    s3 = lambda a, c: pl.BlockSpec((bt, a, c), lambda i: (i, 0, 0))
    lay_g = pl.BlockSpec((None, G, GLA_DK, GLA_DV), lambda i: (layer, i, 0, 0))
    lay_c = pl.BlockSpec((None, bt, SWA_KV, WINDOW), lambda i: (layer, i, 0, 0))
    ng = nbatch * GLA_HEADS
    n_in = 9
    return pl.pallas_call(
        functools.partial(_sample_kernel, nt=nt),
        grid=(nbatch // bt,),
        in_specs=[pl.BlockSpec((3, G, TP, GLA_DK), lambda i: (0, i, 0, 0)), b3(TP, GLA_DV), lay_g,
                  s3(nrow, SWA_KV), s3(nt, SWA_KV), s3(nt, SWA_KV), lay_c, lay_c,
                  pl.BlockSpec((1, nrow, 1), lambda i: (0, 0, 0))]
                 + [pl.BlockSpec(memory_space=pl.ANY)] * 3,
        out_specs=[b3(TP, GLA_DV), s3(nrow, SWA_KV), lay_g, lay_c, lay_c],
        out_shape=[jax.ShapeDtypeStruct((ng, TP, GLA_DV), F32),
                   jax.ShapeDtypeStruct((nbatch, nrow, SWA_KV), F32),
                   jax.ShapeDtypeStruct((DEPTH, ng, GLA_DK, GLA_DV), F32),
                   jax.ShapeDtypeStruct((DEPTH, nbatch, SWA_KV, WINDOW), F32),
                   jax.ShapeDtypeStruct((DEPTH, nbatch, SWA_KV, WINDOW), F32)],
        input_output_aliases={n_in + j: 2 + j for j in range(3)},
        compiler_params=pltpu.CompilerParams(dimension_semantics=("arbitrary",),
                                             vmem_limit_bytes=VMEM_LIMIT),
        name="sample_mix",
    )(qkg4, v4, state, qx, kn, vn, ck, cv, sink_rows, *carried)


def _rope_tables(pos):
    half = SWA_HEAD_DIM // 2
    inv = ROPE_THETA ** (-jnp.arange(half, dtype=F32) / half)
    ang = pos.astype(F32)[:, None] * inv[None, :]
    cos, sin = jnp.cos(ang), jnp.sin(ang)
    zero = jnp.zeros_like(sin)
    rep = LANES // SWA_HEAD_DIM
    return (jnp.tile(jnp.concatenate([cos, cos], axis=1), (1, rep)),
            jnp.tile(jnp.concatenate([-sin, zero], axis=1), (1, rep)),
            jnp.tile(jnp.concatenate([zero, sin], axis=1), (1, rep)))


def kernel(x_prompt, x_sample, state_gla, cache_swa_k, cache_swa_v, norm_attn, w_in, w_gk2, b_gk2,
           gla_norm, attn_sinks, w_o, norm_ffn, w_gate, w_up, w_down, norm_final):
    batch, seq, _ = x_prompt.shape
    nbatch, nt, _ = x_sample.shape
    n_s = nbatch * nt

    w_in_p = jnp.swapaxes(w_in, 1, 2).astype(BF16)
    wg2_p = jnp.concatenate([w_gk2, jnp.zeros((DEPTH, LANES - GATE_RANK, GLA_QK), w_gk2.dtype)],
                            axis=1).astype(BF16)
    bg2 = b_gk2.reshape(DEPTH, 1, GLA_QK)
    nw_a = norm_attn.reshape(DEPTH, 1, D_MODEL)
    nw_f = norm_ffn.reshape(DEPTH, 1, D_MODEL)
    wo_b, wg_b, wu_b, wd_b = (w.astype(BF16) for w in (w_o, w_gate, w_up, w_down))
    nf = norm_final.reshape(1, D_MODEL)
    gn = gla_norm.reshape(DEPTH, 1, GLA_DV)

    tabs_p = _rope_tables(jnp.arange(seq))
    tabs_s = _rope_tables(jnp.tile(PAST_LEN + jnp.arange(nt), TOK_TILE // nt))

    state_r = state_gla.reshape(DEPTH, nbatch * GLA_HEADS, GLA_DK, GLA_DV)
    pos_minor = lambda c: c.transpose(0, 1, 3, 4, 2).reshape(DEPTH, nbatch, SWA_KV, WINDOW)
    ck_r, cv_r = pos_minor(cache_swa_k), pos_minor(cache_swa_v)

    xp = x_prompt.reshape(batch * seq, D_MODEL)
    xs = x_sample.reshape(n_s, D_MODEL)
    sp_l, kp_l, vp_l = [], [], []
    carried = [jnp.zeros(state_r.shape, F32), jnp.zeros(ck_r.shape, F32), jnp.zeros(cv_r.shape, F32)]
    tpad = ((0, 0), (0, SUBLANES - nt), (0, 0))
    for l in range(DEPTH):
        final = l == DEPTH - 1
        q, k, g_hi, g_lo, v, gate, sq, sk, sv = _in_proj(xp, nw_a, w_in_p, wg2_p, bg2, tabs_p, l)
        og, s_fin = _gla_prompt(q, k, g_hi, g_lo, v, batch, seq)
        xp = _ffn_swa(xp, og, gate, gn, attn_sinks[l], sq, sk, sv, wo_b, nw_f, wg_b, wu_b, wd_b, nf, l, final, seq)
        sp_l.append(s_fin)
        last = lambda a: a.reshape(batch, seq, SWA_KV)[:, seq - WINDOW:].reshape(
            batch, WINDOW, SWA_KV_HEADS, SWA_HEAD_DIM)
        kp_l.append(last(sk))
        vp_l.append(last(sv))

        q, k, g_hi, g_lo, v, gate, sq, sk, sv = _in_proj(xs, nw_a, w_in_p, wg2_p, bg2, tabs_s, l)
        g = g_hi.astype(F32) + g_lo.astype(F32)

        def bhtk(a, w_):
            return a.reshape(nbatch, nt, GLA_HEADS, w_).transpose(0, 2, 1, 3).reshape(nbatch * GLA_HEADS, nt, w_)

        qkg4 = jnp.pad(jnp.stack([bhtk(a, GLA_DK) for a in (q, k, g)]), ((0, 0),) + tpad)
        v4 = jnp.pad(bhtk(v.astype(F32), GLA_DV), tpad)
        qk = sq.reshape(nbatch, nt, SWA_KV_HEADS, SWA_GROUP, SWA_HEAD_DIM).transpose(0, 2, 1, 3, 4)
        zq = jnp.zeros_like(qk[:, 0])
        qx = jnp.stack([jnp.concatenate([qk[:, 0], zq], axis=-1), jnp.concatenate([zq, qk[:, 1]], axis=-1)],
                       axis=1).reshape(nbatch, SWA_KV_HEADS * nt * SWA_GROUP, SWA_KV)
        sink_rows = jnp.broadcast_to(attn_sinks[l].reshape(SWA_KV_HEADS, 1, SWA_GROUP),
                                     (SWA_KV_HEADS, nt, SWA_GROUP)).reshape(1, -1, 1)
        og4, os4, *carried = _sample_mix(
            qkg4, v4, state_r, qx, sk.reshape(nbatch, nt, SWA_KV), sv.reshape(nbatch, nt, SWA_KV),
            ck_r, cv_r, sink_rows, carried, l, nbatch, nt)
        og = og4[:, :nt].reshape(nbatch, GLA_HEADS, nt, GLA_DV).transpose(0, 2, 1, 3).reshape(n_s, GLA_WIDTH)
        os5 = os4.reshape(nbatch, SWA_KV_HEADS, nt, SWA_GROUP, SWA_KV)
        osw = jnp.stack([os5[:, 0, :, :, :SWA_HEAD_DIM], os5[:, 1, :, :, SWA_HEAD_DIM:]], axis=2)
        osw = osw.reshape(n_s, SWA_Q)
        xs = _out_ffn(xs, og.astype(BF16), gate, gn, osw.astype(BF16), wo_b, nw_f, wg_b, wu_b, wd_b, nf, l, final)

    s_all, ck_all, cv_all = carried
    pos_major = lambda c: c.reshape(DEPTH, nbatch, SWA_KV_HEADS, SWA_HEAD_DIM, WINDOW).transpose(0, 1, 4, 2, 3)
    return (xp.reshape(batch, seq, D_MODEL), xs.reshape(nbatch, nt, D_MODEL),
            jnp.stack(sp_l), jnp.stack(kp_l), jnp.stack(vp_l),
            s_all.reshape(DEPTH, nbatch, GLA_HEADS, GLA_DK, GLA_DV), pos_major(ck_all), pos_major(cv_all))
```

```python
import functools

import jax
import jax.numpy as jnp
import numpy as np
from jax import lax
from jax.experimental import pallas as pl
from jax.experimental.pallas import tpu as pltpu

F32 = jnp.float32
BF16 = jnp.bfloat16

D_MODEL = 1024
DEPTH = 4
PAST_LEN = 8192
GLA_HEADS = 4
GLA_DV = 128
GLA_DK = 64
GLA_QK = GLA_HEADS * GLA_DK
GLA_WIDTH = GLA_HEADS * GLA_DV
GATE_RANK = 16
GATE_NORMALIZER = 16.0
SWA_HEAD_DIM = 64
SWA_HEADS = 8
SWA_KV_HEADS = 2
SWA_GROUP = 4
SWA_Q = SWA_HEADS * SWA_HEAD_DIM
SWA_KV = SWA_KV_HEADS * SWA_HEAD_DIM
WINDOW = 128
ROPE_THETA = 10000.0
D_FF = 2816
NORM_EPS = 1e-6

LANES = 128
SUBLANES = 8
VMEM_LIMIT = 56 * 1024 * 1024

C_Q, C_K, C_V, C_GG, C_SQ, C_SK, C_SV, C_GA = 0, 256, 512, 1024, 1536, 2048, 2176, 2304
IN_COLS_P = C_GA + LANES

TOK_TILE = 512
IN_TILE = 1024
GLA_CHUNK = 64
GLA_SUB = 8
GLA_TILE = 1024
GLA_UNROLL = 8
FF_CHUNK = D_FF
FF_SPLITS = (0, 768, 1536, 2304, D_FF)
SWA_ROWS = 32
SAMPLE_BT = 16
NEG = -1e30
LOG2E = 1.4426950408889634


def _rms(x, w):
    ms = jnp.mean(x * x, axis=-1, keepdims=True)
    return x * lax.rsqrt(ms + NORM_EPS) * w


def _sigmoid(x):
    return 1.0 / (1.0 + jnp.exp(-x))


def _dot(a, b):
    return jnp.dot(a, b, preferred_element_type=F32)


def _dot_nt(a, b):
    return lax.dot_general(a, b, (((1,), (1,)), ((), ())), preferred_element_type=F32)


def _in_proj_kernel(x_ref, nw_ref, w_ref, wg2_ref, bg2_ref, cos_ref, sa_ref, sb_ref,
                    q_ref, k_ref, ghi_ref, glo_ref, v_ref, gate_ref, sq_ref, sk_ref, sv_ref):
    parts = [slice(r0, r0 + TOK_TILE) for r0 in range(0, x_ref.shape[0], TOK_TILE)]
    hs = [_rms(x_ref[r, :], nw_ref[...]).astype(BF16) for r in parts]
    ga0 = C_GG
    zs = [jnp.concatenate([_dot_nt(h, w_ref[:ga0, :]), _dot_nt(h, w_ref[ga0 + GATE_RANK:, :]),
                           _dot_nt(h, w_ref[ga0:ga0 + LANES, :])], axis=1) for h in hs]

    for r, z in zip(parts, zs):
        q_ref[r, :] = z[:, C_Q:C_K] * (GLA_DK ** -0.5)
        k_ref[r, :] = z[:, C_K:C_V]
        v_ref[r, :] = z[:, C_V:C_GG].astype(BF16)
        gg = z[:, C_GG:C_SQ]
        gate_ref[r, :] = gg * _sigmoid(gg)
        gx = _dot(z[:, C_GA:IN_COLS_P].astype(BF16), wg2_ref[...]) + bg2_ref[...]
        g2 = (jnp.minimum(gx, 0.0) - jnp.log1p(jnp.exp(-jnp.abs(gx)))) * (LOG2E / GATE_NORMALIZER)
        g_hi = g2.astype(BF16)
        ghi_ref[r, :] = g_hi
        glo_ref[r, :] = (g2 - g_hi.astype(F32)).astype(BF16)

        cos, sa, sb = cos_ref[r, :], sa_ref[r, :], sb_ref[r, :]

        def rope(y):
            return y * cos + pltpu.roll(y, LANES - 32, 1) * sa + pltpu.roll(y, 32, 1) * sb

        for j in range(SWA_Q // LANES):
            sq_ref[r, j * LANES:(j + 1) * LANES] = (
                rope(z[:, C_SQ + j * LANES:C_SQ + (j + 1) * LANES]) * (SWA_HEAD_DIM ** -0.5 * LOG2E)).astype(BF16)
        sk_ref[r, :] = rope(z[:, C_SK:C_SV])
        sv_ref[r, :] = z[:, C_SV:C_GA]


def _in_proj(x, nw, w, wg2, bg2, tabs, layer):
    n = x.shape[0]
    tile = min(n, IN_TILE)
    nt = n // tile
    cos, sa, sb = tabs
    ntab = cos.shape[0] // tile
    row = lambda w_: pl.BlockSpec((tile, w_), lambda i: (i, 0))
    tab = pl.BlockSpec((tile, LANES), lambda i: (i % ntab, 0))
    const = lambda shp: pl.BlockSpec((None,) + shp, lambda i: (layer, 0, 0), pipeline_mode=pl.Buffered(1))
    outs = [(GLA_QK, F32), (GLA_QK, F32), (GLA_QK, BF16), (GLA_QK, BF16), (GLA_WIDTH, BF16), (GLA_WIDTH, F32),
            (SWA_Q, BF16), (SWA_KV, F32), (SWA_KV, F32)]
    return pl.pallas_call(
        _in_proj_kernel,
        grid=(nt,),
        in_specs=[row(D_MODEL), const((1, D_MODEL)), const((w.shape[1], D_MODEL)),
                  const((LANES, GLA_QK)), const((1, GLA_QK)), tab, tab, tab],
        out_specs=[row(w_) for w_, _ in outs],
        out_shape=[jax.ShapeDtypeStruct((n, w_), dt) for w_, dt in outs],
        compiler_params=pltpu.CompilerParams(dimension_semantics=("arbitrary",),
                                             vmem_limit_bytes=VMEM_LIMIT),
        name="in_proj",
    )(x, nw, w, wg2, bg2, cos, sa, sb)


def _gla_kernel(q_ref, k_ref, ghi_ref, glo_ref, v_ref, bias_ref, o_ref, sout_ref, s_ref):
    ti = pl.program_id(1)
    C, SB = GLA_CHUNK, GLA_SUB
    NB = C // SB
    NP = GLA_HEADS // 2

    @pl.when(ti == 0)
    def _():
        s_ref[...] = jnp.zeros_like(s_ref)

    ltri = (lax.broadcasted_iota(jnp.int32, (C, 2 * C), 0) >= lax.broadcasted_iota(jnp.int32, (C, 2 * C), 1) % C
            ).astype(F32).astype(BF16)
    lane = lax.broadcasted_iota(jnp.int32, (1, LANES), 1)
    mask_a = (lane < GLA_DK).astype(F32)
    mask_b = 1.0 - mask_a
    col8 = lax.broadcasted_iota(jnp.int32, (SB, LANES), 1) % C
    colblk, colmod = col8 // SB, col8 % SB
    zk = jnp.zeros((GLA_DK, GLA_DV), BF16)
    off1 = [sum(C - SB * (jj + 1) for jj in range(j)) for j in range(NB - 1)]

    def sub_bcast(x, j):
        x3 = x.reshape(NB, SB, LANES)
        return jnp.broadcast_to(x3[:, j:j + 1, :], (NB, SB, LANES)).reshape(C, LANES)

    def heads(x):
        return jnp.concatenate([x * mask_a, x * mask_b], axis=0).astype(BF16)

    def body(it, carry):
        rows = [pl.ds(pl.multiple_of((it * GLA_UNROLL + cc) * C, C), C) for cc in range(GLA_UNROLL)]
        units = [(cc, p) for cc in range(GLA_UNROLL) for p in range(NP)]
        kl = [slice(p * LANES, (p + 1) * LANES) for p in range(NP)]
        vl = [slice(h * GLA_DV, (h + 1) * GLA_DV) for h in range(GLA_HEADS)]

        b_all = [_dot(ltri, jnp.concatenate([ghi_ref[rows[cc], :], glo_ref[rows[cc], :]], axis=0))
                 for cc in range(GLA_UNROLL)]

        qe, outs, dec, upd = {}, {}, {}, {}
        for cc, p in units:
            q, k, b = q_ref[rows[cc], kl[p]], k_ref[rows[cc], kl[p]], b_all[cc][:, kl[p]]
            b_last = b[C - 1:C, :]
            k1 = k * jnp.exp2(sub_bcast(b, SB - 1) - b)
            lhs1 = jnp.concatenate(
                [q[SB * (j + 1):, :] * jnp.exp2(b[SB * (j + 1):, :] - b[SB * j + SB - 1:SB * j + SB, :])
                 for j in range(NB - 1)], axis=0)
            lhs2 = jnp.concatenate(
                [q * jnp.exp2((b - sub_bcast(b, jj)) + jnp.tile(bias_ref[jj], (NB, 1))) for jj in range(SB)],
                axis=0)
            qe[cc, p] = (q * jnp.exp2(b)).astype(BF16)
            outs[cc, p] = (_dot_nt(lhs1.astype(BF16), heads(k1)), _dot_nt(lhs2.astype(BF16), heads(k)))
            kd = k * jnp.exp2(b_last - b)
            mt = jnp.concatenate([kd, jnp.broadcast_to(b_last, (C, LANES))], axis=0).T
            kdt = mt[:, :C].astype(BF16)
            dec[cc, p] = jnp.exp2(mt[:, C:C + 1])
            upd[cc, p] = (_dot(kdt[:GLA_DK], v_ref[rows[cc], vl[2 * p]]),
                          _dot(kdt[GLA_DK:], v_ref[rows[cc], vl[2 * p + 1]]))

        lo = {}
        for u in units:
            out1, out2 = outs[u]
            a_rows = []
            for i in range(NB):
                d = out2[(SB - 1) * C + SB * i:(SB - 1) * C + SB * (i + 1), :]
                for jj in range(SB - 2, -1, -1):
                    d = jnp.where(colmod == jj, out2[jj * C + SB * i:jj * C + SB * (i + 1), :], d)
                acc = jnp.where(colblk == i, d, 0.0)
                for j in range(i):
                    r0 = off1[j] + SB * (i - j - 1)
                    acc = jnp.where(colblk == j, out1[r0:r0 + SB, :], acc)
                a_rows.append(acc)
            a = jnp.concatenate(a_rows, axis=0)
            lo[u] = jnp.concatenate([qe[u], a.astype(BF16)], axis=1)

        state = [s_ref[h] for h in range(GLA_HEADS)]
        o_all = {}
        for cc in range(GLA_UNROLL):
            for p in range(NP):
                ha, hb = 2 * p, 2 * p + 1
                v_a, v_b = v_ref[rows[cc], vl[ha]], v_ref[rows[cc], vl[hb]]
                o_all[cc, ha] = _dot(lo[cc, p], jnp.concatenate([state[ha].astype(BF16), zk, v_a, zk], axis=0))
                o_all[cc, hb] = _dot(lo[cc, p], jnp.concatenate([zk, state[hb].astype(BF16), zk, v_b], axis=0))
                state[ha] = dec[cc, p][:GLA_DK] * state[ha] + upd[cc, p][0]
                state[hb] = dec[cc, p][GLA_DK:] * state[hb] + upd[cc, p][1]
        for h in range(GLA_HEADS):
            s_ref[h] = state[h]

        for cc in range(GLA_UNROLL):
            for h in range(GLA_HEADS):
                o_ref[rows[cc], vl[h]] = o_all[cc, h].astype(BF16)
        return carry

    lax.fori_loop(0, GLA_TILE // (C * GLA_UNROLL), body, 0)

    @pl.when(ti == pl.num_programs(1) - 1)
    def _():
        sout_ref[0] = s_ref[...]


def _gla_prompt(q, k, g_hi, g_lo, v, batch, seq):
    nt = seq // GLA_TILE
    blk = lambda w_: pl.BlockSpec((GLA_TILE, w_), lambda b, t: (b * nt + t, 0))
    r = np.arange(GLA_SUB)
    bias = np.where(r[None, :, None] >= r[:, None, None], 0.0, NEG) * np.ones((1, 1, LANES))
    return pl.pallas_call(
        _gla_kernel,
        grid=(batch, nt),
        in_specs=[blk(GLA_QK), blk(GLA_QK), blk(GLA_QK), blk(GLA_QK), blk(GLA_WIDTH),
                  pl.BlockSpec((GLA_SUB, GLA_SUB, LANES), lambda b, t: (0, 0, 0))],
        out_specs=[blk(GLA_WIDTH),
                   pl.BlockSpec((1, GLA_HEADS, GLA_DK, GLA_DV), lambda b, t: (b, 0, 0, 0))],
        out_shape=[jax.ShapeDtypeStruct((batch * seq, GLA_WIDTH), BF16),
                   jax.ShapeDtypeStruct((batch, GLA_HEADS, GLA_DK, GLA_DV), F32)],
        scratch_shapes=[pltpu.VMEM((GLA_HEADS, GLA_DK, GLA_DV), F32)],
        compiler_params=pltpu.CompilerParams(
            dimension_semantics=("arbitrary", "arbitrary"), vmem_limit_bytes=VMEM_LIMIT),
        name="gla_prompt",
    )(q, k, g_hi, g_lo, v, jnp.asarray(bias, F32))


def _out_ffn_kernel(x_ref, og_ref, gate_ref, gn_ref, os_ref, wo_ref, nw_ref, wg_ref, wu_ref, wd_ref, nf_ref,
                    y_ref, *, final):
    gn = gn_ref[...]
    og = jnp.concatenate([_rms(og_ref[:, h * GLA_DV:(h + 1) * GLA_DV].astype(F32), gn)
                          for h in range(GLA_HEADS)], axis=1) * gate_ref[...]
    x = (x_ref[...] + _dot(og.astype(BF16), wo_ref[:GLA_WIDTH, :]) + _dot(os_ref[...], wo_ref[GLA_WIDTH:, :]))
    h = _rms(x, nw_ref[...]).astype(BF16)
    acc = x
    for c0 in range(0, D_FF, FF_CHUNK):
        a = _dot(h, wg_ref[:, c0:c0 + FF_CHUNK])
        u = _dot(h, wu_ref[:, c0:c0 + FF_CHUNK])
        acc = acc + _dot((a * _sigmoid(a) * u).astype(BF16), wd_ref[c0:c0 + FF_CHUNK, :])
    if final:
        acc = _rms(acc, nf_ref[...])
    y_ref[...] = acc


def _out_ffn(x, og, gate, gn, osw, wo, nw, wg, wu, wd, nf, layer, final):
    n = x.shape[0]
    row = lambda w_: pl.BlockSpec((TOK_TILE, w_), lambda i: (i, 0))
    const = lambda shp: pl.BlockSpec((None,) + shp, lambda i: (layer, 0, 0), pipeline_mode=pl.Buffered(1))
    return pl.pallas_call(
        functools.partial(_out_ffn_kernel, final=final),
        grid=(n // TOK_TILE,),
        in_specs=[row(D_MODEL), row(GLA_WIDTH), row(GLA_WIDTH), const((1, GLA_DV)), row(SWA_Q),
                  const((D_MODEL, D_MODEL)), const((1, D_MODEL)),
                  const((D_MODEL, D_FF)), const((D_MODEL, D_FF)), const((D_FF, D_MODEL)),
                  pl.BlockSpec((1, D_MODEL), lambda i: (0, 0))],
        out_specs=row(D_MODEL),
        out_shape=jax.ShapeDtypeStruct((n, D_MODEL), F32),
        compiler_params=pltpu.CompilerParams(dimension_semantics=("arbitrary",),
                                             vmem_limit_bytes=VMEM_LIMIT),
        name="out_ffn",
    )(x, og, gate, gn, osw, wo, nw, wg, wu, wd, nf)


def _swa_setup(kp_ref, kc_ref, vp_ref, vc_ref, ke_scr, ve_scr):
    kk = jnp.concatenate([kp_ref[...], kc_ref[...]], axis=0)
    vv = jnp.concatenate([vp_ref[...], vc_ref[...]], axis=0)
    lo = lax.broadcasted_iota(jnp.int32, kk.shape, 1) < SWA_HEAD_DIM
    for x, scr in ((kk, ke_scr), (vv, ve_scr)):
        x_sw = pltpu.roll(x, SWA_HEAD_DIM, 1)
        scr[0, 0] = jnp.where(lo, x, 0.0).astype(BF16)
        scr[0, 1] = jnp.where(lo, 0.0, x_sw).astype(BF16)
        scr[1, 0] = jnp.where(lo, x_sw, 0.0).astype(BF16)
        scr[1, 1] = jnp.where(lo, 0.0, x).astype(BF16)


def _swa_scores(j, q_ref, ke_scr, s_scr):
    W = WINDOW
    for p in range(SWA_HEADS // 2):
        kv = p // (SWA_GROUP // 2)
        q = q_ref[j * W:(j + 1) * W, p * LANES:(p + 1) * LANES]
        for hh in range(2):
            s_scr[p, :, hh * 2 * W:(hh + 1) * 2 * W] = _dot_nt(q, ke_scr[kv, hh, j * W:(j + 2) * W, :])


def _swa_softmax(sink_ref, bias_blk, s_scr, p_scr, r_scr):
    W = WINDOW
    lane_q = lax.broadcasted_iota(jnp.int32, (SWA_ROWS, LANES), 1) < SWA_HEAD_DIM
    for p in range(SWA_HEADS // 2):
        for r0 in range(0, W, SWA_ROWS):
            rs = slice(r0, r0 + SWA_ROWS)
            rden = []
            for hh in range(2):
                cs = slice(hh * 2 * W, (hh + 1) * 2 * W)
                sink = sink_ref[2 * p + hh] * LOG2E
                sh = s_scr[p, rs, cs] + bias_blk[rs, :]
                m = jnp.maximum(jnp.max(sh, axis=-1, keepdims=True), sink)
                e = jnp.exp2(sh - m)
                den = jnp.sum(e, axis=-1, keepdims=True) + jnp.exp2(sink - m)
                p_scr[p, rs, cs] = e.astype(BF16)
                rden.append(1.0 / den)
            r_scr[p, rs, :] = jnp.where(lane_q, rden[0], rden[1])


def _swa_values(j, p_scr, ve_scr, r_scr, o_ref):
    W = WINDOW
    for p in range(SWA_HEADS // 2):
        kv = p // (SWA_GROUP // 2)
        o = (_dot(p_scr[p, :, :2 * W], ve_scr[kv, 0, j * W:(j + 2) * W, :])
             + _dot(p_scr[p, :, 2 * W:], ve_scr[kv, 1, j * W:(j + 2) * W, :])) * r_scr[p]
        o_ref[j * W:(j + 1) * W, p * LANES:(p + 1) * LANES] = o.astype(BF16)


def _ffn_swa_kernel(sink_ref, x_ref, og_ref, gate_ref, gn_ref, wo_ref, nw_ref, wg_ref, wu_ref, wd_ref, nf_ref,
                    osw0_ref, q_ref, kp_ref, kc_ref, vp_ref, vc_ref, bias_ref, y_ref,
                    osw_scr, ke_scr, ve_scr, s_scr, p_scr, r_scr, *, final, tiles_per_seq, n_tiles):
    s = pl.program_id(0)
    nblk = TOK_TILE // WINDOW

    @pl.when(s == 0)
    def _():
        osw_scr[0] = osw0_ref[...]

    _swa_setup(kp_ref, kc_ref, vp_ref, vc_ref, ke_scr, ve_scr)
    t_seq = jnp.minimum(s + 1, n_tiles - 1) % tiles_per_seq
    osw_cur = osw_scr.at[(s + 1) % 2]

    gn = gn_ref[...]
    og = jnp.concatenate([_rms(og_ref[:, h * GLA_DV:(h + 1) * GLA_DV].astype(F32), gn)
                          for h in range(GLA_HEADS)], axis=1) * gate_ref[...]
    x = (x_ref[...] + _dot(og.astype(BF16), wo_ref[:GLA_WIDTH, :])
         + _dot(osw_scr[s % 2], wo_ref[GLA_WIDTH:, :]))
    h = _rms(x, nw_ref[...]).astype(BF16)

    t_parts = []
    for j in range(nblk):
        c0, c1 = FF_SPLITS[j], FF_SPLITS[j + 1]
        _swa_scores(j, q_ref, ke_scr, s_scr)
        a = _dot(h, wg_ref[:, c0:c1])
        u = _dot(h, wu_ref[:, c0:c1])
        t_parts.append((a * _sigmoid(a) * u).astype(BF16))
        _swa_softmax(sink_ref, bias_ref.at[jnp.minimum(t_seq * nblk + j, 1)], s_scr, p_scr, r_scr)
        _swa_values(j, p_scr, ve_scr, r_scr, osw_cur)
    acc = x + _dot(jnp.concatenate(t_parts, axis=1), wd_ref[...])
    if final:
        acc = _rms(acc, nf_ref[...])
    y_ref[...] = acc


def _swa_first_kernel(sink_ref, q_ref, kp_ref, kc_ref, vp_ref, vc_ref, bias_ref, o_ref,
                      ke_scr, ve_scr, s_scr, p_scr, r_scr):
    _swa_setup(kp_ref, kc_ref, vp_ref, vc_ref, ke_scr, ve_scr)
    for j in range(TOK_TILE // WINDOW):
        _swa_scores(j, q_ref, ke_scr, s_scr)
        _swa_softmax(sink_ref, bias_ref.at[min(j, 1)], s_scr, p_scr, r_scr)
        _swa_values(j, p_scr, ve_scr, r_scr, o_ref)


def _ffn_swa(x, og, gate, gn, sinks, sq, sk, sv, wo, nw, wg, wu, wd, nf, layer, final, seq):
    n = x.shape[0]
    W = WINDOW
    n_tiles = n // TOK_TILE
    tiles_per_seq = seq // TOK_TILE
    nblk = TOK_TILE // W
    ffn_i = lambda s: (s, 0)
    swa_i = lambda s: (jnp.minimum(s + 1, n_tiles - 1), 0)

    def prev_i(s):
        t = jnp.minimum(s + 1, n_tiles - 1)
        return (t - t % tiles_per_seq) * nblk + jnp.maximum((t % tiles_per_seq) * nblk - 1, 0), 0

    row = lambda w_, im: pl.BlockSpec((TOK_TILE, w_), im)
    const = lambda shp: pl.BlockSpec((None,) + shp, lambda s: (layer, 0, 0), pipeline_mode=pl.Buffered(1))
    first = lambda s: (0, 0)
    qi, kj = np.arange(W)[:, None], np.arange(2 * W)[None, :]
    band = (kj > qi) & (kj <= qi + W)
    bias = jnp.asarray(np.stack([np.where(band & (kj >= W), 0.0, NEG), np.where(band, 0.0, NEG)]), F32)
    bias_spec = pl.BlockSpec((2, W, 2 * W), lambda s: (0, 0, 0))
    ext_scr = pltpu.VMEM((SWA_KV_HEADS, 2, W + TOK_TILE, LANES), BF16)
    npair = SWA_HEADS // 2
    swa_scr = [ext_scr, ext_scr, pltpu.VMEM((npair, W, 4 * W), F32), pltpu.VMEM((npair, W, 4 * W), BF16),
               pltpu.VMEM((npair, W, LANES), F32)]
    osw0 = pl.pallas_call(
        _swa_first_kernel,
        grid=(1,),
        in_specs=[pl.BlockSpec(memory_space=pltpu.SMEM), row(SWA_Q, first),
                  pl.BlockSpec((W, SWA_KV), first), row(SWA_KV, first),
                  pl.BlockSpec((W, SWA_KV), first), row(SWA_KV, first), bias_spec],
        out_specs=row(SWA_Q, first),
        out_shape=jax.ShapeDtypeStruct((TOK_TILE, SWA_Q), BF16),
        scratch_shapes=swa_scr,
        compiler_params=pltpu.CompilerParams(dimension_semantics=("arbitrary",), vmem_limit_bytes=VMEM_LIMIT),
        name="swa_first",
    )(sinks, sq, sk, sk, sv, sv, bias)
    return pl.pallas_call(
        functools.partial(_ffn_swa_kernel, final=final, tiles_per_seq=tiles_per_seq, n_tiles=n_tiles),
        grid=(n_tiles,),
        in_specs=[pl.BlockSpec(memory_space=pltpu.SMEM),
                  row(D_MODEL, ffn_i), row(GLA_WIDTH, ffn_i), row(GLA_WIDTH, ffn_i), const((1, GLA_DV)),
                  const((D_MODEL, D_MODEL)), const((1, D_MODEL)),
                  const((D_MODEL, D_FF)), const((D_MODEL, D_FF)), const((D_FF, D_MODEL)),
                  pl.BlockSpec((1, D_MODEL), lambda s: (0, 0)),
                  row(SWA_Q, first), row(SWA_Q, swa_i),
                  pl.BlockSpec((W, SWA_KV), prev_i), row(SWA_KV, swa_i),
                  pl.BlockSpec((W, SWA_KV), prev_i), row(SWA_KV, swa_i), bias_spec],
        out_specs=row(D_MODEL, ffn_i),
        out_shape=jax.ShapeDtypeStruct((n, D_MODEL), F32),
        scratch_shapes=[pltpu.VMEM((2, TOK_TILE, SWA_Q), BF16)] + swa_scr,
        compiler_params=pltpu.CompilerParams(dimension_semantics=("arbitrary",),
                                             vmem_limit_bytes=VMEM_LIMIT),
        name="ffn_swa",
    )(sinks, x, og, gate, gn, wo, nw, wg, wu, wd, nf, osw0, sq, sk, sk, sv, sv, bias)


def _sample_kernel(qkg_ref, v_ref, s0_ref,
                   qx_ref, kn_ref, vn_ref, ck_ref, cv_ref, sink_ref,
                   s_prev_ref, ck_prev_ref, cv_prev_ref,
                   og_ref, os_ref, sn_ref, cko_ref, cvo_ref, *, nt):
    del s_prev_ref, ck_prev_ref, cv_prev_ref
    TP = SUBLANES
    q, k, g, v = qkg_ref[0], qkg_ref[1], qkg_ref[2], v_ref[...]
    s0 = s0_ref[...]
    tt = lax.broadcasted_iota(jnp.int32, (1, TP, 1), 1)
    b = jnp.zeros_like(g)
    for s in range(nt):
        b = b + jnp.where(tt >= s, g[:, s:s + 1, :], 0.0)
    o = jnp.einsum('gtk,gkv->gtv', (q * jnp.exp2(b)).astype(BF16), s0.astype(BF16),
                   preferred_element_type=F32)
    for s in range(nt):
        dec = jnp.exp2(jnp.where(tt >= s, b - b[:, s:s + 1, :], NEG))
        a = jnp.sum(q * k[:, s:s + 1, :] * dec, axis=-1, keepdims=True)
        o = o + a * v[:, s:s + 1, :]
    og_ref[...] = o

    b_last = b[:, nt - 1:nt, :]
    d = jnp.exp2(b_last)
    d_hi = d.astype(BF16).astype(F32)
    x = jnp.where(tt < nt, k * jnp.exp2(b_last - b), jnp.where(tt == nt, d_hi, jnp.where(tt == nt + 1, d - d_hi, 0.0)))
    ones = jnp.where((tt == nt) | (tt == nt + 1), 1.0, 0.0)
    y = jnp.concatenate([jnp.where(tt < nt, v, 0.0), jnp.broadcast_to(ones, v.shape)], axis=2)
    r = lax.dot_general(x.astype(BF16), y.astype(BF16), (((1,), (1,)), ((0,), (0,))),
                        preferred_element_type=F32)
    sn_ref[...] = r[:, :, GLA_DV:] * s0 + r[:, :, :GLA_DV]

    W = WINDOW
    qx = qx_ref[...]
    ck, cv = ck_ref[...], cv_ref[...]
    kn, vn = kn_ref[...], vn_ref[...]
    sink = sink_ref[...] * LOG2E
    nrow = SWA_KV_HEADS * nt * SWA_GROUP
    r = lax.broadcasted_iota(jnp.int32, (1, nrow, W), 1)
    tq = (r % (nt * SWA_GROUP)) // SWA_GROUP
    j = lax.broadcasted_iota(jnp.int32, (1, nrow, W), 2)
    sc = jnp.einsum('bqd,bds->bqs', qx, ck.astype(BF16), preferred_element_type=F32)
    sc = jnp.where(j > tq, sc, NEG)
    tq1 = tq[:, :, 0:1]
    qf = qx.astype(F32)
    sn = [jnp.where(tq1 >= s, jnp.sum(qf * kn[:, s:s + 1, :], axis=-1, keepdims=True), NEG)
          for s in range(nt)]
    m = jnp.maximum(jnp.max(sc, axis=-1, keepdims=True), sink)
    for s in range(nt):
        m = jnp.maximum(m, sn[s])
    pc = jnp.exp2(sc - m)
    pn = [jnp.exp2(sn[s] - m) for s in range(nt)]
    den = jnp.sum(pc, axis=-1, keepdims=True) + jnp.exp2(sink - m)
    for s in range(nt):
        den = den + pn[s]
    o = jnp.einsum('bqs,bds->bqd', pc.astype(BF16), cv.astype(BF16), preferred_element_type=F32)
    for s in range(nt):
        o = o + pn[s] * vn[:, s:s + 1, :]
    os_ref[...] = o * (1.0 / den)

    wl = lax.broadcasted_iota(jnp.int32, (1, 1, W), 2)
    pr = lax.broadcasted_iota(jnp.int32, (1, 2 * nt, W), 1)
    pw = lax.broadcasted_iota(jnp.int32, (1, 2 * nt, W), 2)
    place = jnp.broadcast_to(jnp.where(pw == W - nt + pr % nt, 1.0, 0.0), (ck.shape[0], 2 * nt, W)).astype(BF16)
    for src, new, dst in ((ck, kn, cko_ref), (cv, vn, cvo_ref)):
        hi = new.astype(BF16).astype(F32)
        x = jnp.concatenate([hi, new - hi], axis=1).astype(BF16)
        placed = lax.dot_general(x, place, (((1,), (1,)), ((0,), (0,))), preferred_element_type=F32)
        dst[...] = jnp.where(wl >= W - nt, placed, pltpu.roll(src, W - nt, 2))


def _sample_mix(qkg4, v4, state, qx, kn, vn, ck, cv, sink_rows, carried, layer, nbatch, nt):
    bt = SAMPLE_BT
    G = bt * GLA_HEADS
    TP = SUBLANES
    nrow = SWA_KV_HEADS * nt * SWA_GROUP
    b3 = lambda a, c: pl.BlockSpec((G, a, c), lambda i: (i, 0, 0))
    s3 = lambda a, c: pl.BlockSpec((bt, a, c), lambda i: (i, 0, 0))
    lay_g = pl.BlockSpec((None, G, GLA_DK, GLA_DV), lambda i: (layer, i, 0, 0))
    lay_c = pl.BlockSpec((None, bt, SWA_KV, WINDOW), lambda i: (layer, i, 0, 0))
    ng = nbatch * GLA_HEADS
    n_in = 9
    return pl.pallas_call(
        functools.partial(_sample_kernel, nt=nt),
        grid=(nbatch // bt,),
        in_specs=[pl.BlockSpec((3, G, TP, GLA_DK), lambda i: (0, i, 0, 0)), b3(TP, GLA_DV), lay_g,
                  s3(nrow, SWA_KV), s3(nt, SWA_KV), s3(nt, SWA_KV), lay_c, lay_c,
                  pl.BlockSpec((1, nrow, 1), lambda i: (0, 0, 0))]
                 + [pl.BlockSpec(memory_space=pl.ANY)] * 3,
        out_specs=[b3(TP, GLA_DV), s3(nrow, SWA_KV), lay_g, lay_c, lay_c],
        out_shape=[jax.ShapeDtypeStruct((ng, TP, GLA_DV), F32),
                   jax.ShapeDtypeStruct((nbatch, nrow, SWA_KV), F32),
                   jax.ShapeDtypeStruct((DEPTH, ng, GLA_DK, GLA_DV), F32),
                   jax.ShapeDtypeStruct((DEPTH, nbatch, SWA_KV, WINDOW), F32),
                   jax.ShapeDtypeStruct((DEPTH, nbatch, SWA_KV, WINDOW), F32)],
        input_output_aliases={n_in + j: 2 + j for j in range(3)},
        compiler_params=pltpu.CompilerParams(dimension_semantics=("arbitrary",),
                                             vmem_limit_bytes=VMEM_LIMIT),
        name="sample_mix",
    )(qkg4, v4, state, qx, kn, vn, ck, cv, sink_rows, *carried)


def _rope_tables(pos):
    half = SWA_HEAD_DIM // 2
    inv = ROPE_THETA ** (-jnp.arange(half, dtype=F32) / half)
    ang = pos.astype(F32)[:, None] * inv[None, :]
    cos, sin = jnp.cos(ang), jnp.sin(ang)
    zero = jnp.zeros_like(sin)
    rep = LANES // SWA_HEAD_DIM
    return (jnp.tile(jnp.concatenate([cos, cos], axis=1), (1, rep)),
            jnp.tile(jnp.concatenate([-sin, zero], axis=1), (1, rep)),
            jnp.tile(jnp.concatenate([zero, sin], axis=1), (1, rep)))


def kernel(x_prompt, x_sample, state_gla, cache_swa_k, cache_swa_v, norm_attn, w_in, w_gk2, b_gk2,
           gla_norm, attn_sinks, w_o, norm_ffn, w_gate, w_up, w_down, norm_final):
    batch, seq, _ = x_prompt.shape
    nbatch, nt, _ = x_sample.shape
    n_s = nbatch * nt

    w_in_p = jnp.swapaxes(w_in, 1, 2).astype(BF16)
    wg2_p = jnp.concatenate([w_gk2, jnp.zeros((DEPTH, LANES - GATE_RANK, GLA_QK), w_gk2.dtype)],
                            axis=1).astype(BF16)
    bg2 = b_gk2.reshape(DEPTH, 1, GLA_QK)
    nw_a = norm_attn.reshape(DEPTH, 1, D_MODEL)
    nw_f = norm_ffn.reshape(DEPTH, 1, D_MODEL)
    wo_b, wg_b, wu_b, wd_b = (w.astype(BF16) for w in (w_o, w_gate, w_up, w_down))
    nf = norm_final.reshape(1, D_MODEL)
    gn = gla_norm.reshape(DEPTH, 1, GLA_DV)

    tabs_p = _rope_tables(jnp.arange(seq))
    tabs_s = _rope_tables(jnp.tile(PAST_LEN + jnp.arange(nt), TOK_TILE // nt))

    state_r = state_gla.reshape(DEPTH, nbatch * GLA_HEADS, GLA_DK, GLA_DV)
    pos_minor = lambda c: c.transpose(0, 1, 3, 4, 2).reshape(DEPTH, nbatch, SWA_KV, WINDOW)
    ck_r, cv_r = pos_minor(cache_swa_k), pos_minor(cache_swa_v)

    xp = x_prompt.reshape(batch * seq, D_MODEL)
    xs = x_sample.reshape(n_s, D_MODEL)
    sp_l, kp_l, vp_l = [], [], []
    carried = [jnp.zeros(state_r.shape, F32), jnp.zeros(ck_r.shape, F32), jnp.zeros(cv_r.shape, F32)]
    tpad = ((0, 0), (0, SUBLANES - nt), (0, 0))
    for l in range(DEPTH):
        final = l == DEPTH - 1
        q, k, g_hi, g_lo, v, gate, sq, sk, sv = _in_proj(xp, nw_a, w_in_p, wg2_p, bg2, tabs_p, l)
        og, s_fin = _gla_prompt(q, k, g_hi, g_lo, v, batch, seq)
        xp = _ffn_swa(xp, og, gate, gn, attn_sinks[l], sq, sk, sv, wo_b, nw_f, wg_b, wu_b, wd_b, nf, l, final, seq)
        sp_l.append(s_fin)
        last = lambda a: a.reshape(batch, seq, SWA_KV)[:, seq - WINDOW:].reshape(
            batch, WINDOW, SWA_KV_HEADS, SWA_HEAD_DIM)
        kp_l.append(last(sk))
        vp_l.append(last(sv))

        q, k, g_hi, g_lo, v, gate, sq, sk, sv = _in_proj(xs, nw_a, w_in_p, wg2_p, bg2, tabs_s, l)
        g = g_hi.astype(F32) + g_lo.astype(F32)

        def bhtk(a, w_):
            return a.reshape(nbatch, nt, GLA_HEADS, w_).transpose(0, 2, 1, 3).reshape(nbatch * GLA_HEADS, nt, w_)

        qkg4 = jnp.pad(jnp.stack([bhtk(a, GLA_DK) for a in (q, k, g)]), ((0, 0),) + tpad)
        v4 = jnp.pad(bhtk(v.astype(F32), GLA_DV), tpad)
        qk = sq.reshape(nbatch, nt, SWA_KV_HEADS, SWA_GROUP, SWA_HEAD_DIM).transpose(0, 2, 1, 3, 4)
        zq = jnp.zeros_like(qk[:, 0])
        qx = jnp.stack([jnp.concatenate([qk[:, 0], zq], axis=-1), jnp.concatenate([zq, qk[:, 1]], axis=-1)],
                       axis=1).reshape(nbatch, SWA_KV_HEADS * nt * SWA_GROUP, SWA_KV)
        sink_rows = jnp.broadcast_to(attn_sinks[l].reshape(SWA_KV_HEADS, 1, SWA_GROUP),
                                     (SWA_KV_HEADS, nt, SWA_GROUP)).reshape(1, -1, 1)
        og4, os4, *carried = _sample_mix(
            qkg4, v4, state_r, qx, sk.reshape(nbatch, nt, SWA_KV), sv.reshape(nbatch, nt, SWA_KV),
            ck_r, cv_r, sink_rows, carried, l, nbatch, nt)
        og = og4[:, :nt].reshape(nbatch, GLA_HEADS, nt, GLA_DV).transpose(0, 2, 1, 3).reshape(n_s, GLA_WIDTH)
        os5 = os4.reshape(nbatch, SWA_KV_HEADS, nt, SWA_GROUP, SWA_KV)
        osw = jnp.stack([os5[:, 0, :, :, :SWA_HEAD_DIM], os5[:, 1, :, :, SWA_HEAD_DIM:]], axis=2)
        osw = osw.reshape(n_s, SWA_Q)
        xs = _out_ffn(xs, og.astype(BF16), gate, gn, osw.astype(BF16), wo_b, nw_f, wg_b, wu_b, wd_b, nf, l, final)

    s_all, ck_all, cv_all = carried
    pos_major = lambda c: c.reshape(DEPTH, nbatch, SWA_KV_HEADS, SWA_HEAD_DIM, WINDOW).transpose(0, 1, 4, 2, 3)
    return (xp.reshape(batch, seq, D_MODEL), xs.reshape(nbatch, nt, D_MODEL),
            jnp.stack(sp_l), jnp.stack(kp_l), jnp.stack(vp_l),
            s_all.reshape(DEPTH, nbatch, GLA_HEADS, GLA_DK, GLA_DV), pos_major(ck_all), pos_major(cv_all))
```

```python
import functools

import jax
import jax.numpy as jnp
import numpy as np
from jax import lax
from jax.experimental import pallas as pl
from jax.experimental.pallas import tpu as pltpu

F32 = jnp.float32
BF16 = jnp.bfloat16

D_MODEL = 1024
DEPTH = 4
PAST_LEN = 8192
GLA_HEADS = 4
GLA_DV = 128
GLA_DK = 64
GLA_QK = GLA_HEADS * GLA_DK
GLA_WIDTH = GLA_HEADS * GLA_DV
GATE_RANK = 16
GATE_NORMALIZER = 16.0
SWA_HEAD_DIM = 64
SWA_HEADS = 8
SWA_KV_HEADS = 2
SWA_GROUP = 4
SWA_Q = SWA_HEADS * SWA_HEAD_DIM
SWA_KV = SWA_KV_HEADS * SWA_HEAD_DIM
WINDOW = 128
ROPE_THETA = 10000.0
D_FF = 2816
NORM_EPS = 1e-6

LANES = 128
SUBLANES = 8
VMEM_LIMIT = 56 * 1024 * 1024

C_Q, C_K, C_V, C_GG, C_SQ, C_SK, C_SV, C_GA = 0, 256, 512, 1024, 1536, 2048, 2176, 2304
IN_COLS_P = C_GA + LANES

TOK_TILE = 512
IN_TILE = 1024
GLA_CHUNK = 64
GLA_SUB = 8
GLA_TILE = 1024
GLA_UNROLL = 8
FF_SPLITS = (0, 768, 1536, 2304, D_FF)
SWA_ROWS = 32
SAMPLE_BT = 16
NEG = -1e30
LOG2E = 1.4426950408889634


def _rms(x, w):
    ms = jnp.mean(x * x, axis=-1, keepdims=True)
    return x * lax.rsqrt(ms + NORM_EPS) * w


def _sigmoid(x):
    return 1.0 / (1.0 + jnp.exp(-x))


def _dot(a, b):
    return jnp.dot(a, b, preferred_element_type=F32)


def _dot_nt(a, b):
    return lax.dot_general(a, b, (((1,), (1,)), ((), ())), preferred_element_type=F32)


def _in_proj_kernel(x_ref, nw_ref, w_ref, wg2_ref, bg2_ref, cos_ref, sa_ref, sb_ref,
                    q_ref, k_ref, ghi_ref, glo_ref, v_ref, gate_ref, sq_ref, sk_ref, sv_ref):
    parts = [slice(r0, r0 + TOK_TILE) for r0 in range(0, x_ref.shape[0], TOK_TILE)]
    hs = [_rms(x_ref[r, :], nw_ref[...]).astype(BF16) for r in parts]
    ga0 = C_GG
    zs = [jnp.concatenate([_dot_nt(h, w_ref[:ga0, :]), _dot_nt(h, w_ref[ga0 + GATE_RANK:, :]),
                           _dot_nt(h, w_ref[ga0:ga0 + LANES, :])], axis=1) for h in hs]

    for r, z in zip(parts, zs):
        q_ref[r, :] = z[:, C_Q:C_K] * (GLA_DK ** -0.5)
        k_ref[r, :] = z[:, C_K:C_V]
        v_ref[r, :] = z[:, C_V:C_GG].astype(BF16)
        gg = z[:, C_GG:C_SQ]
        gate_ref[r, :] = gg * _sigmoid(gg)
        gx = _dot(z[:, C_GA:IN_COLS_P].astype(BF16), wg2_ref[...]) + bg2_ref[...]
        g2 = (jnp.minimum(gx, 0.0) - jnp.log1p(jnp.exp(-jnp.abs(gx)))) * (LOG2E / GATE_NORMALIZER)
        g_hi = g2.astype(BF16)
        ghi_ref[r, :] = g_hi
        glo_ref[r, :] = (g2 - g_hi.astype(F32)).astype(BF16)

        cos, sa, sb = cos_ref[r, :], sa_ref[r, :], sb_ref[r, :]

        def rope(y):
            return y * cos + pltpu.roll(y, LANES - 32, 1) * sa + pltpu.roll(y, 32, 1) * sb

        for j in range(SWA_Q // LANES):
            sq_ref[r, j * LANES:(j + 1) * LANES] = (
                rope(z[:, C_SQ + j * LANES:C_SQ + (j + 1) * LANES]) * (SWA_HEAD_DIM ** -0.5 * LOG2E)).astype(BF16)
        sk_ref[r, :] = rope(z[:, C_SK:C_SV])
        sv_ref[r, :] = z[:, C_SV:C_GA]


def _in_proj(x, nw, w, wg2, bg2, tabs, layer):
    n = x.shape[0]
    tile = min(n, IN_TILE)
    nt = n // tile
    cos, sa, sb = tabs
    ntab = cos.shape[0] // tile
    row = lambda w_: pl.BlockSpec((tile, w_), lambda i: (i, 0))
    tab = pl.BlockSpec((tile, LANES), lambda i: (i % ntab, 0))
    const = lambda shp: pl.BlockSpec((None,) + shp, lambda i: (layer, 0, 0), pipeline_mode=pl.Buffered(1))
    outs = [(GLA_QK, F32), (GLA_QK, F32), (GLA_QK, BF16), (GLA_QK, BF16), (GLA_WIDTH, BF16), (GLA_WIDTH, F32),
            (SWA_Q, BF16), (SWA_KV, F32), (SWA_KV, F32)]
    return pl.pallas_call(
        _in_proj_kernel,
        grid=(nt,),
        in_specs=[row(D_MODEL), const((1, D_MODEL)), const((w.shape[1], D_MODEL)),
                  const((LANES, GLA_QK)), const((1, GLA_QK)), tab, tab, tab],
        out_specs=[row(w_) for w_, _ in outs],
        out_shape=[jax.ShapeDtypeStruct((n, w_), dt) for w_, dt in outs],
        compiler_params=pltpu.CompilerParams(dimension_semantics=("arbitrary",),
                                             vmem_limit_bytes=VMEM_LIMIT),
        name="in_proj",
    )(x, nw, w, wg2, bg2, cos, sa, sb)


def _gla_kernel(q_ref, k_ref, ghi_ref, glo_ref, v_ref, bias_ref, o_ref, sout_ref, s_ref):
    ti = pl.program_id(1)
    C, SB = GLA_CHUNK, GLA_SUB
    NB = C // SB
    NP = GLA_HEADS // 2

    @pl.when(ti == 0)
    def _():
        s_ref[...] = jnp.zeros_like(s_ref)

    ltri = (lax.broadcasted_iota(jnp.int32, (C, 2 * C), 0) >= lax.broadcasted_iota(jnp.int32, (C, 2 * C), 1) % C
            ).astype(F32).astype(BF16)
    lane = lax.broadcasted_iota(jnp.int32, (1, LANES), 1)
    mask_a = (lane < GLA_DK).astype(F32)
    mask_b = 1.0 - mask_a
    col8 = lax.broadcasted_iota(jnp.int32, (SB, LANES), 1) % C
    colblk, colmod = col8 // SB, col8 % SB
    zk = jnp.zeros((GLA_DK, GLA_DV), BF16)
    off1 = [sum(C - SB * (jj + 1) for jj in range(j)) for j in range(NB - 1)]

    def sub_bcast(x, j):
        x3 = x.reshape(NB, SB, LANES)
        return jnp.broadcast_to(x3[:, j:j + 1, :], (NB, SB, LANES)).reshape(C, LANES)

    def heads(x):
        return jnp.concatenate([x * mask_a, x * mask_b], axis=0).astype(BF16)

    def body(it, carry):
        rows = [pl.ds(pl.multiple_of((it * GLA_UNROLL + cc) * C, C), C) for cc in range(GLA_UNROLL)]
        units = [(cc, p) for cc in range(GLA_UNROLL) for p in range(NP)]
        kl = [slice(p * LANES, (p + 1) * LANES) for p in range(NP)]
        vl = [slice(h * GLA_DV, (h + 1) * GLA_DV) for h in range(GLA_HEADS)]

        b_all = [_dot(ltri, jnp.concatenate([ghi_ref[rows[cc], :], glo_ref[rows[cc], :]], axis=0))
                 for cc in range(GLA_UNROLL)]

        qe, outs, dec, upd = {}, {}, {}, {}
        for cc, p in units:
            q, k, b = q_ref[rows[cc], kl[p]], k_ref[rows[cc], kl[p]], b_all[cc][:, kl[p]]
            b_last = b[C - 1:C, :]
            k1 = k * jnp.exp2(sub_bcast(b, SB - 1) - b)
            lhs1 = jnp.concatenate(
                [q[SB * (j + 1):, :] * jnp.exp2(b[SB * (j + 1):, :] - b[SB * j + SB - 1:SB * j + SB, :])
                 for j in range(NB - 1)], axis=0)
            lhs2 = jnp.concatenate(
                [q * jnp.exp2((b - sub_bcast(b, jj)) + jnp.tile(bias_ref[jj], (NB, 1))) for jj in range(SB)],
                axis=0)
            qe[cc, p] = (q * jnp.exp2(b)).astype(BF16)
            outs[cc, p] = (_dot_nt(lhs1.astype(BF16), heads(k1)), _dot_nt(lhs2.astype(BF16), heads(k)))
            kd = k * jnp.exp2(b_last - b)
            mt = jnp.concatenate([kd, jnp.broadcast_to(b_last, (C, LANES))], axis=0).T
            kdt = mt[:, :C].astype(BF16)
            dec[cc, p] = jnp.exp2(mt[:, C:C + 1])
            upd[cc, p] = (_dot(kdt[:GLA_DK], v_ref[rows[cc], vl[2 * p]]),
                          _dot(kdt[GLA_DK:], v_ref[rows[cc], vl[2 * p + 1]]))

        lo = {}
        for u in units:
            out1, out2 = outs[u]
            a_rows = []
            for i in range(NB):
                d = out2[(SB - 1) * C + SB * i:(SB - 1) * C + SB * (i + 1), :]
                for jj in range(SB - 2, -1, -1):
                    d = jnp.where(colmod == jj, out2[jj * C + SB * i:jj * C + SB * (i + 1), :], d)
                acc = jnp.where(colblk == i, d, 0.0)
                for j in range(i):
                    r0 = off1[j] + SB * (i - j - 1)
                    acc = jnp.where(colblk == j, out1[r0:r0 + SB, :], acc)
                a_rows.append(acc)
            a = jnp.concatenate(a_rows, axis=0)
            lo[u] = jnp.concatenate([qe[u], a.astype(BF16)], axis=1)

        state = [s_ref[h] for h in range(GLA_HEADS)]
        o_all = {}
        for cc in range(GLA_UNROLL):
            for p in range(NP):
                ha, hb = 2 * p, 2 * p + 1
                v_a, v_b = v_ref[rows[cc], vl[ha]], v_ref[rows[cc], vl[hb]]
                o_all[cc, ha] = _dot(lo[cc, p], jnp.concatenate([state[ha].astype(BF16), zk, v_a, zk], axis=0))
                o_all[cc, hb] = _dot(lo[cc, p], jnp.concatenate([zk, state[hb].astype(BF16), zk, v_b], axis=0))
                state[ha] = dec[cc, p][:GLA_DK] * state[ha] + upd[cc, p][0]
                state[hb] = dec[cc, p][GLA_DK:] * state[hb] + upd[cc, p][1]
        for h in range(GLA_HEADS):
            s_ref[h] = state[h]

        for cc in range(GLA_UNROLL):
            for h in range(GLA_HEADS):
                o_ref[rows[cc], vl[h]] = o_all[cc, h].astype(BF16)
        return carry

    lax.fori_loop(0, GLA_TILE // (C * GLA_UNROLL), body, 0)

    @pl.when(ti == pl.num_programs(1) - 1)
    def _():
        sout_ref[0] = s_ref[...]


def _gla_prompt(q, k, g_hi, g_lo, v, batch, seq):
    nt = seq // GLA_TILE
    blk = lambda w_: pl.BlockSpec((GLA_TILE, w_), lambda b, t: (b * nt + t, 0))
    r = np.arange(GLA_SUB)
    bias = np.where(r[None, :, None] >= r[:, None, None], 0.0, NEG) * np.ones((1, 1, LANES))
    return pl.pallas_call(
        _gla_kernel,
        grid=(batch, nt),
        in_specs=[blk(GLA_QK), blk(GLA_QK), blk(GLA_QK), blk(GLA_QK), blk(GLA_WIDTH),
                  pl.BlockSpec((GLA_SUB, GLA_SUB, LANES), lambda b, t: (0, 0, 0))],
        out_specs=[blk(GLA_WIDTH),
                   pl.BlockSpec((1, GLA_HEADS, GLA_DK, GLA_DV), lambda b, t: (b, 0, 0, 0))],
        out_shape=[jax.ShapeDtypeStruct((batch * seq, GLA_WIDTH), BF16),
                   jax.ShapeDtypeStruct((batch, GLA_HEADS, GLA_DK, GLA_DV), F32)],
        scratch_shapes=[pltpu.VMEM((GLA_HEADS, GLA_DK, GLA_DV), F32)],
        compiler_params=pltpu.CompilerParams(
            dimension_semantics=("arbitrary", "arbitrary"), vmem_limit_bytes=VMEM_LIMIT),
        name="gla_prompt",
    )(q, k, g_hi, g_lo, v, jnp.asarray(bias, F32))


def _out_ffn_kernel(x_ref, og_ref, gate_ref, gn_ref, os_ref, wo_ref, nw_ref, wg_ref, wu_ref, wd_ref, nf_ref,
                    y_ref, *, final):
    gn = gn_ref[...]
    og = jnp.concatenate([_rms(og_ref[:, h * GLA_DV:(h + 1) * GLA_DV].astype(F32), gn)
                          for h in range(GLA_HEADS)], axis=1) * gate_ref[...]
    x = (x_ref[...] + _dot(og.astype(BF16), wo_ref[:GLA_WIDTH, :]) + _dot(os_ref[...], wo_ref[GLA_WIDTH:, :]))
    h = _rms(x, nw_ref[...]).astype(BF16)
    a = _dot(h, wg_ref[...])
    u = _dot(h, wu_ref[...])
    acc = x + _dot((a * _sigmoid(a) * u).astype(BF16), wd_ref[...])
    if final:
        acc = _rms(acc, nf_ref[...])
    y_ref[...] = acc


def _out_ffn(x, og, gate, gn, osw, wo, nw, wg, wu, wd, nf, layer, final):
    n = x.shape[0]
    row = lambda w_: pl.BlockSpec((TOK_TILE, w_), lambda i: (i, 0))
    const = lambda shp: pl.BlockSpec((None,) + shp, lambda i: (layer, 0, 0), pipeline_mode=pl.Buffered(1))
    return pl.pallas_call(
        functools.partial(_out_ffn_kernel, final=final),
        grid=(n // TOK_TILE,),
        in_specs=[row(D_MODEL), row(GLA_WIDTH), row(GLA_WIDTH), const((1, GLA_DV)), row(SWA_Q),
                  const((D_MODEL, D_MODEL)), const((1, D_MODEL)),
                  const((D_MODEL, D_FF)), const((D_MODEL, D_FF)), const((D_FF, D_MODEL)),
                  pl.BlockSpec((1, D_MODEL), lambda i: (0, 0))],
        out_specs=row(D_MODEL),
        out_shape=jax.ShapeDtypeStruct((n, D_MODEL), F32),
        compiler_params=pltpu.CompilerParams(dimension_semantics=("arbitrary",),
                                             vmem_limit_bytes=VMEM_LIMIT),
        name="out_ffn",
    )(x, og, gate, gn, osw, wo, nw, wg, wu, wd, nf)


def _swa_setup(kp_ref, kc_ref, vp_ref, vc_ref, ke_scr, ve_scr):
    kk = jnp.concatenate([kp_ref[...], kc_ref[...]], axis=0)
    vv = jnp.concatenate([vp_ref[...], vc_ref[...]], axis=0)
    lo = lax.broadcasted_iota(jnp.int32, kk.shape, 1) < SWA_HEAD_DIM
    for x, scr in ((kk, ke_scr), (vv, ve_scr)):
        x_sw = pltpu.roll(x, SWA_HEAD_DIM, 1)
        scr[0, 0] = jnp.where(lo, x, 0.0).astype(BF16)
        scr[0, 1] = jnp.where(lo, 0.0, x_sw).astype(BF16)
        scr[1, 0] = jnp.where(lo, x_sw, 0.0).astype(BF16)
        scr[1, 1] = jnp.where(lo, 0.0, x).astype(BF16)


def _swa_scores(j, q_ref, ke_scr, s_scr):
    W = WINDOW
    for p in range(SWA_HEADS // 2):
        kv = p // (SWA_GROUP // 2)
        q = q_ref[j * W:(j + 1) * W, p * LANES:(p + 1) * LANES]
        for hh in range(2):
            s_scr[p, :, hh * 2 * W:(hh + 1) * 2 * W] = _dot_nt(q, ke_scr[kv, hh, j * W:(j + 2) * W, :])


def _swa_softmax(sink_ref, bias_blk, s_scr, p_scr, r_scr):
    W = WINDOW
    lane_q = lax.broadcasted_iota(jnp.int32, (SWA_ROWS, LANES), 1) < SWA_HEAD_DIM
    for p in range(SWA_HEADS // 2):
        for r0 in range(0, W, SWA_ROWS):
            rs = slice(r0, r0 + SWA_ROWS)
            rden = []
            for hh in range(2):
                cs = slice(hh * 2 * W, (hh + 1) * 2 * W)
                sink = sink_ref[2 * p + hh] * LOG2E
                sh = s_scr[p, rs, cs] + bias_blk[rs, :]
                m = jnp.maximum(jnp.max(sh, axis=-1, keepdims=True), sink)
                e = jnp.exp2(sh - m)
                den = jnp.sum(e, axis=-1, keepdims=True) + jnp.exp2(sink - m)
                p_scr[p, rs, cs] = e.astype(BF16)
                rden.append(1.0 / den)
            r_scr[p, rs, :] = jnp.where(lane_q, rden[0], rden[1])


def _swa_values(j, p_scr, ve_scr, r_scr, o_ref):
    W = WINDOW
    for p in range(SWA_HEADS // 2):
        kv = p // (SWA_GROUP // 2)
        o = (_dot(p_scr[p, :, :2 * W], ve_scr[kv, 0, j * W:(j + 2) * W, :])
             + _dot(p_scr[p, :, 2 * W:], ve_scr[kv, 1, j * W:(j + 2) * W, :])) * r_scr[p]
        o_ref[j * W:(j + 1) * W, p * LANES:(p + 1) * LANES] = o.astype(BF16)


def _ffn_swa_kernel(sink_ref, x_ref, og_ref, gate_ref, gn_ref, wo_ref, nw_ref, wg_ref, wu_ref, wd_ref, nf_ref,
                    osw0_ref, q_ref, kp_ref, kc_ref, vp_ref, vc_ref, bias_ref, y_ref,
                    osw_scr, ke_scr, ve_scr, s_scr, p_scr, r_scr, *, final, tiles_per_seq, n_tiles):
    s = pl.program_id(0)
    nblk = TOK_TILE // WINDOW

    @pl.when(s == 0)
    def _():
        osw_scr[0] = osw0_ref[...]

    _swa_setup(kp_ref, kc_ref, vp_ref, vc_ref, ke_scr, ve_scr)
    t_seq = jnp.minimum(s + 1, n_tiles - 1) % tiles_per_seq
    osw_cur = osw_scr.at[(s + 1) % 2]

    gn = gn_ref[...]
    og = jnp.concatenate([_rms(og_ref[:, h * GLA_DV:(h + 1) * GLA_DV].astype(F32), gn)
                          for h in range(GLA_HEADS)], axis=1) * gate_ref[...]
    x = (x_ref[...] + _dot(og.astype(BF16), wo_ref[:GLA_WIDTH, :])
         + _dot(osw_scr[s % 2], wo_ref[GLA_WIDTH:, :]))
    h = _rms(x, nw_ref[...]).astype(BF16)

    t_parts = []
    for j in range(nblk):
        c0, c1 = FF_SPLITS[j], FF_SPLITS[j + 1]
        _swa_scores(j, q_ref, ke_scr, s_scr)
        a = _dot(h, wg_ref[:, c0:c1])
        u = _dot(h, wu_ref[:, c0:c1])
        t_parts.append((a * _sigmoid(a) * u).astype(BF16))
        _swa_softmax(sink_ref, bias_ref.at[jnp.minimum(t_seq * nblk + j, 1)], s_scr, p_scr, r_scr)
        _swa_values(j, p_scr, ve_scr, r_scr, osw_cur)
    acc = x + _dot(jnp.concatenate(t_parts, axis=1), wd_ref[...])
    if final:
        acc = _rms(acc, nf_ref[...])
    y_ref[...] = acc


def _swa_first_kernel(sink_ref, q_ref, kp_ref, kc_ref, vp_ref, vc_ref, bias_ref, o_ref,
                      ke_scr, ve_scr, s_scr, p_scr, r_scr):
    _swa_setup(kp_ref, kc_ref, vp_ref, vc_ref, ke_scr, ve_scr)
    for j in range(TOK_TILE // WINDOW):
        _swa_scores(j, q_ref, ke_scr, s_scr)
        _swa_softmax(sink_ref, bias_ref.at[min(j, 1)], s_scr, p_scr, r_scr)
        _swa_values(j, p_scr, ve_scr, r_scr, o_ref)


def _ffn_swa(x, og, gate, gn, sinks, sq, sk, sv, wo, nw, wg, wu, wd, nf, layer, final, seq):
    n = x.shape[0]
    W = WINDOW
    n_tiles = n // TOK_TILE
    tiles_per_seq = seq // TOK_TILE
    nblk = TOK_TILE // W
    ffn_i = lambda s: (s, 0)
    swa_i = lambda s: (jnp.minimum(s + 1, n_tiles - 1), 0)

    def prev_i(s):
        t = jnp.minimum(s + 1, n_tiles - 1)
        return (t - t % tiles_per_seq) * nblk + jnp.maximum((t % tiles_per_seq) * nblk - 1, 0), 0

    row = lambda w_, im: pl.BlockSpec((TOK_TILE, w_), im)
    const = lambda shp: pl.BlockSpec((None,) + shp, lambda s: (layer, 0, 0), pipeline_mode=pl.Buffered(1))
    first = lambda s: (0, 0)
    qi, kj = np.arange(W)[:, None], np.arange(2 * W)[None, :]
    band = (kj > qi) & (kj <= qi + W)
    bias = jnp.asarray(np.stack([np.where(band & (kj >= W), 0.0, NEG), np.where(band, 0.0, NEG)]), F32)
    bias_spec = pl.BlockSpec((2, W, 2 * W), lambda s: (0, 0, 0))
    ext_scr = pltpu.VMEM((SWA_KV_HEADS, 2, W + TOK_TILE, LANES), BF16)
    npair = SWA_HEADS // 2
    swa_scr = [ext_scr, ext_scr, pltpu.VMEM((npair, W, 4 * W), F32), pltpu.VMEM((npair, W, 4 * W), BF16),
               pltpu.VMEM((npair, W, LANES), F32)]
    osw0 = pl.pallas_call(
        _swa_first_kernel,
        grid=(1,),
        in_specs=[pl.BlockSpec(memory_space=pltpu.SMEM), row(SWA_Q, first),
                  pl.BlockSpec((W, SWA_KV), first), row(SWA_KV, first),
                  pl.BlockSpec((W, SWA_KV), first), row(SWA_KV, first), bias_spec],
        out_specs=row(SWA_Q, first),
        out_shape=jax.ShapeDtypeStruct((TOK_TILE, SWA_Q), BF16),
        scratch_shapes=swa_scr,
        compiler_params=pltpu.CompilerParams(dimension_semantics=("arbitrary",), vmem_limit_bytes=VMEM_LIMIT),
        name="swa_first",
    )(sinks, sq, sk, sk, sv, sv, bias)
    return pl.pallas_call(
        functools.partial(_ffn_swa_kernel, final=final, tiles_per_seq=tiles_per_seq, n_tiles=n_tiles),
        grid=(n_tiles,),
        in_specs=[pl.BlockSpec(memory_space=pltpu.SMEM),
                  row(D_MODEL, ffn_i), row(GLA_WIDTH, ffn_i), row(GLA_WIDTH, ffn_i), const((1, GLA_DV)),
                  const((D_MODEL, D_MODEL)), const((1, D_MODEL)),
                  const((D_MODEL, D_FF)), const((D_MODEL, D_FF)), const((D_FF, D_MODEL)),
                  pl.BlockSpec((1, D_MODEL), lambda s: (0, 0)),
                  row(SWA_Q, first), row(SWA_Q, swa_i),
                  pl.BlockSpec((W, SWA_KV), prev_i), row(SWA_KV, swa_i),
                  pl.BlockSpec((W, SWA_KV), prev_i), row(SWA_KV, swa_i), bias_spec],
        out_specs=row(D_MODEL, ffn_i),
        out_shape=jax.ShapeDtypeStruct((n, D_MODEL), F32),
        scratch_shapes=[pltpu.VMEM((2, TOK_TILE, SWA_Q), BF16)] + swa_scr,
        compiler_params=pltpu.CompilerParams(dimension_semantics=("arbitrary",),
                                             vmem_limit_bytes=VMEM_LIMIT),
        name="ffn_swa",
    )(sinks, x, og, gate, gn, wo, nw, wg, wu, wd, nf, osw0, sq, sk, sk, sv, sv, bias)


def _sample_kernel(qkg_ref, v_ref, s0_ref,
                   qx_ref, kn_ref, vn_ref, ck_ref, cv_ref, sink_ref,
                   s_prev_ref, ck_prev_ref, cv_prev_ref,
                   og_ref, os_ref, sn_ref, cko_ref, cvo_ref, *, nt):
    del s_prev_ref, ck_prev_ref, cv_prev_ref
    TP = SUBLANES
    q, k, g, v = qkg_ref[0], qkg_ref[1], qkg_ref[2], v_ref[...]
    s0 = s0_ref[...]
    tt = lax.broadcasted_iota(jnp.int32, (1, TP, 1), 1)
    b = jnp.zeros_like(g)
    for s in range(nt):
        b = b + jnp.where(tt >= s, g[:, s:s + 1, :], 0.0)
    o = jnp.einsum('gtk,gkv->gtv', (q * jnp.exp2(b)).astype(BF16), s0.astype(BF16),
                   preferred_element_type=F32)
    for s in range(nt):
        dec = jnp.exp2(jnp.where(tt >= s, b - b[:, s:s + 1, :], NEG))
        a = jnp.sum(q * k[:, s:s + 1, :] * dec, axis=-1, keepdims=True)
        o = o + a * v[:, s:s + 1, :]
    og_ref[...] = o

    b_last = b[:, nt - 1:nt, :]
    d = jnp.exp2(b_last)
    d_hi = d.astype(BF16).astype(F32)
    x = jnp.where(tt < nt, k * jnp.exp2(b_last - b), jnp.where(tt == nt, d_hi, jnp.where(tt == nt + 1, d - d_hi, 0.0)))
    ones = jnp.where((tt == nt) | (tt == nt + 1), 1.0, 0.0)
    y = jnp.concatenate([jnp.where(tt < nt, v, 0.0), jnp.broadcast_to(ones, v.shape)], axis=2)
    r = lax.dot_general(x.astype(BF16), y.astype(BF16), (((1,), (1,)), ((0,), (0,))),
                        preferred_element_type=F32)
    sn_ref[...] = r[:, :, GLA_DV:] * s0 + r[:, :, :GLA_DV]

    W = WINDOW
    qx = qx_ref[...]
    ck, cv = ck_ref[...], cv_ref[...]
    kn, vn = kn_ref[...], vn_ref[...]
    sink = sink_ref[...] * LOG2E
    nrow = SWA_KV_HEADS * nt * SWA_GROUP
    r = lax.broadcasted_iota(jnp.int32, (1, nrow, W), 1)
    tq = (r % (nt * SWA_GROUP)) // SWA_GROUP
    j = lax.broadcasted_iota(jnp.int32, (1, nrow, W), 2)
    sc = jnp.einsum('bqd,bds->bqs', qx, ck.astype(BF16), preferred_element_type=F32)
    sc = jnp.where(j > tq, sc, NEG)
    tq1 = tq[:, :, 0:1]
    qf = qx.astype(F32)
    sn = [jnp.where(tq1 >= s, jnp.sum(qf * kn[:, s:s + 1, :], axis=-1, keepdims=True), NEG)
          for s in range(nt)]
    m = jnp.maximum(jnp.max(sc, axis=-1, keepdims=True), sink)
    for s in range(nt):
        m = jnp.maximum(m, sn[s])
    pc = jnp.exp2(sc - m)
    pn = [jnp.exp2(sn[s] - m) for s in range(nt)]
    den = jnp.sum(pc, axis=-1, keepdims=True) + jnp.exp2(sink - m)
    for s in range(nt):
        den = den + pn[s]
    o = jnp.einsum('bqs,bds->bqd', pc.astype(BF16), cv.astype(BF16), preferred_element_type=F32)
    for s in range(nt):
        o = o + pn[s] * vn[:, s:s + 1, :]
    os_ref[...] = o * (1.0 / den)

    wl = lax.broadcasted_iota(jnp.int32, (1, 1, W), 2)
    pr = lax.broadcasted_iota(jnp.int32, (1, 2 * nt, W), 1)
    pw = lax.broadcasted_iota(jnp.int32, (1, 2 * nt, W), 2)
    place = jnp.broadcast_to(jnp.where(pw == W - nt + pr % nt, 1.0, 0.0), (ck.shape[0], 2 * nt, W)).astype(BF16)
    for src, new, dst in ((ck, kn, cko_ref), (cv, vn, cvo_ref)):
        hi = new.astype(BF16).astype(F32)
        x = jnp.concatenate([hi, new - hi], axis=1).astype(BF16)
        placed = lax.dot_general(x, place, (((1,), (1,)), ((0,), (0,))), preferred_element_type=F32)
        dst[...] = jnp.where(wl >= W - nt, placed, pltpu.roll(src, W - nt, 2))


def _sample_mix(qkg4, v4, state, qx, kn, vn, ck, cv, sink_rows, carried, layer, nbatch, nt):
    bt = SAMPLE_BT
    G = bt * GLA_HEADS
    TP = SUBLANES
    nrow = SWA_KV_HEADS * nt * SWA_GROUP
    b3 = lambda a, c: pl.BlockSpec((G, a, c), lambda i: (i, 0, 0))
    s3 = lambda a, c: pl.BlockSpec((bt, a, c), lambda i: (i, 0, 0))
    lay_g = pl.BlockSpec((None, G, GLA_DK, GLA_DV), lambda i: (layer, i, 0, 0))
    lay_c = pl.BlockSpec((None, bt, SWA_KV, WINDOW), lambda i: (layer, i, 0, 0))
    ng = nbatch * GLA_HEADS
    n_in = 9
    return pl.pallas_call(
        functools.partial(_sample_kernel, nt=nt),
        grid=(nbatch // bt,),
        in_specs=[pl.BlockSpec((3, G, TP, GLA_DK), lambda i: (0, i, 0, 0)), b3(TP, GLA_DV), lay_g,
                  s3(nrow, SWA_KV), s3(nt, SWA_KV), s3(nt, SWA_KV), lay_c, lay_c,
                  pl.BlockSpec((1, nrow, 1), lambda i: (0, 0, 0))]
                 + [pl.BlockSpec(memory_space=pl.ANY)] * 3,
        out_specs=[b3(TP, GLA_DV), s3(nrow, SWA_KV), lay_g, lay_c, lay_c],
        out_shape=[jax.ShapeDtypeStruct((ng, TP, GLA_DV), F32),
                   jax.ShapeDtypeStruct((nbatch, nrow, SWA_KV), F32),
                   jax.ShapeDtypeStruct((DEPTH, ng, GLA_DK, GLA_DV), F32),
                   jax.ShapeDtypeStruct((DEPTH, nbatch, SWA_KV, WINDOW), F32),
                   jax.ShapeDtypeStruct((DEPTH, nbatch, SWA_KV, WINDOW), F32)],
        input_output_aliases={n_in + j: 2 + j for j in range(3)},
        compiler_params=pltpu.CompilerParams(dimension_semantics=("arbitrary",),
                                             vmem_limit_bytes=VMEM_LIMIT),
        name="sample_mix",
    )(qkg4, v4, state, qx, kn, vn, ck, cv, sink_rows, *carried)


def _rope_tables(pos):
    half = SWA_HEAD_DIM // 2
    inv = ROPE_THETA ** (-jnp.arange(half, dtype=F32) / half)
    ang = pos.astype(F32)[:, None] * inv[None, :]
    cos, sin = jnp.cos(ang), jnp.sin(ang)
    zero = jnp.zeros_like(sin)
    rep = LANES // SWA_HEAD_DIM
    return (jnp.tile(jnp.concatenate([cos, cos], axis=1), (1, rep)),
            jnp.tile(jnp.concatenate([-sin, zero], axis=1), (1, rep)),
            jnp.tile(jnp.concatenate([zero, sin], axis=1), (1, rep)))


def kernel(x_prompt, x_sample, state_gla, cache_swa_k, cache_swa_v, norm_attn, w_in, w_gk2, b_gk2,
           gla_norm, attn_sinks, w_o, norm_ffn, w_gate, w_up, w_down, norm_final):
    batch, seq, _ = x_prompt.shape
    nbatch, nt, _ = x_sample.shape
    n_s = nbatch * nt

    w_in_p = jnp.swapaxes(w_in, 1, 2).astype(BF16)
    wg2_p = jnp.concatenate([w_gk2, jnp.zeros((DEPTH, LANES - GATE_RANK, GLA_QK), w_gk2.dtype)],
                            axis=1).astype(BF16)
    bg2 = b_gk2.reshape(DEPTH, 1, GLA_QK)
    nw_a = norm_attn.reshape(DEPTH, 1, D_MODEL)
    nw_f = norm_ffn.reshape(DEPTH, 1, D_MODEL)
    wo_b, wg_b, wu_b, wd_b = (w.astype(BF16) for w in (w_o, w_gate, w_up, w_down))
    nf = norm_final.reshape(1, D_MODEL)
    gn = gla_norm.reshape(DEPTH, 1, GLA_DV)

    tabs_p = _rope_tables(jnp.arange(seq))
    tabs_s = _rope_tables(jnp.tile(PAST_LEN + jnp.arange(nt), TOK_TILE // nt))

    state_r = state_gla.reshape(DEPTH, nbatch * GLA_HEADS, GLA_DK, GLA_DV)
    pos_minor = lambda c: c.transpose(0, 1, 3, 4, 2).reshape(DEPTH, nbatch, SWA_KV, WINDOW)
    ck_r, cv_r = pos_minor(cache_swa_k), pos_minor(cache_swa_v)

    xp = x_prompt.reshape(batch * seq, D_MODEL)
    xs = x_sample.reshape(n_s, D_MODEL)
    sp_l, kp_l, vp_l = [], [], []
    carried = [jnp.zeros(state_r.shape, F32), jnp.zeros(ck_r.shape, F32), jnp.zeros(cv_r.shape, F32)]
    tpad = ((0, 0), (0, SUBLANES - nt), (0, 0))
    for l in range(DEPTH):
        final = l == DEPTH - 1
        q, k, g_hi, g_lo, v, gate, sq, sk, sv = _in_proj(xp, nw_a, w_in_p, wg2_p, bg2, tabs_p, l)
        og, s_fin = _gla_prompt(q, k, g_hi, g_lo, v, batch, seq)
        xp = _ffn_swa(xp, og, gate, gn, attn_sinks[l], sq, sk, sv, wo_b, nw_f, wg_b, wu_b, wd_b, nf, l, final, seq)
        sp_l.append(s_fin)
        last = lambda a: a.reshape(batch, seq, SWA_KV)[:, seq - WINDOW:].reshape(
            batch, WINDOW, SWA_KV_HEADS, SWA_HEAD_DIM)
        kp_l.append(last(sk))
        vp_l.append(last(sv))

        q, k, g_hi, g_lo, v, gate, sq, sk, sv = _in_proj(xs, nw_a, w_in_p, wg2_p, bg2, tabs_s, l)
        g = g_hi.astype(F32) + g_lo.astype(F32)

        def bhtk(a, w_):
            return a.reshape(nbatch, nt, GLA_HEADS, w_).transpose(0, 2, 1, 3).reshape(nbatch * GLA_HEADS, nt, w_)

        qkg4 = jnp.pad(jnp.stack([bhtk(a, GLA_DK) for a in (q, k, g)]), ((0, 0),) + tpad)
        v4 = jnp.pad(bhtk(v.astype(F32), GLA_DV), tpad)
        qk = sq.reshape(nbatch, nt, SWA_KV_HEADS, SWA_GROUP, SWA_HEAD_DIM).transpose(0, 2, 1, 3, 4)
        zq = jnp.zeros_like(qk[:, 0])
        qx = jnp.stack([jnp.concatenate([qk[:, 0], zq], axis=-1), jnp.concatenate([zq, qk[:, 1]], axis=-1)],
                       axis=1).reshape(nbatch, SWA_KV_HEADS * nt * SWA_GROUP, SWA_KV)
        sink_rows = jnp.broadcast_to(attn_sinks[l].reshape(SWA_KV_HEADS, 1, SWA_GROUP),
                                     (SWA_KV_HEADS, nt, SWA_GROUP)).reshape(1, -1, 1)
        og4, os4, *carried = _sample_mix(
            qkg4, v4, state_r, qx, sk.reshape(nbatch, nt, SWA_KV), sv.reshape(nbatch, nt, SWA_KV),
            ck_r, cv_r, sink_rows, carried, l, nbatch, nt)
        og = og4[:, :nt].reshape(nbatch, GLA_HEADS, nt, GLA_DV).transpose(0, 2, 1, 3).reshape(n_s, GLA_WIDTH)
        os5 = os4.reshape(nbatch, SWA_KV_HEADS, nt, SWA_GROUP, SWA_KV)
        osw = jnp.stack([os5[:, 0, :, :, :SWA_HEAD_DIM], os5[:, 1, :, :, SWA_HEAD_DIM:]], axis=2)
        osw = osw.reshape(n_s, SWA_Q)
        xs = _out_ffn(xs, og.astype(BF16), gate, gn, osw.astype(BF16), wo_b, nw_f, wg_b, wu_b, wd_b, nf, l, final)

    s_all, ck_all, cv_all = carried
    pos_major = lambda c: c.reshape(DEPTH, nbatch, SWA_KV_HEADS, SWA_HEAD_DIM, WINDOW).transpose(0, 1, 4, 2, 3)
    return (xp.reshape(batch, seq, D_MODEL), xs.reshape(nbatch, nt, D_MODEL),
            jnp.stack(sp_l), jnp.stack(kp_l), jnp.stack(vp_l),
            s_all.reshape(DEPTH, nbatch, GLA_HEADS, GLA_DK, GLA_DV), pos_major(ck_all), pos_major(cv_all))
```

```python
import functools

import jax
import jax.numpy as jnp
import numpy as np
from jax import lax
from jax.experimental import pallas as pl
from jax.experimental.pallas import tpu as pltpu

F32 = jnp.float32
BF16 = jnp.bfloat16

D_MODEL = 1024
DEPTH = 4
PAST_LEN = 8192
GLA_HEADS = 4
GLA_DV = 128
GLA_DK = 64
GLA_QK = GLA_HEADS * GLA_DK
GLA_WIDTH = GLA_HEADS * GLA_DV
GATE_RANK = 16
GATE_NORMALIZER = 16.0
SWA_HEAD_DIM = 64
SWA_HEADS = 8
SWA_KV_HEADS = 2
SWA_GROUP = 4
SWA_Q = SWA_HEADS * SWA_HEAD_DIM
SWA_KV = SWA_KV_HEADS * SWA_HEAD_DIM
WINDOW = 128
ROPE_THETA = 10000.0
D_FF = 2816
NORM_EPS = 1e-6

LANES = 128
SUBLANES = 8
VMEM_LIMIT = 56 * 1024 * 1024

C_Q, C_K, C_V, C_GG, C_SQ, C_SK, C_SV, C_GA = 0, 256, 512, 1024, 1536, 2048, 2176, 2304
IN_COLS_P = C_GA + LANES

TOK_TILE = 512
IN_TILE = 1024
GLA_CHUNK = 64
GLA_SUB = 8
GLA_TILE = 1024
GLA_UNROLL = 16
FF_SPLITS = (0, 768, 1536, 2304, D_FF)
SWA_ROWS = 32
SAMPLE_BT = 16
NEG = -1e30
LOG2E = 1.4426950408889634


def _rms(x, w):
    ms = jnp.mean(x * x, axis=-1, keepdims=True)
    return x * lax.rsqrt(ms + NORM_EPS) * w


def _sigmoid(x):
    return 1.0 / (1.0 + jnp.exp(-x))


def _dot(a, b):
    return jnp.dot(a, b, preferred_element_type=F32)


def _dot_nt(a, b):
    return lax.dot_general(a, b, (((1,), (1,)), ((), ())), preferred_element_type=F32)


def _in_proj_kernel(x_ref, nw_ref, w_ref, wg2_ref, bg2_ref, cos_ref, sa_ref, sb_ref,
                    q_ref, k_ref, ghi_ref, glo_ref, v_ref, gate_ref, sq_ref, sk_ref, sv_ref):
    parts = [slice(r0, r0 + TOK_TILE) for r0 in range(0, x_ref.shape[0], TOK_TILE)]
    hs = [_rms(x_ref[r, :], nw_ref[...]).astype(BF16) for r in parts]
    ga0 = C_GG
    zs = [jnp.concatenate([_dot_nt(h, w_ref[:ga0, :]), _dot_nt(h, w_ref[ga0 + GATE_RANK:, :]),
                           _dot_nt(h, w_ref[ga0:ga0 + LANES, :])], axis=1) for h in hs]

    for r, z in zip(parts, zs):
        q_ref[r, :] = z[:, C_Q:C_K] * (GLA_DK ** -0.5)
        k_ref[r, :] = z[:, C_K:C_V]
        v_ref[r, :] = z[:, C_V:C_GG].astype(BF16)
        gg = z[:, C_GG:C_SQ]
        gate_ref[r, :] = gg * _sigmoid(gg)
        gx = _dot(z[:, C_GA:IN_COLS_P].astype(BF16), wg2_ref[...]) + bg2_ref[...]
        g2 = (jnp.minimum(gx, 0.0) - jnp.log1p(jnp.exp(-jnp.abs(gx)))) * (LOG2E / GATE_NORMALIZER)
        g_hi = g2.astype(BF16)
        ghi_ref[r, :] = g_hi
        glo_ref[r, :] = (g2 - g_hi.astype(F32)).astype(BF16)

        cos, sa, sb = cos_ref[r, :], sa_ref[r, :], sb_ref[r, :]

        def rope(y):
            return y * cos + pltpu.roll(y, LANES - 32, 1) * sa + pltpu.roll(y, 32, 1) * sb

        for j in range(SWA_Q // LANES):
            sq_ref[r, j * LANES:(j + 1) * LANES] = (
                rope(z[:, C_SQ + j * LANES:C_SQ + (j + 1) * LANES]) * (SWA_HEAD_DIM ** -0.5 * LOG2E)).astype(BF16)
        sk_ref[r, :] = rope(z[:, C_SK:C_SV])
        sv_ref[r, :] = z[:, C_SV:C_GA]


def _in_proj(x, nw, w, wg2, bg2, tabs, layer):
    n = x.shape[0]
    tile = min(n, IN_TILE)
    nt = n // tile
    cos, sa, sb = tabs
    ntab = cos.shape[0] // tile
    row = lambda w_: pl.BlockSpec((tile, w_), lambda i: (i, 0))
    tab = pl.BlockSpec((tile, LANES), lambda i: (i % ntab, 0))
    const = lambda shp: pl.BlockSpec((None,) + shp, lambda i: (layer, 0, 0), pipeline_mode=pl.Buffered(1))
    outs = [(GLA_QK, F32), (GLA_QK, F32), (GLA_QK, BF16), (GLA_QK, BF16), (GLA_WIDTH, BF16), (GLA_WIDTH, F32),
            (SWA_Q, BF16), (SWA_KV, F32), (SWA_KV, F32)]
    return pl.pallas_call(
        _in_proj_kernel,
        grid=(nt,),
        in_specs=[row(D_MODEL), const((1, D_MODEL)), const((w.shape[1], D_MODEL)),
                  const((LANES, GLA_QK)), const((1, GLA_QK)), tab, tab, tab],
        out_specs=[row(w_) for w_, _ in outs],
        out_shape=[jax.ShapeDtypeStruct((n, w_), dt) for w_, dt in outs],
        compiler_params=pltpu.CompilerParams(dimension_semantics=("arbitrary",),
                                             vmem_limit_bytes=VMEM_LIMIT),
        name="in_proj",
    )(x, nw, w, wg2, bg2, cos, sa, sb)


def _gla_kernel(q_ref, k_ref, ghi_ref, glo_ref, v_ref, bias_ref, o_ref, sout_ref, s_ref):
    ti = pl.program_id(1)
    C, SB = GLA_CHUNK, GLA_SUB
    NB = C // SB
    NP = GLA_HEADS // 2

    @pl.when(ti == 0)
    def _():
        s_ref[...] = jnp.zeros_like(s_ref)

    ltri = (lax.broadcasted_iota(jnp.int32, (C, 2 * C), 0) >= lax.broadcasted_iota(jnp.int32, (C, 2 * C), 1) % C
            ).astype(F32).astype(BF16)
    lane = lax.broadcasted_iota(jnp.int32, (1, LANES), 1)
    mask_a = (lane < GLA_DK).astype(F32)
    mask_b = 1.0 - mask_a
    col8 = lax.broadcasted_iota(jnp.int32, (SB, LANES), 1) % C
    colblk, colmod = col8 // SB, col8 % SB
    zk = jnp.zeros((GLA_DK, GLA_DV), BF16)
    off1 = [sum(C - SB * (jj + 1) for jj in range(j)) for j in range(NB - 1)]

    def sub_bcast(x, j):
        x3 = x.reshape(NB, SB, LANES)
        return jnp.broadcast_to(x3[:, j:j + 1, :], (NB, SB, LANES)).reshape(C, LANES)

    def heads(x):
        return jnp.concatenate([x * mask_a, x * mask_b], axis=0).astype(BF16)

    def body(it, carry):
        rows = [pl.ds(pl.multiple_of((it * GLA_UNROLL + cc) * C, C), C) for cc in range(GLA_UNROLL)]
        units = [(cc, p) for cc in range(GLA_UNROLL) for p in range(NP)]
        kl = [slice(p * LANES, (p + 1) * LANES) for p in range(NP)]
        vl = [slice(h * GLA_DV, (h + 1) * GLA_DV) for h in range(GLA_HEADS)]

        b_all = [_dot(ltri, jnp.concatenate([ghi_ref[rows[cc], :], glo_ref[rows[cc], :]], axis=0))
                 for cc in range(GLA_UNROLL)]

        qe, outs, dec, upd = {}, {}, {}, {}
        for cc, p in units:
            q, k, b = q_ref[rows[cc], kl[p]], k_ref[rows[cc], kl[p]], b_all[cc][:, kl[p]]
            b_last = b[C - 1:C, :]
            k1 = k * jnp.exp2(sub_bcast(b, SB - 1) - b)
            lhs1 = jnp.concatenate(
                [q[SB * (j + 1):, :] * jnp.exp2(b[SB * (j + 1):, :] - b[SB * j + SB - 1:SB * j + SB, :])
                 for j in range(NB - 1)], axis=0)
            lhs2 = jnp.concatenate(
                [q * jnp.exp2((b - sub_bcast(b, jj)) + jnp.tile(bias_ref[jj], (NB, 1))) for jj in range(SB)],
                axis=0)
            qe[cc, p] = (q * jnp.exp2(b)).astype(BF16)
            outs[cc, p] = (_dot_nt(lhs1.astype(BF16), heads(k1)), _dot_nt(lhs2.astype(BF16), heads(k)))
            kd = k * jnp.exp2(b_last - b)
            mt = jnp.concatenate([kd, jnp.broadcast_to(b_last, (C, LANES))], axis=0).T
            kdt = mt[:, :C].astype(BF16)
            dec[cc, p] = jnp.exp2(mt[:, C:C + 1])
            upd[cc, p] = (_dot(kdt[:GLA_DK], v_ref[rows[cc], vl[2 * p]]),
                          _dot(kdt[GLA_DK:], v_ref[rows[cc], vl[2 * p + 1]]))

        lo = {}
        for u in units:
            out1, out2 = outs[u]
            a_rows = []
            for i in range(NB):
                d = out2[(SB - 1) * C + SB * i:(SB - 1) * C + SB * (i + 1), :]
                for jj in range(SB - 2, -1, -1):
                    d = jnp.where(colmod == jj, out2[jj * C + SB * i:jj * C + SB * (i + 1), :], d)
                acc = jnp.where(colblk == i, d, 0.0)
                for j in range(i):
                    r0 = off1[j] + SB * (i - j - 1)
                    acc = jnp.where(colblk == j, out1[r0:r0 + SB, :], acc)
                a_rows.append(acc)
            a = jnp.concatenate(a_rows, axis=0)
            lo[u] = jnp.concatenate([qe[u], a.astype(BF16)], axis=1)

        state = [s_ref[h] for h in range(GLA_HEADS)]
        o_all = {}
        for cc in range(GLA_UNROLL):
            for p in range(NP):
                ha, hb = 2 * p, 2 * p + 1
                v_a, v_b = v_ref[rows[cc], vl[ha]], v_ref[rows[cc], vl[hb]]
                o_all[cc, ha] = _dot(lo[cc, p], jnp.concatenate([state[ha].astype(BF16), zk, v_a, zk], axis=0))
                o_all[cc, hb] = _dot(lo[cc, p], jnp.concatenate([zk, state[hb].astype(BF16), zk, v_b], axis=0))
                state[ha] = dec[cc, p][:GLA_DK] * state[ha] + upd[cc, p][0]
                state[hb] = dec[cc, p][GLA_DK:] * state[hb] + upd[cc, p][1]
        for h in range(GLA_HEADS):
            s_ref[h] = state[h]

        for cc in range(GLA_UNROLL):
            for h in range(GLA_HEADS):
                o_ref[rows[cc], vl[h]] = o_all[cc, h].astype(BF16)
        return carry

    lax.fori_loop(0, GLA_TILE // (C * GLA_UNROLL), body, 0)

    @pl.when(ti == pl.num_programs(1) - 1)
    def _():
        sout_ref[0] = s_ref[...]


def _gla_prompt(q, k, g_hi, g_lo, v, batch, seq):
    nt = seq // GLA_TILE
    blk = lambda w_: pl.BlockSpec((GLA_TILE, w_), lambda b, t: (b * nt + t, 0))
    r = np.arange(GLA_SUB)
    bias = np.where(r[None, :, None] >= r[:, None, None], 0.0, NEG) * np.ones((1, 1, LANES))
    return pl.pallas_call(
        _gla_kernel,
        grid=(batch, nt),
        in_specs=[blk(GLA_QK), blk(GLA_QK), blk(GLA_QK), blk(GLA_QK), blk(GLA_WIDTH),
                  pl.BlockSpec((GLA_SUB, GLA_SUB, LANES), lambda b, t: (0, 0, 0))],
        out_specs=[blk(GLA_WIDTH),
                   pl.BlockSpec((1, GLA_HEADS, GLA_DK, GLA_DV), lambda b, t: (b, 0, 0, 0))],
        out_shape=[jax.ShapeDtypeStruct((batch * seq, GLA_WIDTH), BF16),
                   jax.ShapeDtypeStruct((batch, GLA_HEADS, GLA_DK, GLA_DV), F32)],
        scratch_shapes=[pltpu.VMEM((GLA_HEADS, GLA_DK, GLA_DV), F32)],
        compiler_params=pltpu.CompilerParams(
            dimension_semantics=("arbitrary", "arbitrary"), vmem_limit_bytes=VMEM_LIMIT),
        name="gla_prompt",
    )(q, k, g_hi, g_lo, v, jnp.asarray(bias, F32))


def _out_ffn_kernel(x_ref, og_ref, gate_ref, gn_ref, os_ref, wo_ref, nw_ref, wg_ref, wu_ref, wd_ref, nf_ref,
                    y_ref, *, final):
    gn = gn_ref[...]
    og = jnp.concatenate([_rms(og_ref[:, h * GLA_DV:(h + 1) * GLA_DV].astype(F32), gn)
                          for h in range(GLA_HEADS)], axis=1) * gate_ref[...]
    x = (x_ref[...] + _dot(og.astype(BF16), wo_ref[:GLA_WIDTH, :]) + _dot(os_ref[...], wo_ref[GLA_WIDTH:, :]))
    h = _rms(x, nw_ref[...]).astype(BF16)
    a = _dot(h, wg_ref[...])
    u = _dot(h, wu_ref[...])
    acc = x + _dot((a * _sigmoid(a) * u).astype(BF16), wd_ref[...])
    if final:
        acc = _rms(acc, nf_ref[...])
    y_ref[...] = acc


def _out_ffn(x, og, gate, gn, osw, wo, nw, wg, wu, wd, nf, layer, final):
    n = x.shape[0]
    row = lambda w_: pl.BlockSpec((TOK_TILE, w_), lambda i: (i, 0))
    const = lambda shp: pl.BlockSpec((None,) + shp, lambda i: (layer, 0, 0), pipeline_mode=pl.Buffered(1))
    return pl.pallas_call(
        functools.partial(_out_ffn_kernel, final=final),
        grid=(n // TOK_TILE,),
        in_specs=[row(D_MODEL), row(GLA_WIDTH), row(GLA_WIDTH), const((1, GLA_DV)), row(SWA_Q),
                  const((D_MODEL, D_MODEL)), const((1, D_MODEL)),
                  const((D_MODEL, D_FF)), const((D_MODEL, D_FF)), const((D_FF, D_MODEL)),
                  pl.BlockSpec((1, D_MODEL), lambda i: (0, 0))],
        out_specs=row(D_MODEL),
        out_shape=jax.ShapeDtypeStruct((n, D_MODEL), F32),
        compiler_params=pltpu.CompilerParams(dimension_semantics=("arbitrary",),
                                             vmem_limit_bytes=VMEM_LIMIT),
        name="out_ffn",
    )(x, og, gate, gn, osw, wo, nw, wg, wu, wd, nf)


def _swa_setup(kp_ref, kc_ref, vp_ref, vc_ref, ke_scr, ve_scr):
    kk = jnp.concatenate([kp_ref[...], kc_ref[...]], axis=0)
    vv = jnp.concatenate([vp_ref[...], vc_ref[...]], axis=0)
    lo = lax.broadcasted_iota(jnp.int32, kk.shape, 1) < SWA_HEAD_DIM
    for x, scr in ((kk, ke_scr), (vv, ve_scr)):
        x_sw = pltpu.roll(x, SWA_HEAD_DIM, 1)
        scr[0, 0] = jnp.where(lo, x, 0.0).astype(BF16)
        scr[0, 1] = jnp.where(lo, 0.0, x_sw).astype(BF16)
        scr[1, 0] = jnp.where(lo, x_sw, 0.0).astype(BF16)
        scr[1, 1] = jnp.where(lo, 0.0, x).astype(BF16)


def _swa_scores(j, q_ref, ke_scr, s_scr):
    W = WINDOW
    for p in range(SWA_HEADS // 2):
        kv = p // (SWA_GROUP // 2)
        q = q_ref[j * W:(j + 1) * W, p * LANES:(p + 1) * LANES]
        for hh in range(2):
            s_scr[p, :, hh * 2 * W:(hh + 1) * 2 * W] = _dot_nt(q, ke_scr[kv, hh, j * W:(j + 2) * W, :])


def _swa_softmax(sink_ref, bias_blk, s_scr, p_scr, r_scr):
    W = WINDOW
    lane_q = lax.broadcasted_iota(jnp.int32, (SWA_ROWS, LANES), 1) < SWA_HEAD_DIM
    for p in range(SWA_HEADS // 2):
        for r0 in range(0, W, SWA_ROWS):
            rs = slice(r0, r0 + SWA_ROWS)
            rden = []
            for hh in range(2):
                cs = slice(hh * 2 * W, (hh + 1) * 2 * W)
                sink = sink_ref[2 * p + hh] * LOG2E
                sh = s_scr[p, rs, cs] + bias_blk[rs, :]
                m = jnp.maximum(jnp.max(sh, axis=-1, keepdims=True), sink)
                e = jnp.exp2(sh - m)
                den = jnp.sum(e, axis=-1, keepdims=True) + jnp.exp2(sink - m)
                p_scr[p, rs, cs] = e.astype(BF16)
                rden.append(1.0 / den)
            r_scr[p, rs, :] = jnp.where(lane_q, rden[0], rden[1])


def _swa_values(j, p_scr, ve_scr, r_scr, o_ref):
    W = WINDOW
    for p in range(SWA_HEADS // 2):
        kv = p // (SWA_GROUP // 2)
        o = (_dot(p_scr[p, :, :2 * W], ve_scr[kv, 0, j * W:(j + 2) * W, :])
             + _dot(p_scr[p, :, 2 * W:], ve_scr[kv, 1, j * W:(j + 2) * W, :])) * r_scr[p]
        o_ref[j * W:(j + 1) * W, p * LANES:(p + 1) * LANES] = o.astype(BF16)


def _ffn_swa_kernel(sink_ref, x_ref, og_ref, gate_ref, gn_ref, wo_ref, nw_ref, wg_ref, wu_ref, wd_ref, nf_ref,
                    osw0_ref, q_ref, kp_ref, kc_ref, vp_ref, vc_ref, bias_ref, y_ref,
                    osw_scr, ke_scr, ve_scr, s_scr, p_scr, r_scr, *, final, tiles_per_seq, n_tiles):
    s = pl.program_id(0)
    nblk = TOK_TILE // WINDOW

    @pl.when(s == 0)
    def _():
        osw_scr[0] = osw0_ref[...]

    _swa_setup(kp_ref, kc_ref, vp_ref, vc_ref, ke_scr, ve_scr)
    t_seq = jnp.minimum(s + 1, n_tiles - 1) % tiles_per_seq
    osw_cur = osw_scr.at[(s + 1) % 2]

    gn = gn_ref[...]
    og = jnp.concatenate([_rms(og_ref[:, h * GLA_DV:(h + 1) * GLA_DV].astype(F32), gn)
                          for h in range(GLA_HEADS)], axis=1) * gate_ref[...]
    x = (x_ref[...] + _dot(og.astype(BF16), wo_ref[:GLA_WIDTH, :])
         + _dot(osw_scr[s % 2], wo_ref[GLA_WIDTH:, :]))
    h = _rms(x, nw_ref[...]).astype(BF16)

    t_parts = []
    for j in range(nblk):
        c0, c1 = FF_SPLITS[j], FF_SPLITS[j + 1]
        _swa_scores(j, q_ref, ke_scr, s_scr)
        a = _dot(h, wg_ref[:, c0:c1])
        u = _dot(h, wu_ref[:, c0:c1])
        t_parts.append((a * _sigmoid(a) * u).astype(BF16))
        _swa_softmax(sink_ref, bias_ref.at[jnp.minimum(t_seq * nblk + j, 1)], s_scr, p_scr, r_scr)
        _swa_values(j, p_scr, ve_scr, r_scr, osw_cur)
    acc = x + _dot(jnp.concatenate(t_parts, axis=1), wd_ref[...])
    if final:
        acc = _rms(acc, nf_ref[...])
    y_ref[...] = acc


def _swa_first_kernel(sink_ref, q_ref, kp_ref, kc_ref, vp_ref, vc_ref, bias_ref, o_ref,
                      ke_scr, ve_scr, s_scr, p_scr, r_scr):
    _swa_setup(kp_ref, kc_ref, vp_ref, vc_ref, ke_scr, ve_scr)
    for j in range(TOK_TILE // WINDOW):
        _swa_scores(j, q_ref, ke_scr, s_scr)
        _swa_softmax(sink_ref, bias_ref.at[min(j, 1)], s_scr, p_scr, r_scr)
        _swa_values(j, p_scr, ve_scr, r_scr, o_ref)


def _ffn_swa(x, og, gate, gn, sinks, sq, sk, sv, wo, nw, wg, wu, wd, nf, layer, final, seq):
    n = x.shape[0]
    W = WINDOW
    n_tiles = n // TOK_TILE
    tiles_per_seq = seq // TOK_TILE
    nblk = TOK_TILE // W
    ffn_i = lambda s: (s, 0)
    swa_i = lambda s: (jnp.minimum(s + 1, n_tiles - 1), 0)

    def prev_i(s):
        t = jnp.minimum(s + 1, n_tiles - 1)
        return (t - t % tiles_per_seq) * nblk + jnp.maximum((t % tiles_per_seq) * nblk - 1, 0), 0

    row = lambda w_, im: pl.BlockSpec((TOK_TILE, w_), im)
    const = lambda shp: pl.BlockSpec((None,) + shp, lambda s: (layer, 0, 0), pipeline_mode=pl.Buffered(1))
    first = lambda s: (0, 0)
    qi, kj = np.arange(W)[:, None], np.arange(2 * W)[None, :]
    band = (kj > qi) & (kj <= qi + W)
    bias = jnp.asarray(np.stack([np.where(band & (kj >= W), 0.0, NEG), np.where(band, 0.0, NEG)]), F32)
    bias_spec = pl.BlockSpec((2, W, 2 * W), lambda s: (0, 0, 0))
    ext_scr = pltpu.VMEM((SWA_KV_HEADS, 2, W + TOK_TILE, LANES), BF16)
    npair = SWA_HEADS // 2
    swa_scr = [ext_scr, ext_scr, pltpu.VMEM((npair, W, 4 * W), F32), pltpu.VMEM((npair, W, 4 * W), BF16),
               pltpu.VMEM((npair, W, LANES), F32)]
    osw0 = pl.pallas_call(
        _swa_first_kernel,
        grid=(1,),
        in_specs=[pl.BlockSpec(memory_space=pltpu.SMEM), row(SWA_Q, first),
                  pl.BlockSpec((W, SWA_KV), first), row(SWA_KV, first),
                  pl.BlockSpec((W, SWA_KV), first), row(SWA_KV, first), bias_spec],
        out_specs=row(SWA_Q, first),
        out_shape=jax.ShapeDtypeStruct((TOK_TILE, SWA_Q), BF16),
        scratch_shapes=swa_scr,
        compiler_params=pltpu.CompilerParams(dimension_semantics=("arbitrary",), vmem_limit_bytes=VMEM_LIMIT),
        name="swa_first",
    )(sinks, sq, sk, sk, sv, sv, bias)
    return pl.pallas_call(
        functools.partial(_ffn_swa_kernel, final=final, tiles_per_seq=tiles_per_seq, n_tiles=n_tiles),
        grid=(n_tiles,),
        in_specs=[pl.BlockSpec(memory_space=pltpu.SMEM),
                  row(D_MODEL, ffn_i), row(GLA_WIDTH, ffn_i), row(GLA_WIDTH, ffn_i), const((1, GLA_DV)),
                  const((D_MODEL, D_MODEL)), const((1, D_MODEL)),
                  const((D_MODEL, D_FF)), const((D_MODEL, D_FF)), const((D_FF, D_MODEL)),
                  pl.BlockSpec((1, D_MODEL), lambda s: (0, 0)),
                  row(SWA_Q, first), row(SWA_Q, swa_i),
                  pl.BlockSpec((W, SWA_KV), prev_i), row(SWA_KV, swa_i),
                  pl.BlockSpec((W, SWA_KV), prev_i), row(SWA_KV, swa_i), bias_spec],
        out_specs=row(D_MODEL, ffn_i),
        out_shape=jax.ShapeDtypeStruct((n, D_MODEL), F32),
        scratch_shapes=[pltpu.VMEM((2, TOK_TILE, SWA_Q), BF16)] + swa_scr,
        compiler_params=pltpu.CompilerParams(dimension_semantics=("arbitrary",),
                                             vmem_limit_bytes=VMEM_LIMIT),
        name="ffn_swa",
    )(sinks, x, og, gate, gn, wo, nw, wg, wu, wd, nf, osw0, sq, sk, sk, sv, sv, bias)


def _sample_kernel(qkg_ref, v_ref, s0_ref,
                   qx_ref, kn_ref, vn_ref, ck_ref, cv_ref, sink_ref,
                   s_prev_ref, ck_prev_ref, cv_prev_ref,
                   og_ref, os_ref, sn_ref, cko_ref, cvo_ref, *, nt):
    del s_prev_ref, ck_prev_ref, cv_prev_ref
    TP = SUBLANES
    q, k, g, v = qkg_ref[0], qkg_ref[1], qkg_ref[2], v_ref[...]
    s0 = s0_ref[...]
    tt = lax.broadcasted_iota(jnp.int32, (1, TP, 1), 1)
    b = jnp.zeros_like(g)
    for s in range(nt):
        b = b + jnp.where(tt >= s, g[:, s:s + 1, :], 0.0)
    o = jnp.einsum('gtk,gkv->gtv', (q * jnp.exp2(b)).astype(BF16), s0.astype(BF16),
                   preferred_element_type=F32)
    for s in range(nt):
        dec = jnp.exp2(jnp.where(tt >= s, b - b[:, s:s + 1, :], NEG))
        a = jnp.sum(q * k[:, s:s + 1, :] * dec, axis=-1, keepdims=True)
        o = o + a * v[:, s:s + 1, :]
    og_ref[...] = o

    b_last = b[:, nt - 1:nt, :]
    d = jnp.exp2(b_last)
    d_hi = d.astype(BF16).astype(F32)
    x = jnp.where(tt < nt, k * jnp.exp2(b_last - b), jnp.where(tt == nt, d_hi, jnp.where(tt == nt + 1, d - d_hi, 0.0)))
    ones = jnp.where((tt == nt) | (tt == nt + 1), 1.0, 0.0)
    y = jnp.concatenate([jnp.where(tt < nt, v, 0.0), jnp.broadcast_to(ones, v.shape)], axis=2)
    r = lax.dot_general(x.astype(BF16), y.astype(BF16), (((1,), (1,)), ((0,), (0,))),
                        preferred_element_type=F32)
    sn_ref[...] = r[:, :, GLA_DV:] * s0 + r[:, :, :GLA_DV]

    W = WINDOW
    qx = qx_ref[...]
    ck, cv = ck_ref[...], cv_ref[...]
    kn, vn = kn_ref[...], vn_ref[...]
    sink = sink_ref[...] * LOG2E
    nrow = SWA_KV_HEADS * nt * SWA_GROUP
    r = lax.broadcasted_iota(jnp.int32, (1, nrow, W), 1)
    tq = (r % (nt * SWA_GROUP)) // SWA_GROUP
    j = lax.broadcasted_iota(jnp.int32, (1, nrow, W), 2)
    sc = jnp.einsum('bqd,bds->bqs', qx, ck.astype(BF16), preferred_element_type=F32)
    sc = jnp.where(j > tq, sc, NEG)
    tq1 = tq[:, :, 0:1]
    qf = qx.astype(F32)
    sn = [jnp.where(tq1 >= s, jnp.sum(qf * kn[:, s:s + 1, :], axis=-1, keepdims=True), NEG)
          for s in range(nt)]
    m = jnp.maximum(jnp.max(sc, axis=-1, keepdims=True), sink)
    for s in range(nt):
        m = jnp.maximum(m, sn[s])
    pc = jnp.exp2(sc - m)
    pn = [jnp.exp2(sn[s] - m) for s in range(nt)]
    den = jnp.sum(pc, axis=-1, keepdims=True) + jnp.exp2(sink - m)
    for s in range(nt):
        den = den + pn[s]
    o = jnp.einsum('bqs,bds->bqd', pc.astype(BF16), cv.astype(BF16), preferred_element_type=F32)
    for s in range(nt):
        o = o + pn[s] * vn[:, s:s + 1, :]
    os_ref[...] = o * (1.0 / den)

    wl = lax.broadcasted_iota(jnp.int32, (1, 1, W), 2)
    pr = lax.broadcasted_iota(jnp.int32, (1, 2 * nt, W), 1)
    pw = lax.broadcasted_iota(jnp.int32, (1, 2 * nt, W), 2)
    place = jnp.broadcast_to(jnp.where(pw == W - nt + pr % nt, 1.0, 0.0), (ck.shape[0], 2 * nt, W)).astype(BF16)
    for src, new, dst in ((ck, kn, cko_ref), (cv, vn, cvo_ref)):
        hi = new.astype(BF16).astype(F32)
        x = jnp.concatenate([hi, new - hi], axis=1).astype(BF16)
        placed = lax.dot_general(x, place, (((1,), (1,)), ((0,), (0,))), preferred_element_type=F32)
        dst[...] = jnp.where(wl >= W - nt, placed, pltpu.roll(src, W - nt, 2))


def _sample_mix(qkg4, v4, state, qx, kn, vn, ck, cv, sink_rows, carried, layer, nbatch, nt):
    bt = SAMPLE_BT
    G = bt * GLA_HEADS
    TP = SUBLANES
    nrow = SWA_KV_HEADS * nt * SWA_GROUP
    b3 = lambda a, c: pl.BlockSpec((G, a, c), lambda i: (i, 0, 0))
    s3 = lambda a, c: pl.BlockSpec((bt, a, c), lambda i: (i, 0, 0))
    lay_g = pl.BlockSpec((None, G, GLA_DK, GLA_DV), lambda i: (layer, i, 0, 0))
    lay_c = pl.BlockSpec((None, bt, SWA_KV, WINDOW), lambda i: (layer, i, 0, 0))
    ng = nbatch * GLA_HEADS
    n_in = 9
    return pl.pallas_call(
        functools.partial(_sample_kernel, nt=nt),
        grid=(nbatch // bt,),
        in_specs=[pl.BlockSpec((3, G, TP, GLA_DK), lambda i: (0, i, 0, 0)), b3(TP, GLA_DV), lay_g,
                  s3(nrow, SWA_KV), s3(nt, SWA_KV), s3(nt, SWA_KV), lay_c, lay_c,
                  pl.BlockSpec((1, nrow, 1), lambda i: (0, 0, 0))]
                 + [pl.BlockSpec(memory_space=pl.ANY)] * 3,
        out_specs=[b3(TP, GLA_DV), s3(nrow, SWA_KV), lay_g, lay_c, lay_c],
        out_shape=[jax.ShapeDtypeStruct((ng, TP, GLA_DV), F32),
                   jax.ShapeDtypeStruct((nbatch, nrow, SWA_KV), F32),
                   jax.ShapeDtypeStruct((DEPTH, ng, GLA_DK, GLA_DV), F32),
                   jax.ShapeDtypeStruct((DEPTH, nbatch, SWA_KV, WINDOW), F32),
                   jax.ShapeDtypeStruct((DEPTH, nbatch, SWA_KV, WINDOW), F32)],
        input_output_aliases={n_in + j: 2 + j for j in range(3)},
        compiler_params=pltpu.CompilerParams(dimension_semantics=("arbitrary",),
                                             vmem_limit_bytes=VMEM_LIMIT),
        name="sample_mix",
    )(qkg4, v4, state, qx, kn, vn, ck, cv, sink_rows, *carried)


def _rope_tables(pos):
    half = SWA_HEAD_DIM // 2
    inv = ROPE_THETA ** (-jnp.arange(half, dtype=F32) / half)
    ang = pos.astype(F32)[:, None] * inv[None, :]
    cos, sin = jnp.cos(ang), jnp.sin(ang)
    zero = jnp.zeros_like(sin)
    rep = LANES // SWA_HEAD_DIM
    return (jnp.tile(jnp.concatenate([cos, cos], axis=1), (1, rep)),
            jnp.tile(jnp.concatenate([-sin, zero], axis=1), (1, rep)),
            jnp.tile(jnp.concatenate([zero, sin], axis=1), (1, rep)))


def kernel(x_prompt, x_sample, state_gla, cache_swa_k, cache_swa_v, norm_attn, w_in, w_gk2, b_gk2,
           gla_norm, attn_sinks, w_o, norm_ffn, w_gate, w_up, w_down, norm_final):
    batch, seq, _ = x_prompt.shape
    nbatch, nt, _ = x_sample.shape
    n_s = nbatch * nt

    w_in_p = jnp.swapaxes(w_in, 1, 2).astype(BF16)
    wg2_p = jnp.concatenate([w_gk2, jnp.zeros((DEPTH, LANES - GATE_RANK, GLA_QK), w_gk2.dtype)],
                            axis=1).astype(BF16)
    bg2 = b_gk2.reshape(DEPTH, 1, GLA_QK)
    nw_a = norm_attn.reshape(DEPTH, 1, D_MODEL)
    nw_f = norm_ffn.reshape(DEPTH, 1, D_MODEL)
    wo_b, wg_b, wu_b, wd_b = (w.astype(BF16) for w in (w_o, w_gate, w_up, w_down))
    nf = norm_final.reshape(1, D_MODEL)
    gn = gla_norm.reshape(DEPTH, 1, GLA_DV)

    tabs_p = _rope_tables(jnp.arange(seq))
    tabs_s = _rope_tables(jnp.tile(PAST_LEN + jnp.arange(nt), TOK_TILE // nt))

    state_r = state_gla.reshape(DEPTH, nbatch * GLA_HEADS, GLA_DK, GLA_DV)
    pos_minor = lambda c: c.transpose(0, 1, 3, 4, 2).reshape(DEPTH, nbatch, SWA_KV, WINDOW)
    ck_r, cv_r = pos_minor(cache_swa_k), pos_minor(cache_swa_v)

    xp = x_prompt.reshape(batch * seq, D_MODEL)
    xs = x_sample.reshape(n_s, D_MODEL)
    sp_l, kp_l, vp_l = [], [], []
    carried = [jnp.zeros(state_r.shape, F32), jnp.zeros(ck_r.shape, F32), jnp.zeros(cv_r.shape, F32)]
    tpad = ((0, 0), (0, SUBLANES - nt), (0, 0))
    for l in range(DEPTH):
        final = l == DEPTH - 1
        q, k, g_hi, g_lo, v, gate, sq, sk, sv = _in_proj(xp, nw_a, w_in_p, wg2_p, bg2, tabs_p, l)
        og, s_fin = _gla_prompt(q, k, g_hi, g_lo, v, batch, seq)
        xp = _ffn_swa(xp, og, gate, gn, attn_sinks[l], sq, sk, sv, wo_b, nw_f, wg_b, wu_b, wd_b, nf, l, final, seq)
        sp_l.append(s_fin)
        last = lambda a: a.reshape(batch, seq, SWA_KV)[:, seq - WINDOW:].reshape(
            batch, WINDOW, SWA_KV_HEADS, SWA_HEAD_DIM)
        kp_l.append(last(sk))
        vp_l.append(last(sv))

        q, k, g_hi, g_lo, v, gate, sq, sk, sv = _in_proj(xs, nw_a, w_in_p, wg2_p, bg2, tabs_s, l)
        g = g_hi.astype(F32) + g_lo.astype(F32)

        def bhtk(a, w_):
            return a.reshape(nbatch, nt, GLA_HEADS, w_).transpose(0, 2, 1, 3).reshape(nbatch * GLA_HEADS, nt, w_)

        qkg4 = jnp.pad(jnp.stack([bhtk(a, GLA_DK) for a in (q, k, g)]), ((0, 0),) + tpad)
        v4 = jnp.pad(bhtk(v.astype(F32), GLA_DV), tpad)
        qk = sq.reshape(nbatch, nt, SWA_KV_HEADS, SWA_GROUP, SWA_HEAD_DIM).transpose(0, 2, 1, 3, 4)
        zq = jnp.zeros_like(qk[:, 0])
        qx = jnp.stack([jnp.concatenate([qk[:, 0], zq], axis=-1), jnp.concatenate([zq, qk[:, 1]], axis=-1)],
                       axis=1).reshape(nbatch, SWA_KV_HEADS * nt * SWA_GROUP, SWA_KV)
        sink_rows = jnp.broadcast_to(attn_sinks[l].reshape(SWA_KV_HEADS, 1, SWA_GROUP),
                                     (SWA_KV_HEADS, nt, SWA_GROUP)).reshape(1, -1, 1)
        og4, os4, *carried = _sample_mix(
            qkg4, v4, state_r, qx, sk.reshape(nbatch, nt, SWA_KV), sv.reshape(nbatch, nt, SWA_KV),
            ck_r, cv_r, sink_rows, carried, l, nbatch, nt)
        og = og4[:, :nt].reshape(nbatch, GLA_HEADS, nt, GLA_DV).transpose(0, 2, 1, 3).reshape(n_s, GLA_WIDTH)
        os5 = os4.reshape(nbatch, SWA_KV_HEADS, nt, SWA_GROUP, SWA_KV)
        osw = jnp.stack([os5[:, 0, :, :, :SWA_HEAD_DIM], os5[:, 1, :, :, SWA_HEAD_DIM:]], axis=2)
        osw = osw.reshape(n_s, SWA_Q)
        xs = _out_ffn(xs, og.astype(BF16), gate, gn, osw.astype(BF16), wo_b, nw_f, wg_b, wu_b, wd_b, nf, l, final)

    s_all, ck_all, cv_all = carried
    pos_major = lambda c: c.reshape(DEPTH, nbatch, SWA_KV_HEADS, SWA_HEAD_DIM, WINDOW).transpose(0, 1, 4, 2, 3)
    return (xp.reshape(batch, seq, D_MODEL), xs.reshape(nbatch, nt, D_MODEL),
            jnp.stack(sp_l), jnp.stack(kp_l), jnp.stack(vp_l),
            s_all.reshape(DEPTH, nbatch, GLA_HEADS, GLA_DK, GLA_DV), pos_major(ck_all), pos_major(cv_all))
```

```python
import functools

import jax
import jax.numpy as jnp
import numpy as np
from jax import lax
from jax.experimental import pallas as pl
from jax.experimental.pallas import tpu as pltpu

F32 = jnp.float32
BF16 = jnp.bfloat16

D_MODEL = 1024
DEPTH = 4
PAST_LEN = 8192
GLA_HEADS = 4
GLA_DV = 128
GLA_DK = 64
GLA_QK = GLA_HEADS * GLA_DK
GLA_WIDTH = GLA_HEADS * GLA_DV
GATE_RANK = 16
GATE_NORMALIZER = 16.0
SWA_HEAD_DIM = 64
SWA_HEADS = 8
SWA_KV_HEADS = 2
SWA_GROUP = 4
SWA_Q = SWA_HEADS * SWA_HEAD_DIM
SWA_KV = SWA_KV_HEADS * SWA_HEAD_DIM
WINDOW = 128
ROPE_THETA = 10000.0
D_FF = 2816
NORM_EPS = 1e-6

LANES = 128
SUBLANES = 8
VMEM_LIMIT = 56 * 1024 * 1024

C_Q, C_K, C_V, C_GG, C_SQ, C_SK, C_SV, C_GA = 0, 256, 512, 1024, 1536, 2048, 2176, 2304
IN_COLS_P = C_GA + LANES

TOK_TILE = 512
IN_TILE = 1024
GLA_CHUNK = 64
GLA_SUB = 8
GLA_TILE = 2048
GLA_UNROLL = 32
FF_SPLITS = (0, 768, 1536, 2304, D_FF)
SWA_ROWS = 32
SAMPLE_BT = 16
NEG = -1e30
LOG2E = 1.4426950408889634


def _rms(x, w):
    ms = jnp.mean(x * x, axis=-1, keepdims=True)
    return x * lax.rsqrt(ms + NORM_EPS) * w


def _sigmoid(x):
    return 1.0 / (1.0 + jnp.exp(-x))


def _dot(a, b):
    return jnp.dot(a, b, preferred_element_type=F32)


def _dot_nt(a, b):
    return lax.dot_general(a, b, (((1,), (1,)), ((), ())), preferred_element_type=F32)


def _in_proj_kernel(x_ref, nw_ref, w_ref, wg2_ref, bg2_ref, cos_ref, sa_ref, sb_ref,
                    q_ref, k_ref, ghi_ref, glo_ref, v_ref, gate_ref, sq_ref, sk_ref, sv_ref):
    parts = [slice(r0, r0 + TOK_TILE) for r0 in range(0, x_ref.shape[0], TOK_TILE)]
    hs = [_rms(x_ref[r, :], nw_ref[...]).astype(BF16) for r in parts]
    ga0 = C_GG
    zs = [jnp.concatenate([_dot_nt(h, w_ref[:ga0, :]), _dot_nt(h, w_ref[ga0 + GATE_RANK:, :]),
                           _dot_nt(h, w_ref[ga0:ga0 + LANES, :])], axis=1) for h in hs]

    for r, z in zip(parts, zs):
        q_ref[r, :] = z[:, C_Q:C_K] * (GLA_DK ** -0.5)
        k_ref[r, :] = z[:, C_K:C_V]
        v_ref[r, :] = z[:, C_V:C_GG].astype(BF16)
        gg = z[:, C_GG:C_SQ]
        gate_ref[r, :] = gg * _sigmoid(gg)
        gx = _dot(z[:, C_GA:IN_COLS_P].astype(BF16), wg2_ref[...]) + bg2_ref[...]
        g2 = (jnp.minimum(gx, 0.0) - jnp.log1p(jnp.exp(-jnp.abs(gx)))) * (LOG2E / GATE_NORMALIZER)
        g_hi = g2.astype(BF16)
        ghi_ref[r, :] = g_hi
        glo_ref[r, :] = (g2 - g_hi.astype(F32)).astype(BF16)

        cos, sa, sb = cos_ref[r, :], sa_ref[r, :], sb_ref[r, :]

        def rope(y):
            return y * cos + pltpu.roll(y, LANES - 32, 1) * sa + pltpu.roll(y, 32, 1) * sb

        for j in range(SWA_Q // LANES):
            sq_ref[r, j * LANES:(j + 1) * LANES] = (
                rope(z[:, C_SQ + j * LANES:C_SQ + (j + 1) * LANES]) * (SWA_HEAD_DIM ** -0.5 * LOG2E)).astype(BF16)
        sk_ref[r, :] = rope(z[:, C_SK:C_SV])
        sv_ref[r, :] = z[:, C_SV:C_GA]


def _in_proj(x, nw, w, wg2, bg2, tabs, layer):
    n = x.shape[0]
    tile = min(n, IN_TILE)
    nt = n // tile
    cos, sa, sb = tabs
    ntab = cos.shape[0] // tile
    row = lambda w_: pl.BlockSpec((tile, w_), lambda i: (i, 0))
    tab = pl.BlockSpec((tile, LANES), lambda i: (i % ntab, 0))
    const = lambda shp: pl.BlockSpec((None,) + shp, lambda i: (layer, 0, 0), pipeline_mode=pl.Buffered(1))
    outs = [(GLA_QK, F32), (GLA_QK, F32), (GLA_QK, BF16), (GLA_QK, BF16), (GLA_WIDTH, BF16), (GLA_WIDTH, F32),
            (SWA_Q, BF16), (SWA_KV, F32), (SWA_KV, F32)]
    return pl.pallas_call(
        _in_proj_kernel,
        grid=(nt,),
        in_specs=[row(D_MODEL), const((1, D_MODEL)), const((w.shape[1], D_MODEL)),
                  const((LANES, GLA_QK)), const((1, GLA_QK)), tab, tab, tab],
        out_specs=[row(w_) for w_, _ in outs],
        out_shape=[jax.ShapeDtypeStruct((n, w_), dt) for w_, dt in outs],
        compiler_params=pltpu.CompilerParams(dimension_semantics=("arbitrary",),
                                             vmem_limit_bytes=VMEM_LIMIT),
        name="in_proj",
    )(x, nw, w, wg2, bg2, cos, sa, sb)


def _gla_kernel(q_ref, k_ref, ghi_ref, glo_ref, v_ref, bias_ref, o_ref, sout_ref, s_ref):
    ti = pl.program_id(1)
    C, SB = GLA_CHUNK, GLA_SUB
    NB = C // SB
    NP = GLA_HEADS // 2

    @pl.when(ti == 0)
    def _():
        s_ref[...] = jnp.zeros_like(s_ref)

    ltri = (lax.broadcasted_iota(jnp.int32, (C, 2 * C), 0) >= lax.broadcasted_iota(jnp.int32, (C, 2 * C), 1) % C
            ).astype(F32).astype(BF16)
    lane = lax.broadcasted_iota(jnp.int32, (1, LANES), 1)
    mask_a = (lane < GLA_DK).astype(F32)
    mask_b = 1.0 - mask_a
    col8 = lax.broadcasted_iota(jnp.int32, (SB, LANES), 1) % C
    colblk, colmod = col8 // SB, col8 % SB
    zk = jnp.zeros((GLA_DK, GLA_DV), BF16)
    off1 = [sum(C - SB * (jj + 1) for jj in range(j)) for j in range(NB - 1)]

    def sub_bcast(x, j):
        x3 = x.reshape(NB, SB, LANES)
        return jnp.broadcast_to(x3[:, j:j + 1, :], (NB, SB, LANES)).reshape(C, LANES)

    def heads(x):
        return jnp.concatenate([x * mask_a, x * mask_b], axis=0).astype(BF16)

    def body(it, carry):
        rows = [pl.ds(pl.multiple_of((it * GLA_UNROLL + cc) * C, C), C) for cc in range(GLA_UNROLL)]
        units = [(cc, p) for cc in range(GLA_UNROLL) for p in range(NP)]
        kl = [slice(p * LANES, (p + 1) * LANES) for p in range(NP)]
        vl = [slice(h * GLA_DV, (h + 1) * GLA_DV) for h in range(GLA_HEADS)]

        b_all = [_dot(ltri, jnp.concatenate([ghi_ref[rows[cc], :], glo_ref[rows[cc], :]], axis=0))
                 for cc in range(GLA_UNROLL)]

        qe, outs, dec, upd = {}, {}, {}, {}
        for cc, p in units:
            q, k, b = q_ref[rows[cc], kl[p]], k_ref[rows[cc], kl[p]], b_all[cc][:, kl[p]]
            b_last = b[C - 1:C, :]
            k1 = k * jnp.exp2(sub_bcast(b, SB - 1) - b)
            lhs1 = jnp.concatenate(
                [q[SB * (j + 1):, :] * jnp.exp2(b[SB * (j + 1):, :] - b[SB * j + SB - 1:SB * j + SB, :])
                 for j in range(NB - 1)], axis=0)
            lhs2 = jnp.concatenate(
                [q * jnp.exp2((b - sub_bcast(b, jj)) + jnp.tile(bias_ref[jj], (NB, 1))) for jj in range(SB)],
                axis=0)
            qe[cc, p] = (q * jnp.exp2(b)).astype(BF16)
            outs[cc, p] = (_dot_nt(lhs1.astype(BF16), heads(k1)), _dot_nt(lhs2.astype(BF16), heads(k)))
            kd = k * jnp.exp2(b_last - b)
            mt = jnp.concatenate([kd, jnp.broadcast_to(b_last, (C, LANES))], axis=0).T
            kdt = mt[:, :C].astype(BF16)
            dec[cc, p] = jnp.exp2(mt[:, C:C + 1])
            upd[cc, p] = (_dot(kdt[:GLA_DK], v_ref[rows[cc], vl[2 * p]]),
                          _dot(kdt[GLA_DK:], v_ref[rows[cc], vl[2 * p + 1]]))

        lo = {}
        for u in units:
            out1, out2 = outs[u]
            a_rows = []
            for i in range(NB):
                d = out2[(SB - 1) * C + SB * i:(SB - 1) * C + SB * (i + 1), :]
                for jj in range(SB - 2, -1, -1):
                    d = jnp.where(colmod == jj, out2[jj * C + SB * i:jj * C + SB * (i + 1), :], d)
                acc = jnp.where(colblk == i, d, 0.0)
                for j in range(i):
                    r0 = off1[j] + SB * (i - j - 1)
                    acc = jnp.where(colblk == j, out1[r0:r0 + SB, :], acc)
                a_rows.append(acc)
            a = jnp.concatenate(a_rows, axis=0)
            lo[u] = jnp.concatenate([qe[u], a.astype(BF16)], axis=1)

        state = [s_ref[h] for h in range(GLA_HEADS)]
        o_all = {}
        for cc in range(GLA_UNROLL):
            for p in range(NP):
                ha, hb = 2 * p, 2 * p + 1
                v_a, v_b = v_ref[rows[cc], vl[ha]], v_ref[rows[cc], vl[hb]]
                o_all[cc, ha] = _dot(lo[cc, p], jnp.concatenate([state[ha].astype(BF16), zk, v_a, zk], axis=0))
                o_all[cc, hb] = _dot(lo[cc, p], jnp.concatenate([zk, state[hb].astype(BF16), zk, v_b], axis=0))
                state[ha] = dec[cc, p][:GLA_DK] * state[ha] + upd[cc, p][0]
                state[hb] = dec[cc, p][GLA_DK:] * state[hb] + upd[cc, p][1]
        for h in range(GLA_HEADS):
            s_ref[h] = state[h]

        for cc in range(GLA_UNROLL):
            for h in range(GLA_HEADS):
                o_ref[rows[cc], vl[h]] = o_all[cc, h].astype(BF16)
        return carry

    lax.fori_loop(0, GLA_TILE // (C * GLA_UNROLL), body, 0)

    @pl.when(ti == pl.num_programs(1) - 1)
    def _():
        sout_ref[0] = s_ref[...]


def _gla_prompt(q, k, g_hi, g_lo, v, batch, seq):
    nt = seq // GLA_TILE
    blk = lambda w_: pl.BlockSpec((GLA_TILE, w_), lambda b, t: (b * nt + t, 0))
    r = np.arange(GLA_SUB)
    bias = np.where(r[None, :, None] >= r[:, None, None], 0.0, NEG) * np.ones((1, 1, LANES))
    return pl.pallas_call(
        _gla_kernel,
        grid=(batch, nt),
        in_specs=[blk(GLA_QK), blk(GLA_QK), blk(GLA_QK), blk(GLA_QK), blk(GLA_WIDTH),
                  pl.BlockSpec((GLA_SUB, GLA_SUB, LANES), lambda b, t: (0, 0, 0))],
        out_specs=[blk(GLA_WIDTH),
                   pl.BlockSpec((1, GLA_HEADS, GLA_DK, GLA_DV), lambda b, t: (b, 0, 0, 0))],
        out_shape=[jax.ShapeDtypeStruct((batch * seq, GLA_WIDTH), BF16),
                   jax.ShapeDtypeStruct((batch, GLA_HEADS, GLA_DK, GLA_DV), F32)],
        scratch_shapes=[pltpu.VMEM((GLA_HEADS, GLA_DK, GLA_DV), F32)],
        compiler_params=pltpu.CompilerParams(
            dimension_semantics=("arbitrary", "arbitrary"), vmem_limit_bytes=VMEM_LIMIT),
        name="gla_prompt",
    )(q, k, g_hi, g_lo, v, jnp.asarray(bias, F32))


def _out_ffn_kernel(x_ref, og_ref, gate_ref, gn_ref, os_ref, wo_ref, nw_ref, wg_ref, wu_ref, wd_ref, nf_ref,
                    y_ref, *, final):
    gn = gn_ref[...]
    og = jnp.concatenate([_rms(og_ref[:, h * GLA_DV:(h + 1) * GLA_DV].astype(F32), gn)
                          for h in range(GLA_HEADS)], axis=1) * gate_ref[...]
    x = (x_ref[...] + _dot(og.astype(BF16), wo_ref[:GLA_WIDTH, :]) + _dot(os_ref[...], wo_ref[GLA_WIDTH:, :]))
    h = _rms(x, nw_ref[...]).astype(BF16)
    a = _dot(h, wg_ref[...])
    u = _dot(h, wu_ref[...])
    acc = x + _dot((a * _sigmoid(a) * u).astype(BF16), wd_ref[...])
    if final:
        acc = _rms(acc, nf_ref[...])
    y_ref[...] = acc


def _out_ffn(x, og, gate, gn, osw, wo, nw, wg, wu, wd, nf, layer, final):
    n = x.shape[0]
    row = lambda w_: pl.BlockSpec((TOK_TILE, w_), lambda i: (i, 0))
    const = lambda shp: pl.BlockSpec((None,) + shp, lambda i: (layer, 0, 0), pipeline_mode=pl.Buffered(1))
    return pl.pallas_call(
        functools.partial(_out_ffn_kernel, final=final),
        grid=(n // TOK_TILE,),
        in_specs=[row(D_MODEL), row(GLA_WIDTH), row(GLA_WIDTH), const((1, GLA_DV)), row(SWA_Q),
                  const((D_MODEL, D_MODEL)), const((1, D_MODEL)),
                  const((D_MODEL, D_FF)), const((D_MODEL, D_FF)), const((D_FF, D_MODEL)),
                  pl.BlockSpec((1, D_MODEL), lambda i: (0, 0))],
        out_specs=row(D_MODEL),
        out_shape=jax.ShapeDtypeStruct((n, D_MODEL), F32),
        compiler_params=pltpu.CompilerParams(dimension_semantics=("arbitrary",),
                                             vmem_limit_bytes=VMEM_LIMIT),
        name="out_ffn",
    )(x, og, gate, gn, osw, wo, nw, wg, wu, wd, nf)


def _swa_setup(kp_ref, kc_ref, vp_ref, vc_ref, ke_scr, ve_scr):
    kk = jnp.concatenate([kp_ref[...], kc_ref[...]], axis=0)
    vv = jnp.concatenate([vp_ref[...], vc_ref[...]], axis=0)
    lo = lax.broadcasted_iota(jnp.int32, kk.shape, 1) < SWA_HEAD_DIM
    for x, scr in ((kk, ke_scr), (vv, ve_scr)):
        x_sw = pltpu.roll(x, SWA_HEAD_DIM, 1)
        scr[0, 0] = jnp.where(lo, x, 0.0).astype(BF16)
        scr[0, 1] = jnp.where(lo, 0.0, x_sw).astype(BF16)
        scr[1, 0] = jnp.where(lo, x_sw, 0.0).astype(BF16)
        scr[1, 1] = jnp.where(lo, 0.0, x).astype(BF16)


def _swa_scores(j, q_ref, ke_scr, s_scr):
    W = WINDOW
    for p in range(SWA_HEADS // 2):
        kv = p // (SWA_GROUP // 2)
        q = q_ref[j * W:(j + 1) * W, p * LANES:(p + 1) * LANES]
        for hh in range(2):
            s_scr[p, :, hh * 2 * W:(hh + 1) * 2 * W] = _dot_nt(q, ke_scr[kv, hh, j * W:(j + 2) * W, :])


def _swa_softmax(sink_ref, bias_blk, s_scr, p_scr, r_scr):
    W = WINDOW
    lane_q = lax.broadcasted_iota(jnp.int32, (SWA_ROWS, LANES), 1) < SWA_HEAD_DIM
    for p in range(SWA_HEADS // 2):
        for r0 in range(0, W, SWA_ROWS):
            rs = slice(r0, r0 + SWA_ROWS)
            rden = []
            for hh in range(2):
                cs = slice(hh * 2 * W, (hh + 1) * 2 * W)
                sink = sink_ref[2 * p + hh] * LOG2E
                sh = s_scr[p, rs, cs] + bias_blk[rs, :]
                m = jnp.maximum(jnp.max(sh, axis=-1, keepdims=True), sink)
                e = jnp.exp2(sh - m)
                den = jnp.sum(e, axis=-1, keepdims=True) + jnp.exp2(sink - m)
                p_scr[p, rs, cs] = e.astype(BF16)
                rden.append(1.0 / den)
            r_scr[p, rs, :] = jnp.where(lane_q, rden[0], rden[1])


def _swa_values(j, p_scr, ve_scr, r_scr, o_ref):
    W = WINDOW
    for p in range(SWA_HEADS // 2):
        kv = p // (SWA_GROUP // 2)
        o = (_dot(p_scr[p, :, :2 * W], ve_scr[kv, 0, j * W:(j + 2) * W, :])
             + _dot(p_scr[p, :, 2 * W:], ve_scr[kv, 1, j * W:(j + 2) * W, :])) * r_scr[p]
        o_ref[j * W:(j + 1) * W, p * LANES:(p + 1) * LANES] = o.astype(BF16)


def _ffn_swa_kernel(sink_ref, x_ref, og_ref, gate_ref, gn_ref, wo_ref, nw_ref, wg_ref, wu_ref, wd_ref, nf_ref,
                    osw0_ref, q_ref, kp_ref, kc_ref, vp_ref, vc_ref, bias_ref, y_ref,
                    osw_scr, ke_scr, ve_scr, s_scr, p_scr, r_scr, *, final, tiles_per_seq, n_tiles):
    s = pl.program_id(0)
    nblk = TOK_TILE // WINDOW

    @pl.when(s == 0)
    def _():
        osw_scr[0] = osw0_ref[...]

    _swa_setup(kp_ref, kc_ref, vp_ref, vc_ref, ke_scr, ve_scr)
    t_seq = jnp.minimum(s + 1, n_tiles - 1) % tiles_per_seq
    osw_cur = osw_scr.at[(s + 1) % 2]

    gn = gn_ref[...]
    og = jnp.concatenate([_rms(og_ref[:, h * GLA_DV:(h + 1) * GLA_DV].astype(F32), gn)
                          for h in range(GLA_HEADS)], axis=1) * gate_ref[...]
    x = (x_ref[...] + _dot(og.astype(BF16), wo_ref[:GLA_WIDTH, :])
         + _dot(osw_scr[s % 2], wo_ref[GLA_WIDTH:, :]))
    h = _rms(x, nw_ref[...]).astype(BF16)

    t_parts = []
    for j in range(nblk):
        c0, c1 = FF_SPLITS[j], FF_SPLITS[j + 1]
        _swa_scores(j, q_ref, ke_scr, s_scr)
        a = _dot(h, wg_ref[:, c0:c1])
        u = _dot(h, wu_ref[:, c0:c1])
        t_parts.append((a * _sigmoid(a) * u).astype(BF16))
        _swa_softmax(sink_ref, bias_ref.at[jnp.minimum(t_seq * nblk + j, 1)], s_scr, p_scr, r_scr)
        _swa_values(j, p_scr, ve_scr, r_scr, osw_cur)
    acc = x + _dot(jnp.concatenate(t_parts, axis=1), wd_ref[...])
    if final:
        acc = _rms(acc, nf_ref[...])
    y_ref[...] = acc


def _swa_first_kernel(sink_ref, q_ref, kp_ref, kc_ref, vp_ref, vc_ref, bias_ref, o_ref,
                      ke_scr, ve_scr, s_scr, p_scr, r_scr):
    _swa_setup(kp_ref, kc_ref, vp_ref, vc_ref, ke_scr, ve_scr)
    for j in range(TOK_TILE // WINDOW):
        _swa_scores(j, q_ref, ke_scr, s_scr)
        _swa_softmax(sink_ref, bias_ref.at[min(j, 1)], s_scr, p_scr, r_scr)
        _swa_values(j, p_scr, ve_scr, r_scr, o_ref)


def _ffn_swa(x, og, gate, gn, sinks, sq, sk, sv, wo, nw, wg, wu, wd, nf, layer, final, seq):
    n = x.shape[0]
    W = WINDOW
    n_tiles = n // TOK_TILE
    tiles_per_seq = seq // TOK_TILE
    nblk = TOK_TILE // W
    ffn_i = lambda s: (s, 0)
    swa_i = lambda s: (jnp.minimum(s + 1, n_tiles - 1), 0)

    def prev_i(s):
        t = jnp.minimum(s + 1, n_tiles - 1)
        return (t - t % tiles_per_seq) * nblk + jnp.maximum((t % tiles_per_seq) * nblk - 1, 0), 0

    row = lambda w_, im: pl.BlockSpec((TOK_TILE, w_), im)
    const = lambda shp: pl.BlockSpec((None,) + shp, lambda s: (layer, 0, 0), pipeline_mode=pl.Buffered(1))
    first = lambda s: (0, 0)
    qi, kj = np.arange(W)[:, None], np.arange(2 * W)[None, :]
    band = (kj > qi) & (kj <= qi + W)
    bias = jnp.asarray(np.stack([np.where(band & (kj >= W), 0.0, NEG), np.where(band, 0.0, NEG)]), F32)
    bias_spec = pl.BlockSpec((2, W, 2 * W), lambda s: (0, 0, 0))
    ext_scr = pltpu.VMEM((SWA_KV_HEADS, 2, W + TOK_TILE, LANES), BF16)
    npair = SWA_HEADS // 2
    swa_scr = [ext_scr, ext_scr, pltpu.VMEM((npair, W, 4 * W), F32), pltpu.VMEM((npair, W, 4 * W), BF16),
               pltpu.VMEM((npair, W, LANES), F32)]
    osw0 = pl.pallas_call(
        _swa_first_kernel,
        grid=(1,),
        in_specs=[pl.BlockSpec(memory_space=pltpu.SMEM), row(SWA_Q, first),
                  pl.BlockSpec((W, SWA_KV), first), row(SWA_KV, first),
                  pl.BlockSpec((W, SWA_KV), first), row(SWA_KV, first), bias_spec],
        out_specs=row(SWA_Q, first),
        out_shape=jax.ShapeDtypeStruct((TOK_TILE, SWA_Q), BF16),
        scratch_shapes=swa_scr,
        compiler_params=pltpu.CompilerParams(dimension_semantics=("arbitrary",), vmem_limit_bytes=VMEM_LIMIT),
        name="swa_first",
    )(sinks, sq, sk, sk, sv, sv, bias)
    return pl.pallas_call(
        functools.partial(_ffn_swa_kernel, final=final, tiles_per_seq=tiles_per_seq, n_tiles=n_tiles),
        grid=(n_tiles,),
        in_specs=[pl.BlockSpec(memory_space=pltpu.SMEM),
                  row(D_MODEL, ffn_i), row(GLA_WIDTH, ffn_i), row(GLA_WIDTH, ffn_i), const((1, GLA_DV)),
                  const((D_MODEL, D_MODEL)), const((1, D_MODEL)),
                  const((D_MODEL, D_FF)), const((D_MODEL, D_FF)), const((D_FF, D_MODEL)),
                  pl.BlockSpec((1, D_MODEL), lambda s: (0, 0)),
                  row(SWA_Q, first), row(SWA_Q, swa_i),
                  pl.BlockSpec((W, SWA_KV), prev_i), row(SWA_KV, swa_i),
                  pl.BlockSpec((W, SWA_KV), prev_i), row(SWA_KV, swa_i), bias_spec],
        out_specs=row(D_MODEL, ffn_i),
        out_shape=jax.ShapeDtypeStruct((n, D_MODEL), F32),
        scratch_shapes=[pltpu.VMEM((2, TOK_TILE, SWA_Q), BF16)] + swa_scr,
        compiler_params=pltpu.CompilerParams(dimension_semantics=("arbitrary",),
                                             vmem_limit_bytes=VMEM_LIMIT),
        name="ffn_swa",
    )(sinks, x, og, gate, gn, wo, nw, wg, wu, wd, nf, osw0, sq, sk, sk, sv, sv, bias)


def _sample_kernel(qkg_ref, v_ref, s0_ref,
                   qx_ref, kn_ref, vn_ref, ck_ref, cv_ref, sink_ref,
                   s_prev_ref, ck_prev_ref, cv_prev_ref,
                   og_ref, os_ref, sn_ref, cko_ref, cvo_ref, *, nt):
    del s_prev_ref, ck_prev_ref, cv_prev_ref
    TP = SUBLANES
    q, k, g, v = qkg_ref[0], qkg_ref[1], qkg_ref[2], v_ref[...]
    s0 = s0_ref[...]
    tt = lax.broadcasted_iota(jnp.int32, (1, TP, 1), 1)
    b = jnp.zeros_like(g)
    for s in range(nt):
        b = b + jnp.where(tt >= s, g[:, s:s + 1, :], 0.0)
    o = jnp.einsum('gtk,gkv->gtv', (q * jnp.exp2(b)).astype(BF16), s0.astype(BF16),
                   preferred_element_type=F32)
    for s in range(nt):
        dec = jnp.exp2(jnp.where(tt >= s, b - b[:, s:s + 1, :], NEG))
        a = jnp.sum(q * k[:, s:s + 1, :] * dec, axis=-1, keepdims=True)
        o = o + a * v[:, s:s + 1, :]
    og_ref[...] = o

    b_last = b[:, nt - 1:nt, :]
    d = jnp.exp2(b_last)
    d_hi = d.astype(BF16).astype(F32)
    x = jnp.where(tt < nt, k * jnp.exp2(b_last - b), jnp.where(tt == nt, d_hi, jnp.where(tt == nt + 1, d - d_hi, 0.0)))
    ones = jnp.where((tt == nt) | (tt == nt + 1), 1.0, 0.0)
    y = jnp.concatenate([jnp.where(tt < nt, v, 0.0), jnp.broadcast_to(ones, v.shape)], axis=2)
    r = lax.dot_general(x.astype(BF16), y.astype(BF16), (((1,), (1,)), ((0,), (0,))),
                        preferred_element_type=F32)
    sn_ref[...] = r[:, :, GLA_DV:] * s0 + r[:, :, :GLA_DV]

    W = WINDOW
    qx = qx_ref[...]
    ck, cv = ck_ref[...], cv_ref[...]
    kn, vn = kn_ref[...], vn_ref[...]
    sink = sink_ref[...] * LOG2E
    nrow = SWA_KV_HEADS * nt * SWA_GROUP
    r = lax.broadcasted_iota(jnp.int32, (1, nrow, W), 1)
    tq = (r % (nt * SWA_GROUP)) // SWA_GROUP
    j = lax.broadcasted_iota(jnp.int32, (1, nrow, W), 2)
    sc = jnp.einsum('bqd,bds->bqs', qx, ck.astype(BF16), preferred_element_type=F32)
    sc = jnp.where(j > tq, sc, NEG)
    tq1 = tq[:, :, 0:1]
    qf = qx.astype(F32)
    sn = [jnp.where(tq1 >= s, jnp.sum(qf * kn[:, s:s + 1, :], axis=-1, keepdims=True), NEG)
          for s in range(nt)]
    m = jnp.maximum(jnp.max(sc, axis=-1, keepdims=True), sink)
    for s in range(nt):
        m = jnp.maximum(m, sn[s])
    pc = jnp.exp2(sc - m)
    pn = [jnp.exp2(sn[s] - m) for s in range(nt)]
    den = jnp.sum(pc, axis=-1, keepdims=True) + jnp.exp2(sink - m)
    for s in range(nt):
        den = den + pn[s]
    o = jnp.einsum('bqs,bds->bqd', pc.astype(BF16), cv.astype(BF16), preferred_element_type=F32)
    for s in range(nt):
        o = o + pn[s] * vn[:, s:s + 1, :]
    os_ref[...] = o * (1.0 / den)

    wl = lax.broadcasted_iota(jnp.int32, (1, 1, W), 2)
    pr = lax.broadcasted_iota(jnp.int32, (1, 2 * nt, W), 1)
    pw = lax.broadcasted_iota(jnp.int32, (1, 2 * nt, W), 2)
    place = jnp.broadcast_to(jnp.where(pw == W - nt + pr % nt, 1.0, 0.0), (ck.shape[0], 2 * nt, W)).astype(BF16)
    for src, new, dst in ((ck, kn, cko_ref), (cv, vn, cvo_ref)):
        hi = new.astype(BF16).astype(F32)
        x = jnp.concatenate([hi, new - hi], axis=1).astype(BF16)
        placed = lax.dot_general(x, place, (((1,), (1,)), ((0,), (0,))), preferred_element_type=F32)
        dst[...] = jnp.where(wl >= W - nt, placed, pltpu.roll(src, W - nt, 2))


def _sample_mix(qkg4, v4, state, qx, kn, vn, ck, cv, sink_rows, carried, layer, nbatch, nt):
    bt = SAMPLE_BT
    G = bt * GLA_HEADS
    TP = SUBLANES
    nrow = SWA_KV_HEADS * nt * SWA_GROUP
    b3 = lambda a, c: pl.BlockSpec((G, a, c), lambda i: (i, 0, 0))
    s3 = lambda a, c: pl.BlockSpec((bt, a, c), lambda i: (i, 0, 0))
    lay_g = pl.BlockSpec((None, G, GLA_DK, GLA_DV), lambda i: (layer, i, 0, 0))
    lay_c = pl.BlockSpec((None, bt, SWA_KV, WINDOW), lambda i: (layer, i, 0, 0))
    ng = nbatch * GLA_HEADS
    n_in = 9
    return pl.pallas_call(
        functools.partial(_sample_kernel, nt=nt),
        grid=(nbatch // bt,),
        in_specs=[pl.BlockSpec((3, G, TP, GLA_DK), lambda i: (0, i, 0, 0)), b3(TP, GLA_DV), lay_g,
                  s3(nrow, SWA_KV), s3(nt, SWA_KV), s3(nt, SWA_KV), lay_c, lay_c,
                  pl.BlockSpec((1, nrow, 1), lambda i: (0, 0, 0))]
                 + [pl.BlockSpec(memory_space=pl.ANY)] * 3,
        out_specs=[b3(TP, GLA_DV), s3(nrow, SWA_KV), lay_g, lay_c, lay_c],
        out_shape=[jax.ShapeDtypeStruct((ng, TP, GLA_DV), F32),
                   jax.ShapeDtypeStruct((nbatch, nrow, SWA_KV), F32),
                   jax.ShapeDtypeStruct((DEPTH, ng, GLA_DK, GLA_DV), F32),
                   jax.ShapeDtypeStruct((DEPTH, nbatch, SWA_KV, WINDOW), F32),
                   jax.ShapeDtypeStruct((DEPTH, nbatch, SWA_KV, WINDOW), F32)],
        input_output_aliases={n_in + j: 2 + j for j in range(3)},
        compiler_params=pltpu.CompilerParams(dimension_semantics=("arbitrary",),
                                             vmem_limit_bytes=VMEM_LIMIT),
        name="sample_mix",
    )(qkg4, v4, state, qx, kn, vn, ck, cv, sink_rows, *carried)


def _rope_tables(pos):
    half = SWA_HEAD_DIM // 2
    inv = ROPE_THETA ** (-jnp.arange(half, dtype=F32) / half)
    ang = pos.astype(F32)[:, None] * inv[None, :]
    cos, sin = jnp.cos(ang), jnp.sin(ang)
    zero = jnp.zeros_like(sin)
    rep = LANES // SWA_HEAD_DIM
    return (jnp.tile(jnp.concatenate([cos, cos], axis=1), (1, rep)),
            jnp.tile(jnp.concatenate([-sin, zero], axis=1), (1, rep)),
            jnp.tile(jnp.concatenate([zero, sin], axis=1), (1, rep)))


def kernel(x_prompt, x_sample, state_gla, cache_swa_k, cache_swa_v, norm_attn, w_in, w_gk2, b_gk2,
           gla_norm, attn_sinks, w_o, norm_ffn, w_gate, w_up, w_down, norm_final):
    batch, seq, _ = x_prompt.shape
    nbatch, nt, _ = x_sample.shape
    n_s = nbatch * nt

    w_in_p = jnp.swapaxes(w_in, 1, 2).astype(BF16)
    wg2_p = jnp.concatenate([w_gk2, jnp.zeros((DEPTH, LANES - GATE_RANK, GLA_QK), w_gk2.dtype)],
                            axis=1).astype(BF16)
    bg2 = b_gk2.reshape(DEPTH, 1, GLA_QK)
    nw_a = norm_attn.reshape(DEPTH, 1, D_MODEL)
    nw_f = norm_ffn.reshape(DEPTH, 1, D_MODEL)
    wo_b, wg_b, wu_b, wd_b = (w.astype(BF16) for w in (w_o, w_gate, w_up, w_down))
    nf = norm_final.reshape(1, D_MODEL)
    gn = gla_norm.reshape(DEPTH, 1, GLA_DV)

    tabs_p = _rope_tables(jnp.arange(seq))
    tabs_s = _rope_tables(jnp.tile(PAST_LEN + jnp.arange(nt), TOK_TILE // nt))

    state_r = state_gla.reshape(DEPTH, nbatch * GLA_HEADS, GLA_DK, GLA_DV)
    pos_minor = lambda c: c.transpose(0, 1, 3, 4, 2).reshape(DEPTH, nbatch, SWA_KV, WINDOW)
    ck_r, cv_r = pos_minor(cache_swa_k), pos_minor(cache_swa_v)

    xp = x_prompt.reshape(batch * seq, D_MODEL)
    xs = x_sample.reshape(n_s, D_MODEL)
    sp_l, kp_l, vp_l = [], [], []
    carried = [jnp.zeros(state_r.shape, F32), jnp.zeros(ck_r.shape, F32), jnp.zeros(cv_r.shape, F32)]
    tpad = ((0, 0), (0, SUBLANES - nt), (0, 0))
    for l in range(DEPTH):
        final = l == DEPTH - 1
        q, k, g_hi, g_lo, v, gate, sq, sk, sv = _in_proj(xp, nw_a, w_in_p, wg2_p, bg2, tabs_p, l)
        og, s_fin = _gla_prompt(q, k, g_hi, g_lo, v, batch, seq)
        xp = _ffn_swa(xp, og, gate, gn, attn_sinks[l], sq, sk, sv, wo_b, nw_f, wg_b, wu_b, wd_b, nf, l, final, seq)
        sp_l.append(s_fin)
        last = lambda a: a.reshape(batch, seq, SWA_KV)[:, seq - WINDOW:].reshape(
            batch, WINDOW, SWA_KV_HEADS, SWA_HEAD_DIM)
        kp_l.append(last(sk))
        vp_l.append(last(sv))

        q, k, g_hi, g_lo, v, gate, sq, sk, sv = _in_proj(xs, nw_a, w_in_p, wg2_p, bg2, tabs_s, l)
        g = g_hi.astype(F32) + g_lo.astype(F32)

        def bhtk(a, w_):
            return a.reshape(nbatch, nt, GLA_HEADS, w_).transpose(0, 2, 1, 3).reshape(nbatch * GLA_HEADS, nt, w_)

        qkg4 = jnp.pad(jnp.stack([bhtk(a, GLA_DK) for a in (q, k, g)]), ((0, 0),) + tpad)
        v4 = jnp.pad(bhtk(v.astype(F32), GLA_DV), tpad)
        qk = sq.reshape(nbatch, nt, SWA_KV_HEADS, SWA_GROUP, SWA_HEAD_DIM).transpose(0, 2, 1, 3, 4)
        zq = jnp.zeros_like(qk[:, 0])
        qx = jnp.stack([jnp.concatenate([qk[:, 0], zq], axis=-1), jnp.concatenate([zq, qk[:, 1]], axis=-1)],
                       axis=1).reshape(nbatch, SWA_KV_HEADS * nt * SWA_GROUP, SWA_KV)
        sink_rows = jnp.broadcast_to(attn_sinks[l].reshape(SWA_KV_HEADS, 1, SWA_GROUP),
                                     (SWA_KV_HEADS, nt, SWA_GROUP)).reshape(1, -1, 1)
        og4, os4, *carried = _sample_mix(
            qkg4, v4, state_r, qx, sk.reshape(nbatch, nt, SWA_KV), sv.reshape(nbatch, nt, SWA_KV),
            ck_r, cv_r, sink_rows, carried, l, nbatch, nt)
        og = og4[:, :nt].reshape(nbatch, GLA_HEADS, nt, GLA_DV).transpose(0, 2, 1, 3).reshape(n_s, GLA_WIDTH)
        os5 = os4.reshape(nbatch, SWA_KV_HEADS, nt, SWA_GROUP, SWA_KV)
        osw = jnp.stack([os5[:, 0, :, :, :SWA_HEAD_DIM], os5[:, 1, :, :, SWA_HEAD_DIM:]], axis=2)
        osw = osw.reshape(n_s, SWA_Q)
        xs = _out_ffn(xs, og.astype(BF16), gate, gn, osw.astype(BF16), wo_b, nw_f, wg_b, wu_b, wd_b, nf, l, final)

    s_all, ck_all, cv_all = carried
    pos_major = lambda c: c.reshape(DEPTH, nbatch, SWA_KV_HEADS, SWA_HEAD_DIM, WINDOW).transpose(0, 1, 4, 2, 3)
    return (xp.reshape(batch, seq, D_MODEL), xs.reshape(nbatch, nt, D_MODEL),
            jnp.stack(sp_l), jnp.stack(kp_l), jnp.stack(vp_l),
            s_all.reshape(DEPTH, nbatch, GLA_HEADS, GLA_DK, GLA_DV), pos_major(ck_all), pos_major(cv_all))
```
